```python
import jax
import jax.numpy as jnp
from jax import lax
import numpy as np

D_MODEL = 2048
BATCH = 1
SEQ = 8192
DEPTH = 2

GRID_W = 64
CTX_LEN = 256
NORM_EPS = 1e-6

RWKV_HEAD = 64
RWKV_W = D_MODEL // 4
RWKV_HEADS = RWKV_W // RWKV_HEAD
DECAY_LORA = 64
ICLR_LORA = 64
GATE_LORA = 128
RWKV_GN_EPS = 64e-5
RWKV_SPLIT = (RWKV_W, RWKV_W, RWKV_W, DECAY_LORA, DECAY_LORA, ICLR_LORA, ICLR_LORA, GATE_LORA)
RWKV_PROJ = sum(RWKV_SPLIT)

GM_W = D_MODEL // 4
GM_GROUPS = 8
GM_CHUNK = 128
GM_PROJ = 2 * GM_W

ATTN_HEAD = 64
ATTN_W = D_MODEL // 2
ATTN_HEADS = ATTN_W // ATTN_HEAD
ATTN_KV_HEADS = ATTN_HEADS // 4
ATTN_KV_W = ATTN_KV_HEADS * ATTN_HEAD
ATTN_WINDOW = 128
ATTN_BLOCK = 128
ROPE_THETA = 10000.0
ATTN_PROJ = ATTN_W + 2 * ATTN_KV_W

IN_PROJ = RWKV_PROJ + GM_PROJ + ATTN_PROJ
MIX_W = RWKV_W + GM_W + ATTN_W

N_EXPERTS = 64
EXPERT_FF = 512
SHARED_FF = 512
TOP_K = 8
N_EXPERT_GROUPS = 8
TOPK_GROUPS = 4
ROUTED_SCALE = 2.5
EXPERT_BLOCK = 128

kernel_name = 'hybrid_rwkv7_gmlp_swa_moe_dit'


def split_cols(t, widths):
    return jnp.split(t, [int(i) for i in np.cumsum(widths)[:-1]], axis=-1)


def rms_norm(x, g):
    xf = x.astype(jnp.float32)
    y = xf * lax.rsqrt(jnp.mean(xf * xf, axis=-1, keepdims=True) + NORM_EPS)
    return (y * g.astype(jnp.float32)).astype(x.dtype)


def modulate(h, shift, scale):
    return h * (1.0 + scale) + shift


def swiglu(h, wg, wu, wd):
    return (jax.nn.silu(h @ wg) * (h @ wu)) @ wd


def centred_shift(p, mu_prev, mu_next):
    prev = jnp.pad(p, ((0, 0), (1, 0), (0, 0)))[:, :-1]
    nxt = jnp.pad(p, ((0, 0), (0, 1), (0, 0)))[:, 1:]
    return p + mu_prev * (prev - p) + mu_next * (nxt - p)


def rwkv7_scan(s0, r, decay, k, v, kk, a, reverse, emit):
    def step(S, inp):
        r_t, w_t, k_t, v_t, kk_t, a_t = inp
        sa = jnp.einsum('bhij,bhj->bhi', S, -kk_t)
        S = (S * w_t[:, :, None, :] + sa[..., None] * (kk_t * a_t)[:, :, None, :]
             + v_t[..., None] * k_t[:, :, None, :])
        y = jnp.einsum('bhij,bhj->bhi', S, r_t) if emit else None
        return S, y
    xs = tuple(jnp.swapaxes(t, 0, 1) for t in (r, decay, k, v, kk, a))
    s_fin, y = lax.scan(step, s0, xs, reverse=reverse)
    return s_fin, (jnp.swapaxes(y, 0, 1) if emit else None)


def rwkv7_mixer(p, s0_fwd, s0_bwd, prm, readout):
    out_dtype = p.dtype
    p = centred_shift(p.astype(jnp.float32), prm['mu'][0], prm['mu'][1])
    r, k, v, wd_f, wd_b, ad_f, ad_b, gd = split_cols(p, RWKV_SPLIT)
    B, T, _ = r.shape

    def heads(t):
        return t.reshape(B, T, RWKV_HEADS, RWKV_HEAD)

    kk = heads(k * prm['k_k'])
    kk = kk / jnp.maximum(jnp.sqrt(jnp.sum(kk * kk, axis=-1, keepdims=True)), 1e-12)
    ys, states = [], []
    for d, (wd, ad, s0) in enumerate(((wd_f, ad_f, s0_fwd), (wd_b, ad_b, s0_bwd))):
        w_log = -jax.nn.softplus(-(prm['w0'][d] + jnp.tanh(wd) @ prm['w2'][d])) - 0.5
        decay = jnp.exp(-jnp.exp(w_log))
        a = jax.nn.sigmoid(prm['a0'][d] + ad @ prm['a2'][d])
        k_d = k * (1.0 + (a - 1.0) * prm['k_a'])
        s_fin, y = rwkv7_scan(s0, heads(r), heads(decay), heads(k_d), heads(v), kk, heads(a),
                              reverse=(d == 1), emit=readout)
        ys.append(y)
        states.append(s_fin)
    if not readout:
        return None, states[0], states[1]
    y = ys[0] + ys[1]
    mean = jnp.mean(y, axis=-1, keepdims=True)
    var = jnp.mean(jnp.square(y - mean), axis=-1, keepdims=True)
    y = ((y - mean) * lax.rsqrt(var + RWKV_GN_EPS)).reshape(B, T, RWKV_W) * prm['ln_g'] + prm['ln_b']
    bonus = (jnp.sum(heads(r) * heads(k) * prm['r_k'], axis=-1, keepdims=True) * heads(v)).reshape(B, T, RWKV_W)
    g = jax.nn.sigmoid(gd) @ prm['g2']
    return ((y + bonus) * g).astype(out_dtype), states[0], states[1]


def chunk_gmlp(p, norm_g, ws, bs):
    u, v = jnp.split(p, 2, axis=-1)
    u = jax.nn.gelu(u)
    v = rms_norm(jax.nn.gelu(v), norm_g)
    B, T, C = v.shape
    vr = v.reshape(B, T // GM_CHUNK, GM_CHUNK, GM_GROUPS, C // GM_GROUPS)
    s = jnp.einsum('gpq,bnqgc->bnpgc', ws, vr) + bs.T[:, :, None]
    return u * s.reshape(B, T, C)


def attn_q(p, q_g):
    B, T, _ = p.shape
    return rms_norm(p[..., :ATTN_W].reshape(B, T, ATTN_HEADS, ATTN_HEAD), q_g)


def attn_kv(p, k_g):
    B, T, _ = p.shape
    k = p[..., ATTN_W:ATTN_W + ATTN_KV_W].reshape(B, T, ATTN_KV_HEADS, ATTN_HEAD)
    v = p[..., ATTN_W + ATTN_KV_W:].reshape(B, T, ATTN_KV_HEADS, ATTN_HEAD)
    return rms_norm(k, k_g), v


def axial_rope_angles(n_tokens):
    rows = n_tokens // GRID_W
    row = jnp.repeat(jnp.arange(rows, dtype=jnp.float32), GRID_W)
    col = jnp.tile(jnp.arange(GRID_W, dtype=jnp.float32), rows)
    n_freq = ATTN_HEAD // 4
    inv_freq = ROPE_THETA ** (-jnp.arange(n_freq, dtype=jnp.float32) / n_freq)
    return row[:, None] * inv_freq, col[:, None] * inv_freq


def rotate_half_pairs(x, ang):
    x1, x2 = jnp.split(x, 2, axis=-1)
    cos = jnp.cos(ang)[None, :, None, :]
    sin = jnp.sin(ang)[None, :, None, :]
    return jnp.concatenate([x1 * cos - x2 * sin, x2 * cos + x1 * sin], axis=-1)


def axial_rope(x, ang_row, ang_col):
    xr, xc = jnp.split(x.astype(jnp.float32), 2, axis=-1)
    return jnp.concatenate([rotate_half_pairs(xr, ang_row), rotate_half_pairs(xc, ang_col)], axis=-1).astype(x.dtype)


def sink_column(sink, B, kvh, g, n_q):
    return jnp.broadcast_to(sink.reshape(kvh, g)[None, :, :, None, None].astype(jnp.float32), (B, kvh, g, n_q, 1))


def windowed_attention(q, k, v, k_ctx, v_ctx, sink):
    B, S, H, D = q.shape
    KVH = k.shape[2]
    G = H // KVH
    C = k_ctx.shape[1]
    L = ATTN_BLOCK
    nb = S // L
    scale = D ** -0.5
    qb = jnp.moveaxis(q.reshape(B, nb, L, KVH, G, D), 1, 0)

    def band(t):
        tp = jnp.pad(t, ((0, 0), (L, L), (0, 0), (0, 0))).reshape(B, nb + 2, L, KVH, D)
        w = jnp.concatenate([tp[:, :-2], tp[:, 1:-1], tp[:, 2:]], axis=2)
        return jnp.moveaxis(w, 1, 0)

    kb, vb = band(k), band(v)
    qi = jnp.arange(L)[:, None]
    kj = jnp.arange(3 * L)[None, :]
    key_pos = (jnp.arange(nb)[:, None, None] - 1) * L + kj
    valid = (jnp.abs(kj - L - qi) <= ATTN_WINDOW) & (key_pos >= 0) & (key_pos < S)
    sink_b = sink_column(sink, B, KVH, G, L)
    kc = k_ctx.astype(jnp.float32)
    vc = v_ctx.astype(jnp.float32)

    def block(args):
        q_blk, k_blk, v_blk, m_blk = args
        qf = q_blk.astype(jnp.float32) * scale
        s_loc = jnp.einsum('bqkgd,bskd->bkgqs', qf, k_blk.astype(jnp.float32))
        s_loc = jnp.where(m_blk[None, None, None], s_loc, -jnp.inf)
        s_ctx = jnp.einsum('bqkgd,bskd->bkgqs', qf, kc)
        prob = jax.nn.softmax(jnp.concatenate([s_loc, s_ctx, sink_b], axis=-1), axis=-1)
        out = (jnp.einsum('bkgqs,bskd->bqkgd', prob[..., :3 * L], v_blk.astype(jnp.float32))
               + jnp.einsum('bkgqs,bskd->bqkgd', prob[..., 3 * L:3 * L + C], vc))
        return out.astype(q.dtype)

    out = lax.map(block, (qb, kb, vb, valid))
    return jnp.moveaxis(out, 0, 1).reshape(B, S, H * D)


def context_attention(q, k, v, sink):
    B, C, H, D = q.shape
    KVH = k.shape[2]
    G = H // KVH
    qf = q.reshape(B, C, KVH, G, D).astype(jnp.float32) * D ** -0.5
    s = jnp.einsum('bqkgd,bskd->bkgqs', qf, k.astype(jnp.float32))
    prob = jax.nn.softmax(jnp.concatenate([s, sink_column(sink, B, KVH, G, C)], axis=-1), axis=-1)[..., :C]
    out = jnp.einsum('bkgqs,bskd->bqkgd', prob, v.astype(jnp.float32))
    return out.reshape(B, C, H * D).astype(q.dtype)


def routed_experts(h, idx, wts, wg, wu, wd):
    n_tok, d = h.shape
    n_assign = idx.shape[0] * idx.shape[1]
    flat_e = idx.reshape(-1)
    order = jnp.argsort(flat_e)
    sorted_e = flat_e[order]
    counts = jnp.bincount(flat_e, length=N_EXPERTS)
    padded = (counts + EXPERT_BLOCK - 1) // EXPERT_BLOCK * EXPERT_BLOCK
    pad_end = jnp.cumsum(padded)
    pad_start = pad_end - padded
    seg_start = jnp.cumsum(counts) - counts
    dest = pad_start[sorted_e] + jnp.arange(n_assign) - seg_start[sorted_e]
    n_blocks = -(-n_assign // EXPERT_BLOCK) + N_EXPERTS
    cap = n_blocks * EXPERT_BLOCK
    slot_tok = jnp.zeros((cap,), jnp.int32).at[dest].set((order // idx.shape[1]).astype(jnp.int32))
    slot_w = jnp.zeros((cap,), wts.dtype).at[dest].set(wts.reshape(-1)[order])
    block_e = jnp.minimum(jnp.searchsorted(pad_end, jnp.arange(n_blocks) * EXPERT_BLOCK, side='right'), N_EXPERTS - 1)

    def run_block(args):
        tok_b, w_b, e = args
        return swiglu(h[tok_b], wg[e], wu[e], wd[e]) * w_b[:, None]

    y = lax.map(run_block, (slot_tok.reshape(n_blocks, EXPERT_BLOCK), slot_w.reshape(n_blocks, EXPERT_BLOCK), block_e))
    return jax.ops.segment_sum(y.reshape(cap, d), slot_tok, num_segments=n_tok)


def moe_ffn(h, router, bias, wg, wu, wd, sh_wg, sh_wu, sh_wd):
    n_tok = h.shape[0]
    scores = jax.nn.sigmoid(h.astype(jnp.float32) @ router.astype(jnp.float32))
    biased = scores + bias.astype(jnp.float32)
    per_group = N_EXPERTS // N_EXPERT_GROUPS
    group_score = jnp.sum(lax.top_k(biased.reshape(n_tok, N_EXPERT_GROUPS, per_group), 2)[0], axis=-1)
    top_groups = lax.top_k(group_score, TOPK_GROUPS)[1]
    group_mask = jnp.any(top_groups[:, :, None] == jnp.arange(N_EXPERT_GROUPS), axis=1)
    masked = jnp.where(jnp.repeat(group_mask, per_group, axis=1), biased, -jnp.inf)
    idx = lax.top_k(masked, TOP_K)[1]
    wts = jnp.take_along_axis(scores, idx, axis=1)
    wts = wts / jnp.sum(wts, axis=-1, keepdims=True) * ROUTED_SCALE
    return swiglu(h, sh_wg, sh_wu, sh_wd) + routed_experts(h, idx, wts.astype(h.dtype), wg, wu, wd)


def setup_inputs(seed: int = 0) -> dict:
    key = jax.random.key(seed)
    ks = iter(list(jax.random.split(key, 35)))
    f32 = jnp.float32
    L, D = DEPTH, D_MODEL

    def nrm(shape, scale=1.0):
        return jax.random.normal(next(ks), shape, f32) * scale

    def near(shape, center, noise):
        return center + nrm(shape, noise)

    def unif(shape, lo, hi):
        return jax.random.uniform(next(ks), shape, f32, lo, hi)

    return {
        'x': nrm((BATCH, SEQ, D)),
        'c': nrm((BATCH, D)),
        'ctx': nrm((BATCH, CTX_LEN, D)),
        'c_ctx': nrm((D,)),
        'w_ada': nrm((L, D, 6 * D), 0.5 * D ** -0.5),
        'b_ada': nrm((L, 6 * D), 0.01),
        'norm1_g': near((L, D), 1.0, 0.01),
        'norm2_g': near((L, D), 1.0, 0.01),
        'w_in': nrm((L, D, IN_PROJ), D ** -0.5),
        'w_o': nrm((L, MIX_W, D), MIX_W ** -0.5),
        'rw_mu': unif((L, 2, RWKV_PROJ), 0.0, 0.5),
        'rw_w0': unif((L, 2, RWKV_W), -6.0, 1.0),
        'rw_w2': nrm((L, 2, DECAY_LORA, RWKV_W), 0.5 * DECAY_LORA ** -0.5),
        'rw_a0': nrm((L, 2, RWKV_W), 0.1),
        'rw_a2': nrm((L, 2, ICLR_LORA, RWKV_W), 0.5 * ICLR_LORA ** -0.5),
        'rw_g2': nrm((L, GATE_LORA, RWKV_W), GATE_LORA ** -0.5),
        'rw_kk': near((L, RWKV_W), 0.85, 0.02),
        'rw_ka': near((L, RWKV_W), 1.0, 0.02),
        'rw_rk': nrm((L, RWKV_HEADS, RWKV_HEAD), 0.1),
        'rw_ln_g': near((L, RWKV_W), 1.0, 0.01),
        'rw_ln_b': nrm((L, RWKV_W), 0.01),
        'gm_norm_g': near((L, GM_W), 1.0, 0.01),
        'gm_ws': nrm((L, GM_GROUPS, GM_CHUNK, GM_CHUNK), 0.5 * GM_CHUNK ** -0.5),
        'gm_b': near((L, GM_GROUPS, GM_CHUNK), 1.0, 0.01),
        'at_qn': near((L, ATTN_HEAD), 1.0, 0.01),
        'at_kn': near((L, ATTN_HEAD), 1.0, 0.01),
        'at_sink': nrm((L, ATTN_HEADS), 1.0),
        'moe_router': nrm((L, D, N_EXPERTS), D ** -0.5),
        'moe_bias': nrm((L, N_EXPERTS), 0.01),
        'moe_wg': nrm((L, N_EXPERTS, D, EXPERT_FF), D ** -0.5),
        'moe_wu': nrm((L, N_EXPERTS, D, EXPERT_FF), D ** -0.5),
        'moe_wd': nrm((L, N_EXPERTS, EXPERT_FF, D), EXPERT_FF ** -0.5),
        'sh_wg': nrm((L, D, SHARED_FF), D ** -0.5),
        'sh_wu': nrm((L, D, SHARED_FF), D ** -0.5),
        'sh_wd': nrm((L, SHARED_FF, D), SHARED_FF ** -0.5),
    }


def reference(x, c, ctx, c_ctx, w_ada, b_ada, norm1_g, norm2_g, w_in, w_o,
              rw_mu, rw_w0, rw_w2, rw_a0, rw_a2, rw_g2, rw_kk, rw_ka, rw_rk, rw_ln_g, rw_ln_b,
              gm_norm_g, gm_ws, gm_b, at_qn, at_kn, at_sink,
              moe_router, moe_bias, moe_wg, moe_wu, moe_wd, sh_wg, sh_wu, sh_wd):
    B, S, D = x.shape
    C = ctx.shape[1]
    ang_row, ang_col = axial_rope_angles(S)
    cond = jax.nn.silu(c)
    cond_ctx = jax.nn.silu(c_ctx)
    split_pts = [RWKV_PROJ, RWKV_PROJ + GM_PROJ]
    for l in range(DEPTH):
        last = l == DEPTH - 1
        sh1, sc1, g1, sh2, sc2, g2 = [t[:, None, :] for t in jnp.split(cond @ w_ada[l] + b_ada[l], 6, axis=-1)]
        csh1, csc1, cg1, csh2, csc2, cg2 = jnp.split(cond_ctx @ w_ada[l] + b_ada[l], 6, axis=-1)

        h = modulate(rms_norm(x, norm1_g[l]), sh1, sc1)
        hc = modulate(rms_norm(ctx, norm1_g[l]), csh1, csc1)
        p_rw, p_gm, p_at = jnp.split(h @ w_in[l], split_pts, axis=-1)
        pc_rw, pc_gm, pc_at = jnp.split(hc @ w_in[l], split_pts, axis=-1)

        rw = {'mu': rw_mu[l], 'w0': rw_w0[l], 'w2': rw_w2[l], 'a0': rw_a0[l], 'a2': rw_a2[l],
              'g2': rw_g2[l], 'k_k': rw_kk[l], 'k_a': rw_ka[l], 'r_k': rw_rk[l],
              'ln_g': rw_ln_g[l], 'ln_b': rw_ln_b[l]}
        s0 = jnp.zeros((B, RWKV_HEADS, RWKV_HEAD, RWKV_HEAD), jnp.float32)
        yc_rw, s_fwd, s_bwd = rwkv7_mixer(pc_rw, s0, s0, rw, readout=not last)
        y_rw, _, _ = rwkv7_mixer(p_rw, s_fwd, s_bwd, rw, readout=True)

        y_gm = chunk_gmlp(p_gm, gm_norm_g[l], gm_ws[l], gm_b[l])

        kc, vc = attn_kv(pc_at, at_kn[l])
        q = axial_rope(attn_q(p_at, at_qn[l]), ang_row, ang_col)
        k, v = attn_kv(p_at, at_kn[l])
        k = axial_rope(k, ang_row, ang_col)
        y_at = windowed_attention(q, k, v, kc, vc, at_sink[l])

        x = x + g1 * (jnp.concatenate([y_rw, y_gm, y_at], axis=-1) @ w_o[l])

        moe_p = (moe_router[l], moe_bias[l], moe_wg[l], moe_wu[l], moe_wd[l], sh_wg[l], sh_wu[l], sh_wd[l])
        h2 = modulate(rms_norm(x, norm2_g[l]), sh2, sc2).reshape(B * S, D)
        if last:
            x = x + g2 * moe_ffn(h2, *moe_p).reshape(B, S, D)
        else:
            yc_gm = chunk_gmlp(pc_gm, gm_norm_g[l], gm_ws[l], gm_b[l])
            yc_at = context_attention(attn_q(pc_at, at_qn[l]), kc, vc, at_sink[l])
            ctx = ctx + cg1 * (jnp.concatenate([yc_rw, yc_gm, yc_at], axis=-1) @ w_o[l])
            hc2 = modulate(rms_norm(ctx, norm2_g[l]), csh2, csc2).reshape(B * C, D)
            f = moe_ffn(jnp.concatenate([hc2, h2], axis=0), *moe_p)
            ctx = ctx + cg2 * f[:B * C].reshape(B, C, D)
            x = x + g2 * f[B * C:].reshape(B, S, D)
    return x
```

```python
import functools

import jax
import jax.numpy as jnp
import numpy as np
from jax import lax
from jax.experimental import pallas as pl
from jax.experimental.pallas import tpu as pltpu

F32 = jnp.float32
BF16 = jnp.bfloat16
HI = lax.Precision.HIGHEST

NORM_EPS = 1e-6
GRID_W = 64

RWKV_HEAD = 64
DECAY_LORA = 64
ICLR_LORA = 64
GATE_LORA = 128
RWKV_GN_EPS = 64e-5

GM_GROUPS = 8
GM_CHUNK = 128

ATTN_HEAD = 64
ATTN_GROUP = 4
ATTN_WINDOW = 128
ATTN_BLOCK = 128
ROPE_THETA = 10000.0

N_EXPERTS = 64
TOP_K = 8
N_EXPERT_GROUPS = 8
TOPK_GROUPS = 4
ROUTED_SCALE = 2.5

SCAN_CHUNK = 64
EXPERT_TILE = 256
VMEM_LIMIT = 56 * 1024 * 1024


def _cp(sem, vmem=VMEM_LIMIT):
    return pltpu.CompilerParams(dimension_semantics=sem, vmem_limit_bytes=vmem)


def _dot(a, b, prec=None):
    return jnp.dot(a, b, preferred_element_type=F32, precision=prec)


def _dot_nt(a, b, prec=None):
    return lax.dot_general(a, b, (((1,), (1,)), ((), ())), preferred_element_type=F32, precision=prec)


def _dot_tn(a, b, prec=None):
    return lax.dot_general(a, b, (((0,), (0,)), ((), ())), preferred_element_type=F32, precision=prec)


def _seg_sum(x, g):
    xh = x.astype(BF16)
    xl = (x - xh.astype(F32)).astype(BF16)
    return _dot(xh, g) + _dot(xl, g)


def _sigmoid(x):
    return jax.nn.sigmoid(x)


def _norm_mod(x, g, sh, sc):
    y = x * lax.rsqrt(jnp.mean(x * x, axis=-1, keepdims=True) + NORM_EPS)
    return (y * g) * (1.0 + sc) + sh


def _ctx_rows(i, tm, n_ctx):
    row = i * tm + lax.broadcasted_iota(jnp.int32, (tm, 1), 0)
    return row < n_ctx


def _pick_mod(mod_ref, is_ctx):
    return jnp.where(is_ctx, mod_ref[0, 0:1, :], mod_ref[0, 1:2, :])


def _ada_kernel(c_ref, w_ref, b_ref, o_ref):
    c = c_ref[...]
    s = c * _sigmoid(c)
    o_ref[0] = _dot(s, w_ref[0], HI) + b_ref[0]


def _ada(cond8, w_ada, b_ada):
    L, D, N = w_ada.shape
    tn = 1024
    return pl.pallas_call(
        _ada_kernel,
        grid=(L, N // tn),
        in_specs=[pl.BlockSpec((8, D), lambda l, j: (0, 0)),
                  pl.BlockSpec((1, D, tn), lambda l, j: (l, 0, j)),
                  pl.BlockSpec((1, 1, tn), lambda l, j: (l, 0, j))],
        out_specs=pl.BlockSpec((1, 8, tn), lambda l, j: (l, 0, j)),
        out_shape=jax.ShapeDtypeStruct((L, 8, N), F32),
        compiler_params=_cp(("arbitrary", "arbitrary")),
        name="ada",
    )(cond8, w_ada, b_ada.reshape(L, 1, N))


def _inproj_kernel(x_ref, sh_ref, sc_ref, g_ref, w_ref, o_ref, h_scr, *, tm, n_ctx):
    i = pl.program_id(0)

    @pl.when(pl.program_id(1) == 0)
    def _():
        is_ctx = _ctx_rows(i, tm, n_ctx)
        h = _norm_mod(x_ref[...], g_ref[...], _pick_mod(sh_ref, is_ctx), _pick_mod(sc_ref, is_ctx))
        h_scr[...] = h.astype(BF16)

    o_ref[...] = _dot(h_scr[...], w_ref[...])


def _inproj(x, mods, l, g, w, n_ctx, tm):
    T, D = x.shape
    N = w.shape[1]
    tn = 512
    return pl.pallas_call(
        functools.partial(_inproj_kernel, tm=tm, n_ctx=n_ctx),
        grid=(T // tm, N // tn),
        in_specs=[pl.BlockSpec((tm, D), lambda i, j: (i, 0)),
                  pl.BlockSpec((1, 8, D), lambda i, j: (l, 0, 0)),
                  pl.BlockSpec((1, 8, D), lambda i, j: (l, 0, 1)),
                  pl.BlockSpec((1, D), lambda i, j: (0, 0)),
                  pl.BlockSpec((D, tn), lambda i, j: (0, j))],
        out_specs=pl.BlockSpec((tm, tn), lambda i, j: (i, j)),
        out_shape=jax.ShapeDtypeStruct((T, N), F32),
        scratch_shapes=[pltpu.VMEM((tm, D), BF16)],
        compiler_params=_cp(("arbitrary", "arbitrary")),
        name="inproj",
    )(x, mods, mods, g, w)


def _softplus(x):
    return jnp.maximum(x, 0.0) + jnp.log(1.0 + jnp.exp(-jnp.abs(x)))


def _rwkv_prep_kernel(p_ref, pp_ref, pn_ref, mu_ref, w0_ref, w2_ref, a0_ref, a2_ref, g2_ref,
                      kkp_ref, kap_ref, rkp_ref, gh_ref,
                      r_ref, v_ref, kk_ref, lwf_ref, kf_ref, bf_ref, lwb_ref, kb_ref, bb_ref,
                      gate_ref, bonus_ref, *, tm, n_ctx, n_tot, W):
    i = pl.program_id(0)
    p = p_ref[...]
    lrow = lax.broadcasted_iota(jnp.int32, (tm, 1), 0)
    grow = i * tm + lrow
    prev = jnp.where(lrow == 0, pp_ref[7:8, :], pltpu.roll(p, 1, axis=0))
    prev = jnp.where((grow == 0) | (grow == n_ctx), 0.0, prev)
    nxt = jnp.where(lrow == tm - 1, pn_ref[0:1, :], pltpu.roll(p, tm - 1, axis=0))
    nxt = jnp.where((grow == n_ctx - 1) | (grow == n_tot - 1), 0.0, nxt)
    ps = p + mu_ref[0:1, :] * (prev - p) + mu_ref[1:2, :] * (nxt - p)

    r = ps[:, 0:W]
    k = ps[:, W:2 * W]
    v = ps[:, 2 * W:3 * W]
    o = 3 * W
    wd = (ps[:, o:o + DECAY_LORA], ps[:, o + DECAY_LORA:o + 2 * DECAY_LORA])
    o += 2 * DECAY_LORA
    ad = (ps[:, o:o + ICLR_LORA], ps[:, o + ICLR_LORA:o + 2 * ICLR_LORA])
    o += 2 * ICLR_LORA
    gd = ps[:, o:o + GATE_LORA]

    gh = gh_ref[...]
    kk = k * kkp_ref[...]
    kk = kk / jnp.maximum(jnp.sqrt(_seg_sum(kk * kk, gh)), 1e-12)
    r_ref[...] = r
    v_ref[...] = v
    kk_ref[...] = kk
    outs = ((lwf_ref, kf_ref, bf_ref), (lwb_ref, kb_ref, bb_ref))
    for d in range(2):
        z = w0_ref[d:d + 1, :] + _dot(jnp.tanh(wd[d]), w2_ref[d], HI)
        w_log = -_softplus(-z) - 0.5
        a = _sigmoid(a0_ref[d:d + 1, :] + _dot(ad[d], a2_ref[d], HI))
        lw_ref, kd_ref, bd_ref = outs[d]
        lw_ref[...] = -jnp.exp(w_log)
        kd_ref[...] = k * (1.0 + (a - 1.0) * kap_ref[...])
        bd_ref[...] = kk * a
    gate_ref[...] = _dot(_sigmoid(gd), g2_ref[...], HI)
    bonus_ref[...] = _seg_sum(r * k * rkp_ref[...], gh) * v


def _rwkv_prep(P, prm, gh, n_ctx, tm):
    T = P.shape[0]
    W = prm["kk"].shape[1]
    PW = 2048
    nb8 = T // 8
    row = pl.BlockSpec((tm, W), lambda i: (i, 0))
    full = lambda a: pl.BlockSpec(a.shape, lambda i: (0,) * a.ndim)
    args = (prm["mu"], prm["w0"], prm["w2"], prm["a0"], prm["a2"], prm["g2"], prm["kk"], prm["ka"], prm["rk"], gh)
    return pl.pallas_call(
        functools.partial(_rwkv_prep_kernel, tm=tm, n_ctx=n_ctx, n_tot=T, W=W),
        grid=(T // tm,),
        in_specs=[pl.BlockSpec((tm, PW), lambda i: (i, 0)),
                  pl.BlockSpec((8, PW), lambda i: (jnp.maximum(i * (tm // 8) - 1, 0), 0)),
                  pl.BlockSpec((8, PW), lambda i: (jnp.minimum((i + 1) * (tm // 8), nb8 - 1), 0))]
                 + [full(a) for a in args],
        out_specs=[row] * 11,
        out_shape=[jax.ShapeDtypeStruct((T, W), F32)] * 11,
        compiler_params=_cp(("arbitrary",)),
        name="rwkv_prep",
    )(P, P, P, *args)


def _rwkv_scan_kernel(rf, vf, kkf, lwf, kf, bf, rb, vb, kkb, lwb, kb, bb, yf_ref, yb_ref, s_scr, *, C, H, N):
    @pl.when(pl.program_id(0) == 0)
    def _():
        s_scr[...] = jnp.zeros_like(s_scr)

    row = lax.broadcasted_iota(jnp.int32, (C, C), 0)
    col = lax.broadcasted_iota(jnp.int32, (C, C), 1)
    eye = (row == col).astype(F32)
    n_sq = int(np.log2(C)) - 1
    dirs = ((rf, vf, kkf, lwf, kf, bf, yf_ref), (rb, vb, kkb, lwb, kb, bb, yb_ref))
    for d, (r_ref, v_ref, kk_ref, lw_ref, k_ref, b_ref, y_ref) in enumerate(dirs):
        incl = (col <= row) if d == 0 else (col >= row)
        strict = (col < row) if d == 0 else (col > row)
        lw = lw_ref[...]
        cum = _dot(incl.astype(F32), lw, HI)
        e_pos = jnp.exp(cum)
        e_neg = jnp.exp(-cum)
        rt = (r_ref[...] * e_pos).astype(BF16)
        at = (-kk_ref[...] * jnp.exp(cum - lw)).astype(BF16)
        bt = (b_ref[...] * e_neg).astype(BF16)
        kt = (k_ref[...] * e_neg).astype(BF16)
        vv = v_ref[...].astype(BF16)
        gam = e_pos[C - 1:C, :] if d == 0 else e_pos[0:1, :]
        ys = []
        for h in range(H):
            sl = slice(h * N, (h + 1) * N)
            at_h, rt_h, bt_h, kt_h, v_h = at[:, sl], rt[:, sl], bt[:, sl], kt[:, sl], vv[:, sl]
            a_all = _dot_nt(jnp.concatenate([at_h, rt_h], axis=0), jnp.concatenate([bt_h, kt_h], axis=0))
            l_ab = jnp.where(strict, a_all[:C, :C], 0.0)
            l_ak = jnp.where(strict, a_all[:C, C:], 0.0)
            m_rb = jnp.where(incl, a_all[C:, :C], 0.0)
            m_rk = jnp.where(incl, a_all[C:, C:], 0.0)
            t_inv = eye + l_ab
            l_pow = l_ab
            for _ in range(n_sq):
                lb = l_pow.astype(BF16)
                l_pow = _dot(lb, lb)
                t_inv = t_inv + _dot(t_inv.astype(BF16), l_pow.astype(BF16))
            s0 = s_scr[d, h]
            s0b = s0.astype(BF16)
            w1 = _dot_nt(at_h, s0b) + _dot(l_ak.astype(BF16), v_h)
            u = _dot(t_inv.astype(BF16), w1.astype(BF16))
            ub = u.astype(BF16)
            ys.append(_dot_nt(rt_h, s0b) + _dot(m_rb.astype(BF16), ub) + _dot(m_rk.astype(BF16), v_h))
            s_scr[d, h] = (s0 + _dot_tn(ub, bt_h) + _dot_tn(v_h, kt_h)) * gam[:, sl]
        y_ref[...] = jnp.concatenate(ys, axis=1)


def _rwkv_scan(r, v, kk, lwf, kf, bf, lwb, kb, bb, n_ctx):
    T, W = r.shape
    C = SCAN_CHUNK
    H = W // RWKV_HEAD
    nch = T // C
    cch = n_ctx // C
    fwd = pl.BlockSpec((C, W), lambda n: (n, 0))
    bwd = pl.BlockSpec((C, W), lambda n: (jnp.where(n < cch, cch - 1 - n, nch - 1 + cch - n), 0))
    return pl.pallas_call(
        functools.partial(_rwkv_scan_kernel, C=C, H=H, N=RWKV_HEAD),
        grid=(nch,),
        in_specs=[fwd] * 6 + [bwd] * 6,
        out_specs=[fwd, bwd],
        out_shape=[jax.ShapeDtypeStruct((T, W), F32)] * 2,
        scratch_shapes=[pltpu.VMEM((2, H, RWKV_HEAD, RWKV_HEAD), F32)],
        compiler_params=_cp(("arbitrary",)),
        name="rwkv_scan",
    )(r, v, kk, lwf, kf, bf, r, v, kk, lwb, kb, bb)


def _rwkv_post_kernel(yf_ref, yb_ref, bonus_ref, gate_ref, lng_ref, lnb_ref, gh_ref, o_ref):
    gh = gh_ref[...]
    y = yf_ref[...] + yb_ref[...]
    mean = _seg_sum(y, gh) * (1.0 / RWKV_HEAD)
    yc = y - mean
    var = _seg_sum(yc * yc, gh) * (1.0 / RWKV_HEAD)
    yn = yc * lax.rsqrt(var + RWKV_GN_EPS) * lng_ref[...] + lnb_ref[...]
    o_ref[...] = ((yn + bonus_ref[...]) * gate_ref[...]).astype(o_ref.dtype)


def _rwkv_post(yf, yb, bonus, gate, lng, lnb, gh, tm):
    T, W = yf.shape
    row = pl.BlockSpec((tm, W), lambda i: (i, 0))
    full = lambda a: pl.BlockSpec(a.shape, lambda i: (0,) * a.ndim)
    return pl.pallas_call(
        _rwkv_post_kernel,
        grid=(T // tm,),
        in_specs=[row] * 4 + [full(lng), full(lnb), full(gh)],
        out_specs=row,
        out_shape=jax.ShapeDtypeStruct((T, W), BF16),
        compiler_params=_cp(("arbitrary",)),
        name="rwkv_post",
    )(yf, yb, bonus, gate, lng, lnb, gh)


def _gmlp_kernel(p_ref, g_ref, ws_ref, b_ref, o_ref, *, tm, W):
    u = jax.nn.gelu(p_ref[:, 0:W])
    v = jax.nn.gelu(p_ref[:, W:2 * W])
    v = v * lax.rsqrt(jnp.mean(v * v, axis=-1, keepdims=True) + NORM_EPS) * g_ref[...]
    vb = v.astype(BF16)
    gw = W // GM_GROUPS
    for c in range(tm // GM_CHUNK):
        rows = slice(c * GM_CHUNK, (c + 1) * GM_CHUNK)
        parts = [_dot(ws_ref[g], vb[rows, g * gw:(g + 1) * gw]) for g in range(GM_GROUPS)]
        s = jnp.concatenate(parts, axis=1) + b_ref[...]
        o_ref[rows, :] = (u[rows, :] * s).astype(o_ref.dtype)


def _gmlp(P, col_block, g, ws, bias, tm):
    T = P.shape[0]
    W = g.shape[1]
    return pl.pallas_call(
        functools.partial(_gmlp_kernel, tm=tm, W=W),
        grid=(T // tm,),
        in_specs=[pl.BlockSpec((tm, 2 * W), lambda i: (i, col_block)),
                  pl.BlockSpec((1, W), lambda i: (0, 0)),
                  pl.BlockSpec(ws.shape, lambda i: (0, 0, 0)),
                  pl.BlockSpec(bias.shape, lambda i: (0, 0))],
        out_specs=pl.BlockSpec((tm, W), lambda i: (i, 0)),
        out_shape=jax.ShapeDtypeStruct((T, W), BF16),
        compiler_params=_cp(("arbitrary",)),
        name="gmlp",
    )(P, g, ws, bias)


def _rope(x, cos, sin, lane):
    w = x.shape[1]
    partner = jnp.where((lane % 32) < 16, pltpu.roll(x, w - 16, axis=1), pltpu.roll(x, 16, axis=1))
    return x * cos + partner * sin


def _qk_prep_kernel(q_ref, k_ref, cos_ref, sin_ref, qg_ref, kg_ref, ghq_ref, ghk_ref, qo_ref, ko_ref, *, scale):
    cos = cos_ref[...]
    sin = sin_ref[...]
    for x_ref, g_ref, gh_ref, o_ref, mul in ((q_ref, qg_ref, ghq_ref, qo_ref, scale), (k_ref, kg_ref, ghk_ref, ko_ref, 1.0)):
        x = x_ref[...]
        w = x.shape[1]
        ss = _seg_sum(x * x, gh_ref[...]) * (1.0 / ATTN_HEAD)
        xn = x * lax.rsqrt(ss + NORM_EPS) * g_ref[...]
        rep = w // cos.shape[1]
        lane = lax.broadcasted_iota(jnp.int32, x.shape, 1)
        xr = _rope(xn, jnp.tile(cos, (1, rep)), jnp.tile(sin, (1, rep)), lane)
        o_ref[...] = (xr * mul).astype(o_ref.dtype)


def _qk_prep(P, q_block, k_block, cos, sin, qg, kg, ghq, ghk, tm):
    T = P.shape[0]
    QW, KW = qg.shape[1], kg.shape[1]
    full = lambda a: pl.BlockSpec(a.shape, lambda i: (0,) * a.ndim)
    return pl.pallas_call(
        functools.partial(_qk_prep_kernel, scale=ATTN_HEAD ** -0.5),
        grid=(T // tm,),
        in_specs=[pl.BlockSpec((tm, QW), lambda i: (i, q_block)),
                  pl.BlockSpec((tm, KW), lambda i: (i, k_block)),
                  pl.BlockSpec((tm, cos.shape[1]), lambda i: (i, 0)),
                  pl.BlockSpec((tm, sin.shape[1]), lambda i: (i, 0)),
                  full(qg), full(kg), full(ghq), full(ghk)],
        out_specs=[pl.BlockSpec((tm, QW), lambda i: (i, 0)), pl.BlockSpec((tm, KW), lambda i: (i, 0))],
        out_shape=[jax.ShapeDtypeStruct((T, QW), BF16), jax.ShapeDtypeStruct((T, KW), BF16)],
        compiler_params=_cp(("arbitrary",)),
        name="qk_prep",
    )(P, P, cos, sin, qg, kg, ghq, ghk)


def _attn_block(i, nb, sink_ref, q_ref, k_refs, v_refs, o_ref, local):
    L = ATTN_BLOCK
    G = ATTN_GROUP
    hd = ATTN_HEAD
    n_kv = k_refs[-1].shape[1] // hd
    R = G * L
    srow = lax.broadcasted_iota(jnp.int32, (R, 1), 0)
    if local:
        qi = lax.broadcasted_iota(jnp.int32, (R, 3 * L), 0) % L
        kj = lax.broadcasted_iota(jnp.int32, (R, 3 * L), 1)
        rel = kj - L - qi
        valid = (rel <= ATTN_WINDOW) & (rel >= -ATTN_WINDOW)
        valid = valid & ((kj >= L) | (i > 0)) & ((kj < 2 * L) | (i < nb - 1))
    for j in range(n_kv):
        ks = [r[:, j * hd:(j + 1) * hd] for r in k_refs]
        vs = [r[:, j * hd:(j + 1) * hd].astype(BF16) for r in v_refs]
        q = jnp.concatenate([q_ref[:, (j * G + g) * hd:(j * G + g + 1) * hd] for g in range(G)], axis=0)
        sink = jnp.zeros((R, 1), F32)
        for g in range(G):
            sink = jnp.where((srow >= g * L) & (srow < (g + 1) * L), sink_ref[j * G + g], sink)
        s_ctx = _dot_nt(q, ks[-1])
        m = jnp.maximum(jnp.max(s_ctx, axis=1, keepdims=True), sink)
        if local:
            s_loc = _dot_nt(q, jnp.concatenate(ks[:3], axis=0))
            s_loc = jnp.where(valid, s_loc, -1e30)
            m = jnp.maximum(m, jnp.max(s_loc, axis=1, keepdims=True))
            p_loc = jnp.exp(s_loc - m)
        p_ctx = jnp.exp(s_ctx - m)
        den = jnp.sum(p_ctx, axis=1, keepdims=True) + jnp.exp(sink - m)
        acc = _dot(p_ctx.astype(BF16), vs[-1])
        if local:
            den = den + jnp.sum(p_loc, axis=1, keepdims=True)
            acc = acc + _dot(p_loc.astype(BF16), jnp.concatenate(vs[:3], axis=0))
        out = acc / den
        for g in range(G):
            h = j * G + g
            o_ref[:, h * hd:(h + 1) * hd] = out[g * L:(g + 1) * L, :].astype(o_ref.dtype)


def _attn_kernel(sink_ref, q_ref, kp_ref, kc_ref, kn_ref, kx_ref, vp_ref, vc_ref, vn_ref, vx_ref, o_ref, *, cb, nb):
    i = pl.program_id(0)

    @pl.when(i < cb)
    def _():
        _attn_block(i, nb, sink_ref, q_ref, (kx_ref,), (vx_ref,), o_ref, False)

    @pl.when(i >= cb)
    def _():
        _attn_block(i - cb, nb, sink_ref, q_ref, (kp_ref, kc_ref, kn_ref, kx_ref),
                    (vp_ref, vc_ref, vn_ref, vx_ref), o_ref, True)


def _attention(qr, kr, P, v_block, sink, n_ctx):
    T, QW = qr.shape
    KW = kr.shape[1]
    L = ATTN_BLOCK
    cb = n_ctx // L
    nb = (T - n_ctx) // L
    lo, hi = cb, cb + nb - 1
    shifts = (lambda i: jnp.clip(i - 1, lo, hi), lambda i: jnp.clip(i, lo, hi), lambda i: jnp.clip(i + 1, lo, hi))
    kspec = lambda f: pl.BlockSpec((L, KW), lambda i: (f(i), 0))
    vspec = lambda f: pl.BlockSpec((L, KW), lambda i: (f(i), v_block))
    in_specs = ([pl.BlockSpec(memory_space=pltpu.SMEM), pl.BlockSpec((L, QW), lambda i: (i, 0))]
                + [kspec(f) for f in shifts] + [pl.BlockSpec((n_ctx, KW), lambda i: (0, 0))]
                + [vspec(f) for f in shifts] + [pl.BlockSpec((n_ctx, KW), lambda i: (0, v_block))])
    return pl.pallas_call(
        functools.partial(_attn_kernel, cb=cb, nb=nb),
        grid=(cb + nb,),
        in_specs=in_specs,
        out_specs=pl.BlockSpec((L, QW), lambda i: (i, 0)),
        out_shape=jax.ShapeDtypeStruct((T, QW), BF16),
        compiler_params=_cp(("arbitrary",)),
        name="attn",
    )(sink, qr, kr, kr, kr, kr, P, P, P, P)


def _oproj_kernel(x_ref, gt_ref, a_ref, b_ref, c_ref, wa_ref, wb_ref, wc_ref, o_ref, *, tm, n_ctx):
    acc = _dot(a_ref[...], wa_ref[...]) + _dot(b_ref[...], wb_ref[...]) + _dot(c_ref[...], wc_ref[...])
    gate = _pick_mod(gt_ref, _ctx_rows(pl.program_id(0), tm, n_ctx))
    o_ref[...] = x_ref[...] + gate * acc


def _oproj(x, mods, l, y_rw, y_gm, y_at, w_o, n_ctx, tm):
    T, D = x.shape
    tn = 512
    W1 = y_rw.shape[1]
    W3 = y_at.shape[1]
    gate_col = 2 * (D // tn)
    return pl.pallas_call(
        functools.partial(_oproj_kernel, tm=tm, n_ctx=n_ctx),
        grid=(T // tm, D // tn),
        in_specs=[pl.BlockSpec((tm, tn), lambda i, j: (i, j)),
                  pl.BlockSpec((1, 8, tn), lambda i, j: (l, 0, gate_col + j)),
                  pl.BlockSpec((tm, W1), lambda i, j: (i, 0)),
                  pl.BlockSpec((tm, W1), lambda i, j: (i, 0)),
                  pl.BlockSpec((tm, W3), lambda i, j: (i, 0)),
                  pl.BlockSpec((W1, tn), lambda i, j: (0, j)),
                  pl.BlockSpec((W1, tn), lambda i, j: (1, j)),
                  pl.BlockSpec((W3, tn), lambda i, j: (1, j))],
        out_specs=pl.BlockSpec((tm, tn), lambda i, j: (i, j)),
        out_shape=jax.ShapeDtypeStruct((T, D), F32),
        compiler_params=_cp(("arbitrary", "arbitrary")),
        name="oproj",
    )(x, mods, y_rw, y_gm, y_at, w_o, w_o, w_o)


def _router_kernel(x_ref, sh_ref, sc_ref, g_ref, rt_ref, bias_ref, tri_ref,
                   h_ref, idx_ref, wt_ref, rank_ref, cnt_ref, carry, *, tm, n_ctx):
    i = pl.program_id(0)

    @pl.when(i == 0)
    def _():
        carry[...] = jnp.zeros_like(carry)

    is_ctx = _ctx_rows(i, tm, n_ctx)
    h = _norm_mod(x_ref[...], g_ref[...], _pick_mod(sh_ref, is_ctx), _pick_mod(sc_ref, is_ctx))
    h_ref[...] = h
    E = N_EXPERTS
    pg = E // N_EXPERT_GROUPS
    neg = -jnp.inf
    scores = _sigmoid(_dot_nt(rt_ref[...], h, HI))
    biased = scores + bias_ref[:, 0:1]
    b3 = biased.reshape(N_EXPERT_GROUPS, pg, tm)
    i3 = lax.broadcasted_iota(jnp.int32, b3.shape, 1)
    m1 = jnp.max(b3, axis=1, keepdims=True)
    first = jnp.min(jnp.where(b3 == m1, i3, pg), axis=1, keepdims=True)
    m2 = jnp.max(jnp.where(i3 == first, neg, b3), axis=1, keepdims=True)
    gs = (m1 + m2).reshape(N_EXPERT_GROUPS, tm)
    gi = lax.broadcasted_iota(jnp.int32, gs.shape, 0)
    gsel = jnp.zeros(gs.shape, jnp.bool_)
    for _ in range(TOPK_GROUPS):
        gm = jnp.max(gs, axis=0, keepdims=True)
        gfirst = jnp.min(jnp.where(gs == gm, gi, N_EXPERT_GROUPS), axis=0, keepdims=True)
        hit = gi == gfirst
        gsel = gsel | hit
        gs = jnp.where(hit, neg, gs)
    masked = jnp.where(gsel.reshape(N_EXPERT_GROUPS, 1, tm), b3, neg).reshape(E, tm)
    ei = lax.broadcasted_iota(jnp.int32, (E, tm), 0)
    sel = jnp.zeros((E, tm), jnp.bool_)
    picks = []
    for _ in range(TOP_K):
        mx = jnp.max(masked, axis=0, keepdims=True)
        efirst = jnp.min(jnp.where(masked == mx, ei, E), axis=0, keepdims=True)
        hit = ei == efirst
        sel = sel | hit
        masked = jnp.where(hit, neg, masked)
        picks.append((efirst, hit, jnp.sum(jnp.where(hit, scores, 0.0), axis=0, keepdims=True)))
    wsum = picks[0][2]
    for pk in picks[1:]:
        wsum = wsum + pk[2]
    self_f = jnp.where(sel, 1.0, 0.0)
    rank_dense = carry[:, 0:1] + _dot(self_f.astype(BF16), tri_ref[...])
    carry[...] = carry[...] + jnp.sum(self_f, axis=1, keepdims=True)
    cnt_ref[...] = carry[...]
    for kx, (efirst, hit, wk) in enumerate(picks):
        idx_ref[kx:kx + 1, :] = efirst
        wt_ref[kx:kx + 1, :] = wk / wsum * ROUTED_SCALE
        rank_ref[kx:kx + 1, :] = jnp.sum(jnp.where(hit, rank_dense, 0.0), axis=0, keepdims=True).astype(jnp.int32)


def _router(x, mods, l, g, router_t, bias, tri, n_ctx, tm):
    T, D = x.shape
    E = router_t.shape[0]
    kspec = pl.BlockSpec((TOP_K, tm), lambda i: (0, i))
    return pl.pallas_call(
        functools.partial(_router_kernel, tm=tm, n_ctx=n_ctx),
        grid=(T // tm,),
        in_specs=[pl.BlockSpec((tm, D), lambda i: (i, 0)),
                  pl.BlockSpec((1, 8, D), lambda i: (l, 0, 3)),
                  pl.BlockSpec((1, 8, D), lambda i: (l, 0, 4)),
                  pl.BlockSpec((1, D), lambda i: (0, 0)),
                  pl.BlockSpec((E, D), lambda i: (0, 0)),
                  pl.BlockSpec((E, 128), lambda i: (0, 0)),
                  pl.BlockSpec((tm, tm), lambda i: (0, 0))],
        out_specs=[pl.BlockSpec((tm, D), lambda i: (i, 0)), kspec, kspec, kspec,
                   pl.BlockSpec((E, 128), lambda i: (0, 0))],
        out_shape=[jax.ShapeDtypeStruct((T, D), F32),
                   jax.ShapeDtypeStruct((TOP_K, T), jnp.int32),
                   jax.ShapeDtypeStruct((TOP_K, T), F32),
                   jax.ShapeDtypeStruct((TOP_K, T), jnp.int32),
                   jax.ShapeDtypeStruct((E, 128), F32)],
        scratch_shapes=[pltpu.VMEM((E, 128), F32)],
        compiler_params=_cp(("arbitrary",)),
        name="router",
    )(x, mods, mods, g, router_t, bias, tri)


def _gather_rows(idx_ref, n_rows, src_hbm, dst, sem):
    def body(r, carry):
        pltpu.make_async_copy(src_hbm.at[pl.ds(idx_ref[r], 1), :], dst.at[pl.ds(r, 1), :], sem).start()
        return carry
    lax.fori_loop(0, n_rows, body, 0)


def _experts_kernel(be_ref, nu_ref, tokc_ref, tokn_ref, h_hbm, wg_ref, wu_ref, wd_ref, y_ref,
                    hbuf, sem, wg_s, wu_s, wd_s, *, TB, nblk):
    b = pl.program_id(0)
    slot = b % 2
    n_used = nu_ref[0]

    @pl.when(b == 0)
    def _():
        _gather_rows(tokc_ref.at[0, 0], TB, h_hbm, hbuf.at[0], sem.at[0])

    @pl.when(b + 1 < n_used)
    def _():
        _gather_rows(tokn_ref.at[0, 0], TB, h_hbm, hbuf.at[1 - slot], sem.at[1 - slot])

    new_expert = (b == 0) | (be_ref[b] != be_ref[jnp.maximum(b - 1, 0)])

    @pl.when(new_expert & (b < n_used))
    def _():
        wg_s[...] = wg_ref[0].astype(BF16)
        wu_s[...] = wu_ref[0].astype(BF16)
        wd_s[...] = wd_ref[0].astype(BF16)

    @pl.when(b < n_used)
    def _():
        pltpu.make_async_copy(h_hbm.at[pl.ds(0, TB), :], hbuf.at[slot], sem.at[slot]).wait()
        hb = hbuf[slot].astype(BF16)
        a = _dot(hb, wg_s[...])
        u = _dot(hb, wu_s[...])
        act = (a * _sigmoid(a) * u).astype(BF16)
        y_ref[...] = _dot(act, wd_s[...])

    @pl.when(b >= n_used)
    def _():
        y_ref[...] = jnp.zeros_like(y_ref)


def _experts(h2, tok, block_e, n_used, wg, wu, wd):
    T, D = h2.shape
    nblk = block_e.shape[0]
    TB = EXPERT_TILE
    E, _, FF = wg.shape
    tok3 = tok.reshape(nblk, 1, TB)
    grid_spec = pltpu.PrefetchScalarGridSpec(
        num_scalar_prefetch=2,
        grid=(nblk,),
        in_specs=[pl.BlockSpec((1, 1, TB), lambda b, be, nu: (b, 0, 0), memory_space=pltpu.SMEM),
                  pl.BlockSpec((1, 1, TB), lambda b, be, nu: (jnp.minimum(b + 1, nblk - 1), 0, 0),
                               memory_space=pltpu.SMEM),
                  pl.BlockSpec(memory_space=pl.ANY),
                  pl.BlockSpec((1, D, FF), lambda b, be, nu: (be[b], 0, 0)),
                  pl.BlockSpec((1, D, FF), lambda b, be, nu: (be[b], 0, 0)),
                  pl.BlockSpec((1, FF, D), lambda b, be, nu: (be[b], 0, 0))],
        out_specs=pl.BlockSpec((TB, D), lambda b, be, nu: (b, 0)),
        scratch_shapes=[pltpu.VMEM((2, TB, D), F32), pltpu.SemaphoreType.DMA((2,)),
                        pltpu.VMEM((D, FF), BF16), pltpu.VMEM((D, FF), BF16), pltpu.VMEM((FF, D), BF16)],
    )
    return pl.pallas_call(
        functools.partial(_experts_kernel, TB=TB, nblk=nblk),
        grid_spec=grid_spec,
        out_shape=jax.ShapeDtypeStruct((nblk * TB, D), F32),
        compiler_params=_cp(("arbitrary",)),
        name="experts",
    )(block_e, n_used, tok3, tok3, h2, wg, wu, wd)


def _combine_kernel(slc_ref, sln_ref, y_hbm, x_ref, h_ref, wt_ref, gt_ref, swg_ref, swu_ref, swd_ref, o_ref,
                    ybuf, sem, *, tm, n_ctx, nt):
    i = pl.program_id(0)
    slot = i % 2

    @pl.when(i == 0)
    def _():
        for k in range(TOP_K):
            _gather_rows(slc_ref.at[0, k], tm, y_hbm, ybuf.at[0, k], sem.at[0])

    @pl.when(i + 1 < nt)
    def _():
        for k in range(TOP_K):
            _gather_rows(sln_ref.at[0, k], tm, y_hbm, ybuf.at[1 - slot, k], sem.at[1 - slot])

    hb = h_ref[...].astype(BF16)
    a = _dot(hb, swg_ref[...])
    u = _dot(hb, swu_ref[...])
    acc = _dot((a * _sigmoid(a) * u).astype(BF16), swd_ref[...])
    for k in range(TOP_K):
        pltpu.make_async_copy(y_hbm.at[pl.ds(0, tm), :], ybuf.at[slot, k], sem.at[slot]).wait()
    for k in range(TOP_K):
        acc = acc + wt_ref[:, k:k + 1] * ybuf[slot, k]
    gate = _pick_mod(gt_ref, _ctx_rows(i, tm, n_ctx))
    o_ref[...] = x_ref[...] + gate * acc


def _combine(y, slot8, wt_t, x, h2, mods, l, swg, swu, swd, n_ctx, tm):
    T, D = x.shape
    nt = T // tm
    FF = swg.shape[1]
    sl3 = slot8.reshape(TOP_K, nt, tm).transpose(1, 0, 2)
    return pl.pallas_call(
        functools.partial(_combine_kernel, tm=tm, n_ctx=n_ctx, nt=nt),
        grid=(nt,),
        in_specs=[pl.BlockSpec((1, TOP_K, tm), lambda i: (i, 0, 0), memory_space=pltpu.SMEM),
                  pl.BlockSpec((1, TOP_K, tm), lambda i: (jnp.minimum(i + 1, nt - 1), 0, 0),
                               memory_space=pltpu.SMEM),
                  pl.BlockSpec(memory_space=pl.ANY),
                  pl.BlockSpec((tm, D), lambda i: (i, 0)),
                  pl.BlockSpec((tm, D), lambda i: (i, 0)),
                  pl.BlockSpec((tm, TOP_K), lambda i: (i, 0)),
                  pl.BlockSpec((1, 8, D), lambda i: (l, 0, 5)),
                  pl.BlockSpec((D, FF), lambda i: (0, 0)),
                  pl.BlockSpec((D, FF), lambda i: (0, 0)),
                  pl.BlockSpec((FF, D), lambda i: (0, 0))],
        out_specs=pl.BlockSpec((tm, D), lambda i: (i, 0)),
        out_shape=jax.ShapeDtypeStruct((T, D), F32),
        scratch_shapes=[pltpu.VMEM((2, TOP_K, tm, D), F32), pltpu.SemaphoreType.DMA((2,))],
        compiler_params=_cp(("arbitrary",)),
        name="combine",
    )(sl3, sl3, y, x, h2, wt_t, mods, swg, swu, swd)


def _seg_indicator(width, seg):
    idx = np.arange(width) // seg
    return jnp.asarray(idx[:, None] == idx[None, :], dtype=BF16)


def _rope_tables(n_ctx, n_lat):
    rows = n_lat // GRID_W
    row = jnp.repeat(jnp.arange(rows, dtype=F32), GRID_W)
    col = jnp.tile(jnp.arange(GRID_W, dtype=F32), rows)
    n_freq = ATTN_HEAD // 4
    inv_freq = ROPE_THETA ** (-jnp.arange(n_freq, dtype=F32) / n_freq)
    ar, ac = row[:, None] * inv_freq, col[:, None] * inv_freq
    cos = jnp.concatenate([jnp.cos(ar), jnp.cos(ar), jnp.cos(ac), jnp.cos(ac)], axis=1)
    sin = jnp.concatenate([-jnp.sin(ar), jnp.sin(ar), -jnp.sin(ac), jnp.sin(ac)], axis=1)
    cos = jnp.concatenate([jnp.ones((n_ctx, ATTN_HEAD), F32), cos], axis=0)
    sin = jnp.concatenate([jnp.zeros((n_ctx, ATTN_HEAD), F32), sin], axis=0)
    return jnp.tile(cos, (1, 2)), jnp.tile(sin, (1, 2))


def kernel(x, c, ctx, c_ctx, w_ada, b_ada, norm1_g, norm2_g, w_in, w_o, rw_mu, rw_w0, rw_w2, rw_a0, rw_a2, rw_g2, rw_kk, rw_ka, rw_rk, rw_ln_g, rw_ln_b, gm_norm_g, gm_ws, gm_b, at_qn, at_kn, at_sink, moe_router, moe_bias, moe_wg, moe_wu, moe_wd, sh_wg, sh_wu, sh_wd):
    B, S, D = x.shape
    C = ctx.shape[1]
    assert B == 1
    L = w_ada.shape[0]
    T = C + S
    RW = rw_kk.shape[1]
    GW = gm_norm_g.shape[1]
    QW = at_sink.shape[1] * ATTN_HEAD
    KW = QW // ATTN_GROUP
    rw_proj = rw_mu.shape[2]
    RWP = 2048
    assert rw_proj <= RWP and T % 768 == 0 and C % 256 == 0 and S % 256 == 0

    xs = jnp.concatenate([ctx[0], x[0]], axis=0)
    cond8 = jnp.zeros((8, D), F32).at[0].set(c_ctx).at[1].set(c[0])
    mods = _ada(cond8, w_ada, b_ada)

    gh_rw = _seg_indicator(RW, RWKV_HEAD)
    gh_q = _seg_indicator(QW, ATTN_HEAD)
    gh_k = _seg_indicator(KW, ATTN_HEAD)
    cos, sin = _rope_tables(C, S)
    tri = jnp.asarray(np.arange(256)[:, None] < np.arange(256)[None, :], dtype=BF16)
    TB = EXPERT_TILE
    nblk = -(-(T * TOP_K) // TB) + N_EXPERTS

    for l in range(L):
        w_in_l = w_in[l]
        w_in_p = jnp.concatenate([w_in_l[:, :rw_proj], jnp.zeros((D, RWP - rw_proj), F32), w_in_l[:, rw_proj:]],
                                 axis=1).astype(BF16)
        P = _inproj(xs, mods, l, norm1_g[l][None], w_in_p, C, 768)
        gm_block = RWP // (2 * GW)
        q_block = (RWP + 2 * GW) // QW
        k_block = (RWP + 2 * GW + QW) // KW
        v_block = k_block + 1

        pad = lambda a: jnp.pad(a, ((0, 0), (0, RWP - rw_proj)))
        prm = {"mu": pad(rw_mu[l]), "w0": rw_w0[l], "w2": rw_w2[l], "a0": rw_a0[l], "a2": rw_a2[l],
               "g2": rw_g2[l], "kk": rw_kk[l][None], "ka": rw_ka[l][None], "rk": rw_rk[l].reshape(1, RW)}
        r, v, kk, lwf, kf, bf, lwb, kb, bb, gate, bonus = _rwkv_prep(P, prm, gh_rw, C, 256)
        yf, yb = _rwkv_scan(r, v, kk, lwf, kf, bf, lwb, kb, bb, C)
        y_rw = _rwkv_post(yf, yb, bonus, gate, rw_ln_g[l][None], rw_ln_b[l][None], gh_rw, 256)

        gm_bias = jnp.repeat(gm_b[l].T, GW // GM_GROUPS, axis=1)
        y_gm = _gmlp(P, gm_block, gm_norm_g[l][None], gm_ws[l].astype(BF16), gm_bias, 256)

        qg = jnp.tile(at_qn[l], QW // ATTN_HEAD)[None]
        kg = jnp.tile(at_kn[l], KW // ATTN_HEAD)[None]
        qr, kr = _qk_prep(P, q_block, k_block, cos, sin, qg, kg, gh_q, gh_k, 256)
        y_at = _attention(qr, kr, P, v_block, at_sink[l], C)

        xs = _oproj(xs, mods, l, y_rw, y_gm, y_at, w_o[l].astype(BF16), C, 768)

        bias_col = jnp.broadcast_to(moe_bias[l][:, None], (N_EXPERTS, 128))
        h2, idx8, wt8, rank8, cnt = _router(xs, mods, l, norm2_g[l][None], moe_router[l].T, bias_col, tri, C, 256)
        counts = cnt[:, 0].astype(jnp.int32)
        padded = (counts + TB - 1) // TB * TB
        pad_end = jnp.cumsum(padded)
        pad_start = pad_end - padded
        slot8 = pad_start[idx8] + rank8
        tok = jnp.zeros((nblk * TB,), jnp.int32).at[slot8.reshape(-1)].set(
            jnp.tile(jnp.arange(T, dtype=jnp.int32), TOP_K), unique_indices=True)
        block_e = jnp.minimum(jnp.searchsorted(pad_end, jnp.arange(nblk, dtype=jnp.int32) * TB, side="right"),
                              N_EXPERTS - 1).astype(jnp.int32)
        n_used = (pad_end[-1] // TB).astype(jnp.int32).reshape(1)
        y = _experts(h2, tok, block_e, n_used, moe_wg[l], moe_wu[l], moe_wd[l])
        xs = _combine(y, slot8, wt8.T, xs, h2, mods, l, sh_wg[l].astype(BF16), sh_wu[l].astype(BF16),
                      sh_wd[l].astype(BF16), C, 128)
    return xs[C:].reshape(B, S, D)
```

```python
import functools

import jax
import jax.numpy as jnp
import numpy as np
from jax import lax
from jax.experimental import pallas as pl
from jax.experimental.pallas import tpu as pltpu

F32 = jnp.float32
BF16 = jnp.bfloat16
HI = lax.Precision.HIGHEST

NORM_EPS = 1e-6
GRID_W = 64

RWKV_HEAD = 64
DECAY_LORA = 64
ICLR_LORA = 64
GATE_LORA = 128
RWKV_GN_EPS = 64e-5

GM_GROUPS = 8
GM_CHUNK = 128

ATTN_HEAD = 64
ATTN_GROUP = 4
ATTN_WINDOW = 128
ATTN_BLOCK = 128
ROPE_THETA = 10000.0

N_EXPERTS = 64
TOP_K = 8
N_EXPERT_GROUPS = 8
TOPK_GROUPS = 4
ROUTED_SCALE = 2.5

SCAN_CHUNK = 64
EXPERT_TILE = 256
VMEM_LIMIT = 56 * 1024 * 1024


def _cp(sem, vmem=VMEM_LIMIT):
    return pltpu.CompilerParams(dimension_semantics=sem, vmem_limit_bytes=vmem)


def _dot(a, b, prec=None):
    return jnp.dot(a, b, preferred_element_type=F32, precision=prec)


def _dot_nt(a, b, prec=None):
    return lax.dot_general(a, b, (((1,), (1,)), ((), ())), preferred_element_type=F32, precision=prec)


def _dot_tn(a, b, prec=None):
    return lax.dot_general(a, b, (((0,), (0,)), ((), ())), preferred_element_type=F32, precision=prec)


def _seg_sum(x, g):
    xh = x.astype(BF16)
    xl = (x - xh.astype(F32)).astype(BF16)
    return _dot(xh, g) + _dot(xl, g)


def _sigmoid(x):
    return jax.nn.sigmoid(x)


def _norm_mod(x, g, sh, sc):
    y = x * lax.rsqrt(jnp.mean(x * x, axis=-1, keepdims=True) + NORM_EPS)
    return (y * g) * (1.0 + sc) + sh


def _ctx_rows(i, tm, n_ctx):
    row = i * tm + lax.broadcasted_iota(jnp.int32, (tm, 1), 0)
    return row < n_ctx


def _pick_mod(mod_ref, is_ctx):
    return jnp.where(is_ctx, mod_ref[0, 0:1, :], mod_ref[0, 1:2, :])


def _ada_kernel(c_ref, w_ref, b_ref, o_ref):
    c = c_ref[...]
    s = c * _sigmoid(c)
    o_ref[0] = _dot(s, w_ref[0], HI) + b_ref[0]


def _ada(cond8, w_ada, b_ada):
    L, D, N = w_ada.shape
    tn = 1024
    return pl.pallas_call(
        _ada_kernel,
        grid=(L, N // tn),
        in_specs=[pl.BlockSpec((8, D), lambda l, j: (0, 0)),
                  pl.BlockSpec((1, D, tn), lambda l, j: (l, 0, j)),
                  pl.BlockSpec((1, 1, tn), lambda l, j: (l, 0, j))],
        out_specs=pl.BlockSpec((1, 8, tn), lambda l, j: (l, 0, j)),
        out_shape=jax.ShapeDtypeStruct((L, 8, N), F32),
        compiler_params=_cp(("arbitrary", "arbitrary")),
        name="ada",
    )(cond8, w_ada, b_ada.reshape(L, 1, N))


def _inproj_kernel(x_ref, sh_ref, sc_ref, g_ref, w_ref, o_ref, h_scr, *, tm, n_ctx):
    i = pl.program_id(0)

    @pl.when(pl.program_id(1) == 0)
    def _():
        is_ctx = _ctx_rows(i, tm, n_ctx)
        h = _norm_mod(x_ref[...], g_ref[...], _pick_mod(sh_ref, is_ctx), _pick_mod(sc_ref, is_ctx))
        h_scr[...] = h.astype(BF16)

    o_ref[...] = _dot(h_scr[...], w_ref[...])


def _inproj(x, mods, l, g, w, n_ctx, tm):
    T, D = x.shape
    N = w.shape[1]
    tn = 512
    return pl.pallas_call(
        functools.partial(_inproj_kernel, tm=tm, n_ctx=n_ctx),
        grid=(T // tm, N // tn),
        in_specs=[pl.BlockSpec((tm, D), lambda i, j: (i, 0)),
                  pl.BlockSpec((1, 8, D), lambda i, j: (l, 0, 0)),
                  pl.BlockSpec((1, 8, D), lambda i, j: (l, 0, 1)),
                  pl.BlockSpec((1, D), lambda i, j: (0, 0)),
                  pl.BlockSpec((D, tn), lambda i, j: (0, j))],
        out_specs=pl.BlockSpec((tm, tn), lambda i, j: (i, j)),
        out_shape=jax.ShapeDtypeStruct((T, N), F32),
        scratch_shapes=[pltpu.VMEM((tm, D), BF16)],
        compiler_params=_cp(("arbitrary", "arbitrary")),
        name="inproj",
    )(x, mods, mods, g, w)


def _softplus(x):
    return jnp.maximum(x, 0.0) + jnp.log(1.0 + jnp.exp(-jnp.abs(x)))


def _rwkv_prep_kernel(p_ref, pp_ref, pn_ref, mu_ref, w0_ref, w2_ref, a0_ref, a2_ref, g2_ref,
                      kkp_ref, kap_ref, rkp_ref, gh_ref,
                      r_ref, v_ref, kk_ref, lwf_ref, kf_ref, bf_ref, lwb_ref, kb_ref, bb_ref,
                      gate_ref, bonus_ref, *, tm, n_ctx, n_tot, W):
    i = pl.program_id(0)
    p = p_ref[...]
    lrow = lax.broadcasted_iota(jnp.int32, (tm, 1), 0)
    grow = i * tm + lrow
    prev = jnp.where(lrow == 0, pp_ref[7:8, :], pltpu.roll(p, 1, axis=0))
    prev = jnp.where((grow == 0) | (grow == n_ctx), 0.0, prev)
    nxt = jnp.where(lrow == tm - 1, pn_ref[0:1, :], pltpu.roll(p, tm - 1, axis=0))
    nxt = jnp.where((grow == n_ctx - 1) | (grow == n_tot - 1), 0.0, nxt)
    ps = p + mu_ref[0:1, :] * (prev - p) + mu_ref[1:2, :] * (nxt - p)

    r = ps[:, 0:W]
    k = ps[:, W:2 * W]
    v = ps[:, 2 * W:3 * W]
    o = 3 * W
    wd = (ps[:, o:o + DECAY_LORA], ps[:, o + DECAY_LORA:o + 2 * DECAY_LORA])
    o += 2 * DECAY_LORA
    ad = (ps[:, o:o + ICLR_LORA], ps[:, o + ICLR_LORA:o + 2 * ICLR_LORA])
    o += 2 * ICLR_LORA
    gd = ps[:, o:o + GATE_LORA]

    gh = gh_ref[...]
    kk = k * kkp_ref[...]
    kk = kk / jnp.maximum(jnp.sqrt(_seg_sum(kk * kk, gh)), 1e-12)
    r_ref[...] = r
    v_ref[...] = v
    kk_ref[...] = kk
    outs = ((lwf_ref, kf_ref, bf_ref), (lwb_ref, kb_ref, bb_ref))
    for d in range(2):
        z = w0_ref[d:d + 1, :] + _dot(jnp.tanh(wd[d]), w2_ref[d], HI)
        w_log = -_softplus(-z) - 0.5
        a = _sigmoid(a0_ref[d:d + 1, :] + _dot(ad[d], a2_ref[d], HI))
        lw_ref, kd_ref, bd_ref = outs[d]
        lw_ref[...] = -jnp.exp(w_log)
        kd_ref[...] = k * (1.0 + (a - 1.0) * kap_ref[...])
        bd_ref[...] = kk * a
    gate_ref[...] = _dot(_sigmoid(gd), g2_ref[...], HI)
    bonus_ref[...] = _seg_sum(r * k * rkp_ref[...], gh) * v


def _rwkv_prep(P, prm, gh, n_ctx, tm):
    T = P.shape[0]
    W = prm["kk"].shape[1]
    PW = 2048
    nb8 = T // 8
    row = pl.BlockSpec((tm, W), lambda i: (i, 0))
    full = lambda a: pl.BlockSpec(a.shape, lambda i: (0,) * a.ndim)
    args = (prm["mu"], prm["w0"], prm["w2"], prm["a0"], prm["a2"], prm["g2"], prm["kk"], prm["ka"], prm["rk"], gh)
    return pl.pallas_call(
        functools.partial(_rwkv_prep_kernel, tm=tm, n_ctx=n_ctx, n_tot=T, W=W),
        grid=(T // tm,),
        in_specs=[pl.BlockSpec((tm, PW), lambda i: (i, 0)),
                  pl.BlockSpec((8, PW), lambda i: (jnp.maximum(i * (tm // 8) - 1, 0), 0)),
                  pl.BlockSpec((8, PW), lambda i: (jnp.minimum((i + 1) * (tm // 8), nb8 - 1), 0))]
                 + [full(a) for a in args],
        out_specs=[row] * 11,
        out_shape=[jax.ShapeDtypeStruct((T, W), F32)] * 11,
        compiler_params=_cp(("arbitrary",)),
        name="rwkv_prep",
    )(P, P, P, *args)


def _rwkv_scan_kernel(rf, vf, kkf, lwf, kf, bf, rb, vb, kkb, lwb, kb, bb, yf_ref, yb_ref, s_scr, *, C, H, N):
    @pl.when(pl.program_id(0) == 0)
    def _():
        s_scr[...] = jnp.zeros_like(s_scr)

    row = lax.broadcasted_iota(jnp.int32, (C, C), 0)
    col = lax.broadcasted_iota(jnp.int32, (C, C), 1)
    eye = (row == col).astype(F32)
    n_sq = int(np.log2(C)) - 1
    dirs = ((rf, vf, kkf, lwf, kf, bf), (rb, vb, kkb, lwb, kb, bb))
    ch = []
    for d, (r_ref, v_ref, kk_ref, lw_ref, k_ref, b_ref) in enumerate(dirs):
        incl = (col <= row) if d == 0 else (col >= row)
        strict = (col < row) if d == 0 else (col > row)
        lw = lw_ref[...]
        cum = _dot(incl.astype(F32), lw, HI)
        e_pos = jnp.exp(cum)
        e_neg = jnp.exp(-cum)
        rt = (r_ref[...] * e_pos).astype(BF16)
        at = (-kk_ref[...] * jnp.exp(cum - lw)).astype(BF16)
        bt = (b_ref[...] * e_neg).astype(BF16)
        kt = (k_ref[...] * e_neg).astype(BF16)
        vv = v_ref[...].astype(BF16)
        gam = e_pos[C - 1:C, :] if d == 0 else e_pos[0:1, :]
        for h in range(H):
            sl = slice(h * N, (h + 1) * N)
            ch.append(dict(d=d, h=h, incl=incl, strict=strict, at=at[:, sl], rt=rt[:, sl], bt=bt[:, sl],
                           kt=kt[:, sl], v=vv[:, sl], gam=gam[:, sl], s0=s_scr[d, h]))
    for c in ch:
        a_all = _dot_nt(jnp.concatenate([c["at"], c["rt"]], axis=0), jnp.concatenate([c["bt"], c["kt"]], axis=0))
        c["l_pow"] = jnp.where(c["strict"], a_all[:C, :C], 0.0)
        c["l_ak"] = jnp.where(c["strict"], a_all[:C, C:], 0.0).astype(BF16)
        c["m_rb"] = jnp.where(c["incl"], a_all[C:, :C], 0.0).astype(BF16)
        c["m_rk"] = jnp.where(c["incl"], a_all[C:, C:], 0.0).astype(BF16)
        c["t_inv"] = eye + c["l_pow"]
    for _ in range(n_sq):
        for c in ch:
            lb = c["l_pow"].astype(BF16)
            c["l_pow"] = _dot(lb, lb)
        for c in ch:
            c["t_inv"] = c["t_inv"] + _dot(c["t_inv"].astype(BF16), c["l_pow"].astype(BF16))
    for c in ch:
        c["s0b"] = c["s0"].astype(BF16)
        c["w1"] = _dot_nt(c["at"], c["s0b"]) + _dot(c["l_ak"], c["v"])
    for c in ch:
        c["u"] = _dot(c["t_inv"].astype(BF16), c["w1"].astype(BF16)).astype(BF16)
    for c in ch:
        s_scr[c["d"], c["h"]] = (c["s0"] + _dot_tn(c["u"], c["bt"]) + _dot_tn(c["v"], c["kt"])) * c["gam"]
    for c in ch:
        c["y"] = _dot_nt(c["rt"], c["s0b"]) + _dot(c["m_rb"], c["u"]) + _dot(c["m_rk"], c["v"])
    yf_ref[...] = jnp.concatenate([c["y"] for c in ch[:H]], axis=1)
    yb_ref[...] = jnp.concatenate([c["y"] for c in ch[H:]], axis=1)


def _rwkv_scan(r, v, kk, lwf, kf, bf, lwb, kb, bb, n_ctx):
    T, W = r.shape
    C = SCAN_CHUNK
    H = W // RWKV_HEAD
    nch = T // C
    cch = n_ctx // C
    fwd = pl.BlockSpec((C, W), lambda n: (n, 0))
    bwd = pl.BlockSpec((C, W), lambda n: (jnp.where(n < cch, cch - 1 - n, nch - 1 + cch - n), 0))
    return pl.pallas_call(
        functools.partial(_rwkv_scan_kernel, C=C, H=H, N=RWKV_HEAD),
        grid=(nch,),
        in_specs=[fwd] * 6 + [bwd] * 6,
        out_specs=[fwd, bwd],
        out_shape=[jax.ShapeDtypeStruct((T, W), F32)] * 2,
        scratch_shapes=[pltpu.VMEM((2, H, RWKV_HEAD, RWKV_HEAD), F32)],
        compiler_params=_cp(("arbitrary",)),
        name="rwkv_scan",
    )(r, v, kk, lwf, kf, bf, r, v, kk, lwb, kb, bb)


def _rwkv_post_kernel(yf_ref, yb_ref, bonus_ref, gate_ref, lng_ref, lnb_ref, gh_ref, o_ref):
    gh = gh_ref[...]
    y = yf_ref[...] + yb_ref[...]
    mean = _seg_sum(y, gh) * (1.0 / RWKV_HEAD)
    yc = y - mean
    var = _seg_sum(yc * yc, gh) * (1.0 / RWKV_HEAD)
    yn = yc * lax.rsqrt(var + RWKV_GN_EPS) * lng_ref[...] + lnb_ref[...]
    o_ref[...] = ((yn + bonus_ref[...]) * gate_ref[...]).astype(o_ref.dtype)


def _rwkv_post(yf, yb, bonus, gate, lng, lnb, gh, tm):
    T, W = yf.shape
    row = pl.BlockSpec((tm, W), lambda i: (i, 0))
    full = lambda a: pl.BlockSpec(a.shape, lambda i: (0,) * a.ndim)
    return pl.pallas_call(
        _rwkv_post_kernel,
        grid=(T // tm,),
        in_specs=[row] * 4 + [full(lng), full(lnb), full(gh)],
        out_specs=row,
        out_shape=jax.ShapeDtypeStruct((T, W), BF16),
        compiler_params=_cp(("arbitrary",)),
        name="rwkv_post",
    )(yf, yb, bonus, gate, lng, lnb, gh)


def _gmlp_kernel(p_ref, g_ref, ws_ref, b_ref, o_ref, *, tm, W):
    u = jax.nn.gelu(p_ref[:, 0:W])
    v = jax.nn.gelu(p_ref[:, W:2 * W])
    v = v * lax.rsqrt(jnp.mean(v * v, axis=-1, keepdims=True) + NORM_EPS) * g_ref[...]
    vb = v.astype(BF16)
    gw = W // GM_GROUPS
    for c in range(tm // GM_CHUNK):
        rows = slice(c * GM_CHUNK, (c + 1) * GM_CHUNK)
        parts = [_dot(ws_ref[g], vb[rows, g * gw:(g + 1) * gw]) for g in range(GM_GROUPS)]
        s = jnp.concatenate(parts, axis=1) + b_ref[...]
        o_ref[rows, :] = (u[rows, :] * s).astype(o_ref.dtype)


def _gmlp(P, col_block, g, ws, bias, tm):
    T = P.shape[0]
    W = g.shape[1]
    return pl.pallas_call(
        functools.partial(_gmlp_kernel, tm=tm, W=W),
        grid=(T // tm,),
        in_specs=[pl.BlockSpec((tm, 2 * W), lambda i: (i, col_block)),
                  pl.BlockSpec((1, W), lambda i: (0, 0)),
                  pl.BlockSpec(ws.shape, lambda i: (0, 0, 0)),
                  pl.BlockSpec(bias.shape, lambda i: (0, 0))],
        out_specs=pl.BlockSpec((tm, W), lambda i: (i, 0)),
        out_shape=jax.ShapeDtypeStruct((T, W), BF16),
        compiler_params=_cp(("arbitrary",)),
        name="gmlp",
    )(P, g, ws, bias)


def _rope(x, cos, sin, lane):
    w = x.shape[1]
    partner = jnp.where((lane % 32) < 16, pltpu.roll(x, w - 16, axis=1), pltpu.roll(x, 16, axis=1))
    return x * cos + partner * sin


def _qk_prep_kernel(q_ref, k_ref, cos_ref, sin_ref, qg_ref, kg_ref, ghq_ref, ghk_ref, qo_ref, ko_ref, *, scale):
    cos = cos_ref[...]
    sin = sin_ref[...]
    for x_ref, g_ref, gh_ref, o_ref, mul in ((q_ref, qg_ref, ghq_ref, qo_ref, scale), (k_ref, kg_ref, ghk_ref, ko_ref, 1.0)):
        x = x_ref[...]
        w = x.shape[1]
        ss = _seg_sum(x * x, gh_ref[...]) * (1.0 / ATTN_HEAD)
        xn = x * lax.rsqrt(ss + NORM_EPS) * g_ref[...]
        rep = w // cos.shape[1]
        lane = lax.broadcasted_iota(jnp.int32, x.shape, 1)
        xr = _rope(xn, jnp.tile(cos, (1, rep)), jnp.tile(sin, (1, rep)), lane)
        o_ref[...] = (xr * mul).astype(o_ref.dtype)


def _qk_prep(P, q_block, k_block, cos, sin, qg, kg, ghq, ghk, tm):
    T = P.shape[0]
    QW, KW = qg.shape[1], kg.shape[1]
    full = lambda a: pl.BlockSpec(a.shape, lambda i: (0,) * a.ndim)
    return pl.pallas_call(
        functools.partial(_qk_prep_kernel, scale=ATTN_HEAD ** -0.5),
        grid=(T // tm,),
        in_specs=[pl.BlockSpec((tm, QW), lambda i: (i, q_block)),
                  pl.BlockSpec((tm, KW), lambda i: (i, k_block)),
                  pl.BlockSpec((tm, cos.shape[1]), lambda i: (i, 0)),
                  pl.BlockSpec((tm, sin.shape[1]), lambda i: (i, 0)),
                  full(qg), full(kg), full(ghq), full(ghk)],
        out_specs=[pl.BlockSpec((tm, QW), lambda i: (i, 0)), pl.BlockSpec((tm, KW), lambda i: (i, 0))],
        out_shape=[jax.ShapeDtypeStruct((T, QW), BF16), jax.ShapeDtypeStruct((T, KW), BF16)],
        compiler_params=_cp(("arbitrary",)),
        name="qk_prep",
    )(P, P, cos, sin, qg, kg, ghq, ghk)


def _attn_block(i, nb, sink_ref, q_ref, k_refs, v_refs, o_ref, local):
    L = ATTN_BLOCK
    G = ATTN_GROUP
    hd = ATTN_HEAD
    n_kv = k_refs[-1].shape[1] // hd
    R = G * L
    srow = lax.broadcasted_iota(jnp.int32, (R, 1), 0)
    if local:
        qi = lax.broadcasted_iota(jnp.int32, (R, 3 * L), 0) % L
        kj = lax.broadcasted_iota(jnp.int32, (R, 3 * L), 1)
        rel = kj - L - qi
        valid = (rel <= ATTN_WINDOW) & (rel >= -ATTN_WINDOW)
        valid = valid & ((kj >= L) | (i > 0)) & ((kj < 2 * L) | (i < nb - 1))
    for j in range(n_kv):
        ks = [r[:, j * hd:(j + 1) * hd] for r in k_refs]
        vs = [r[:, j * hd:(j + 1) * hd].astype(BF16) for r in v_refs]
        q = jnp.concatenate([q_ref[:, (j * G + g) * hd:(j * G + g + 1) * hd] for g in range(G)], axis=0)
        sink = jnp.zeros((R, 1), F32)
        for g in range(G):
            sink = jnp.where((srow >= g * L) & (srow < (g + 1) * L), sink_ref[j * G + g], sink)
        s_ctx = _dot_nt(q, ks[-1])
        m = jnp.maximum(jnp.max(s_ctx, axis=1, keepdims=True), sink)
        if local:
            s_loc = _dot_nt(q, jnp.concatenate(ks[:3], axis=0))
            s_loc = jnp.where(valid, s_loc, -1e30)
            m = jnp.maximum(m, jnp.max(s_loc, axis=1, keepdims=True))
            p_loc = jnp.exp(s_loc - m)
        p_ctx = jnp.exp(s_ctx - m)
        den = jnp.sum(p_ctx, axis=1, keepdims=True) + jnp.exp(sink - m)
        acc = _dot(p_ctx.astype(BF16), vs[-1])
        if local:
            den = den + jnp.sum(p_loc, axis=1, keepdims=True)
            acc = acc + _dot(p_loc.astype(BF16), jnp.concatenate(vs[:3], axis=0))
        out = acc / den
        for g in range(G):
            h = j * G + g
            o_ref[:, h * hd:(h + 1) * hd] = out[g * L:(g + 1) * L, :].astype(o_ref.dtype)


def _attn_kernel(sink_ref, q_ref, kp_ref, kc_ref, kn_ref, kx_ref, vp_ref, vc_ref, vn_ref, vx_ref, o_ref, *, cb, nb):
    i = pl.program_id(0)

    @pl.when(i < cb)
    def _():
        _attn_block(i, nb, sink_ref, q_ref, (kx_ref,), (vx_ref,), o_ref, False)

    @pl.when(i >= cb)
    def _():
        _attn_block(i - cb, nb, sink_ref, q_ref, (kp_ref, kc_ref, kn_ref, kx_ref),
                    (vp_ref, vc_ref, vn_ref, vx_ref), o_ref, True)


def _attention(qr, kr, P, v_block, sink, n_ctx):
    T, QW = qr.shape
    KW = kr.shape[1]
    L = ATTN_BLOCK
    cb = n_ctx // L
    nb = (T - n_ctx) // L
    lo, hi = cb, cb + nb - 1
    shifts = (lambda i: jnp.clip(i - 1, lo, hi), lambda i: jnp.clip(i, lo, hi), lambda i: jnp.clip(i + 1, lo, hi))
    kspec = lambda f: pl.BlockSpec((L, KW), lambda i: (f(i), 0))
    vspec = lambda f: pl.BlockSpec((L, KW), lambda i: (f(i), v_block))
    in_specs = ([pl.BlockSpec(memory_space=pltpu.SMEM), pl.BlockSpec((L, QW), lambda i: (i, 0))]
                + [kspec(f) for f in shifts] + [pl.BlockSpec((n_ctx, KW), lambda i: (0, 0))]
                + [vspec(f) for f in shifts] + [pl.BlockSpec((n_ctx, KW), lambda i: (0, v_block))])
    return pl.pallas_call(
        functools.partial(_attn_kernel, cb=cb, nb=nb),
        grid=(cb + nb,),
        in_specs=in_specs,
        out_specs=pl.BlockSpec((L, QW), lambda i: (i, 0)),
        out_shape=jax.ShapeDtypeStruct((T, QW), BF16),
        compiler_params=_cp(("arbitrary",)),
        name="attn",
    )(sink, qr, kr, kr, kr, kr, P, P, P, P)


def _oproj_kernel(x_ref, gt_ref, a_ref, b_ref, c_ref, wa_ref, wb_ref, wc_ref, o_ref, *, tm, n_ctx):
    acc = _dot(a_ref[...], wa_ref[...]) + _dot(b_ref[...], wb_ref[...]) + _dot(c_ref[...], wc_ref[...])
    gate = _pick_mod(gt_ref, _ctx_rows(pl.program_id(0), tm, n_ctx))
    o_ref[...] = x_ref[...] + gate * acc


def _oproj(x, mods, l, y_rw, y_gm, y_at, w_o, n_ctx, tm):
    T, D = x.shape
    tn = 512
    W1 = y_rw.shape[1]
    W3 = y_at.shape[1]
    gate_col = 2 * (D // tn)
    return pl.pallas_call(
        functools.partial(_oproj_kernel, tm=tm, n_ctx=n_ctx),
        grid=(T // tm, D // tn),
        in_specs=[pl.BlockSpec((tm, tn), lambda i, j: (i, j)),
                  pl.BlockSpec((1, 8, tn), lambda i, j: (l, 0, gate_col + j)),
                  pl.BlockSpec((tm, W1), lambda i, j: (i, 0)),
                  pl.BlockSpec((tm, W1), lambda i, j: (i, 0)),
                  pl.BlockSpec((tm, W3), lambda i, j: (i, 0)),
                  pl.BlockSpec((W1, tn), lambda i, j: (0, j)),
                  pl.BlockSpec((W1, tn), lambda i, j: (1, j)),
                  pl.BlockSpec((W3, tn), lambda i, j: (1, j))],
        out_specs=pl.BlockSpec((tm, tn), lambda i, j: (i, j)),
        out_shape=jax.ShapeDtypeStruct((T, D), F32),
        compiler_params=_cp(("arbitrary", "arbitrary")),
        name="oproj",
    )(x, mods, y_rw, y_gm, y_at, w_o, w_o, w_o)


def _router_kernel(x_ref, sh_ref, sc_ref, g_ref, rt_ref, bias_ref, tri_ref,
                   h_ref, idx_ref, wt_ref, rank_ref, cnt_ref, carry, *, tm, n_ctx):
    i = pl.program_id(0)

    @pl.when(i == 0)
    def _():
        carry[...] = jnp.zeros_like(carry)

    is_ctx = _ctx_rows(i, tm, n_ctx)
    h = _norm_mod(x_ref[...], g_ref[...], _pick_mod(sh_ref, is_ctx), _pick_mod(sc_ref, is_ctx))
    h_ref[...] = h
    E = N_EXPERTS
    pg = E // N_EXPERT_GROUPS
    neg = -jnp.inf
    scores = _sigmoid(_dot_nt(rt_ref[...], h, HI))
    biased = scores + bias_ref[:, 0:1]
    b3 = biased.reshape(N_EXPERT_GROUPS, pg, tm)
    i3 = lax.broadcasted_iota(jnp.int32, b3.shape, 1)
    m1 = jnp.max(b3, axis=1, keepdims=True)
    first = jnp.min(jnp.where(b3 == m1, i3, pg), axis=1, keepdims=True)
    m2 = jnp.max(jnp.where(i3 == first, neg, b3), axis=1, keepdims=True)
    gs = (m1 + m2).reshape(N_EXPERT_GROUPS, tm)
    gi = lax.broadcasted_iota(jnp.int32, gs.shape, 0)
    gsel = jnp.zeros(gs.shape, jnp.bool_)
    for _ in range(TOPK_GROUPS):
        gm = jnp.max(gs, axis=0, keepdims=True)
        gfirst = jnp.min(jnp.where(gs == gm, gi, N_EXPERT_GROUPS), axis=0, keepdims=True)
        hit = gi == gfirst
        gsel = gsel | hit
        gs = jnp.where(hit, neg, gs)
    masked = jnp.where(gsel.reshape(N_EXPERT_GROUPS, 1, tm), b3, neg).reshape(E, tm)
    ei = lax.broadcasted_iota(jnp.int32, (E, tm), 0)
    sel = jnp.zeros((E, tm), jnp.bool_)
    picks = []
    for _ in range(TOP_K):
        mx = jnp.max(masked, axis=0, keepdims=True)
        efirst = jnp.min(jnp.where(masked == mx, ei, E), axis=0, keepdims=True)
        hit = ei == efirst
        sel = sel | hit
        masked = jnp.where(hit, neg, masked)
        picks.append((efirst, hit, jnp.sum(jnp.where(hit, scores, 0.0), axis=0, keepdims=True)))
    wsum = picks[0][2]
    for pk in picks[1:]:
        wsum = wsum + pk[2]
    self_f = jnp.where(sel, 1.0, 0.0)
    rank_dense = carry[:, 0:1] + _dot(self_f.astype(BF16), tri_ref[...])
    carry[...] = carry[...] + jnp.sum(self_f, axis=1, keepdims=True)
    cnt_ref[...] = carry[...]
    for kx, (efirst, hit, wk) in enumerate(picks):
        idx_ref[kx:kx + 1, :] = efirst
        wt_ref[kx:kx + 1, :] = wk / wsum * ROUTED_SCALE
        rank_ref[kx:kx + 1, :] = jnp.sum(jnp.where(hit, rank_dense, 0.0), axis=0, keepdims=True).astype(jnp.int32)


def _router(x, mods, l, g, router_t, bias, tri, n_ctx, tm):
    T, D = x.shape
    E = router_t.shape[0]
    kspec = pl.BlockSpec((TOP_K, tm), lambda i: (0, i))
    return pl.pallas_call(
        functools.partial(_router_kernel, tm=tm, n_ctx=n_ctx),
        grid=(T // tm,),
        in_specs=[pl.BlockSpec((tm, D), lambda i: (i, 0)),
                  pl.BlockSpec((1, 8, D), lambda i: (l, 0, 3)),
                  pl.BlockSpec((1, 8, D), lambda i: (l, 0, 4)),
                  pl.BlockSpec((1, D), lambda i: (0, 0)),
                  pl.BlockSpec((E, D), lambda i: (0, 0)),
                  pl.BlockSpec((E, 128), lambda i: (0, 0)),
                  pl.BlockSpec((tm, tm), lambda i: (0, 0))],
        out_specs=[pl.BlockSpec((tm, D), lambda i: (i, 0)), kspec, kspec, kspec,
                   pl.BlockSpec((E, 128), lambda i: (0, 0))],
        out_shape=[jax.ShapeDtypeStruct((T, D), F32),
                   jax.ShapeDtypeStruct((TOP_K, T), jnp.int32),
                   jax.ShapeDtypeStruct((TOP_K, T), F32),
                   jax.ShapeDtypeStruct((TOP_K, T), jnp.int32),
                   jax.ShapeDtypeStruct((E, 128), F32)],
        scratch_shapes=[pltpu.VMEM((E, 128), F32)],
        compiler_params=_cp(("arbitrary",)),
        name="router",
    )(x, mods, mods, g, router_t, bias, tri)


def _gather_rows(idx_ref, n_rows, src_hbm, dst, sem):
    def body(r, carry):
        pltpu.make_async_copy(src_hbm.at[pl.ds(idx_ref[r], 1), :], dst.at[pl.ds(r, 1), :], sem).start()
        return carry
    lax.fori_loop(0, n_rows, body, 0)


def _experts_kernel(be_ref, nu_ref, tokc_ref, tokn_ref, h_hbm, wg_ref, wu_ref, wd_ref, y_ref,
                    hbuf, sem, wg_s, wu_s, wd_s, *, TB, nblk):
    b = pl.program_id(0)
    slot = b % 2
    n_used = nu_ref[0]

    @pl.when(b == 0)
    def _():
        _gather_rows(tokc_ref.at[0, 0], TB, h_hbm, hbuf.at[0], sem.at[0])

    @pl.when(b + 1 < n_used)
    def _():
        _gather_rows(tokn_ref.at[0, 0], TB, h_hbm, hbuf.at[1 - slot], sem.at[1 - slot])

    new_expert = (b == 0) | (be_ref[b] != be_ref[jnp.maximum(b - 1, 0)])

    @pl.when(new_expert & (b < n_used))
    def _():
        wg_s[...] = wg_ref[0, 0].astype(BF16)
        wu_s[...] = wu_ref[0, 0].astype(BF16)
        wd_s[...] = wd_ref[0, 0].astype(BF16)

    @pl.when(b < n_used)
    def _():
        pltpu.make_async_copy(h_hbm.at[pl.ds(0, TB), :], hbuf.at[slot], sem.at[slot]).wait()
        hb = hbuf[slot].astype(BF16)
        a = _dot(hb, wg_s[...])
        u = _dot(hb, wu_s[...])
        act = (a * _sigmoid(a) * u).astype(BF16)
        y_ref[...] = _dot(act, wd_s[...])

    @pl.when(b >= n_used)
    def _():
        y_ref[...] = jnp.zeros_like(y_ref)


def _experts(h2, tok, block_e, n_used, wg, wu, wd, l):
    T, D = h2.shape
    nblk = block_e.shape[0]
    TB = EXPERT_TILE
    FF = wg.shape[3]
    tok3 = tok.reshape(nblk, 1, TB)
    grid_spec = pltpu.PrefetchScalarGridSpec(
        num_scalar_prefetch=2,
        grid=(nblk,),
        in_specs=[pl.BlockSpec((1, 1, TB), lambda b, be, nu: (b, 0, 0), memory_space=pltpu.SMEM),
                  pl.BlockSpec((1, 1, TB), lambda b, be, nu: (jnp.minimum(b + 1, nblk - 1), 0, 0),
                               memory_space=pltpu.SMEM),
                  pl.BlockSpec(memory_space=pl.ANY),
                  pl.BlockSpec((1, 1, D, FF), lambda b, be, nu: (l, be[b], 0, 0)),
                  pl.BlockSpec((1, 1, D, FF), lambda b, be, nu: (l, be[b], 0, 0)),
                  pl.BlockSpec((1, 1, FF, D), lambda b, be, nu: (l, be[b], 0, 0))],
        out_specs=pl.BlockSpec((TB, D), lambda b, be, nu: (b, 0)),
        scratch_shapes=[pltpu.VMEM((2, TB, D), F32), pltpu.SemaphoreType.DMA((2,)),
                        pltpu.VMEM((D, FF), BF16), pltpu.VMEM((D, FF), BF16), pltpu.VMEM((FF, D), BF16)],
    )
    return pl.pallas_call(
        functools.partial(_experts_kernel, TB=TB, nblk=nblk),
        grid_spec=grid_spec,
        out_shape=jax.ShapeDtypeStruct((nblk * TB, D), F32),
        compiler_params=_cp(("arbitrary",)),
        name="experts",
    )(block_e, n_used, tok3, tok3, h2, wg, wu, wd)


def _combine_kernel(slc_ref, sln_ref, y_hbm, x_ref, h_ref, wt_ref, gt_ref, swg_ref, swu_ref, swd_ref, o_ref,
                    ybuf, sem, *, tm, n_ctx, nt):
    i = pl.program_id(0)
    slot = i % 2

    @pl.when(i == 0)
    def _():
        for k in range(TOP_K):
            _gather_rows(slc_ref.at[0, k], tm, y_hbm, ybuf.at[0, k], sem.at[0])

    @pl.when(i + 1 < nt)
    def _():
        for k in range(TOP_K):
            _gather_rows(sln_ref.at[0, k], tm, y_hbm, ybuf.at[1 - slot, k], sem.at[1 - slot])

    hb = h_ref[...].astype(BF16)
    a = _dot(hb, swg_ref[...])
    u = _dot(hb, swu_ref[...])
    acc = _dot((a * _sigmoid(a) * u).astype(BF16), swd_ref[...])
    for k in range(TOP_K):
        pltpu.make_async_copy(y_hbm.at[pl.ds(0, tm), :], ybuf.at[slot, k], sem.at[slot]).wait()
    for k in range(TOP_K):
        acc = acc + wt_ref[:, k:k + 1] * ybuf[slot, k]
    gate = _pick_mod(gt_ref, _ctx_rows(i, tm, n_ctx))
    o_ref[...] = x_ref[...] + gate * acc


def _combine(y, slot8, wt_t, x, h2, mods, l, swg, swu, swd, n_ctx, tm):
    T, D = x.shape
    nt = T // tm
    FF = swg.shape[1]
    sl3 = slot8.reshape(TOP_K, nt, tm).transpose(1, 0, 2)
    return pl.pallas_call(
        functools.partial(_combine_kernel, tm=tm, n_ctx=n_ctx, nt=nt),
        grid=(nt,),
        in_specs=[pl.BlockSpec((1, TOP_K, tm), lambda i: (i, 0, 0), memory_space=pltpu.SMEM),
                  pl.BlockSpec((1, TOP_K, tm), lambda i: (jnp.minimum(i + 1, nt - 1), 0, 0),
                               memory_space=pltpu.SMEM),
                  pl.BlockSpec(memory_space=pl.ANY),
                  pl.BlockSpec((tm, D), lambda i: (i, 0)),
                  pl.BlockSpec((tm, D), lambda i: (i, 0)),
                  pl.BlockSpec((tm, TOP_K), lambda i: (i, 0)),
                  pl.BlockSpec((1, 8, D), lambda i: (l, 0, 5)),
                  pl.BlockSpec((D, FF), lambda i: (0, 0)),
                  pl.BlockSpec((D, FF), lambda i: (0, 0)),
                  pl.BlockSpec((FF, D), lambda i: (0, 0))],
        out_specs=pl.BlockSpec((tm, D), lambda i: (i, 0)),
        out_shape=jax.ShapeDtypeStruct((T, D), F32),
        scratch_shapes=[pltpu.VMEM((2, TOP_K, tm, D), F32), pltpu.SemaphoreType.DMA((2,))],
        compiler_params=_cp(("arbitrary",)),
        name="combine",
    )(sl3, sl3, y, x, h2, wt_t, mods, swg, swu, swd)


def _seg_indicator(width, seg):
    idx = np.arange(width) // seg
    return jnp.asarray(idx[:, None] == idx[None, :], dtype=BF16)


def _rope_tables(n_ctx, n_lat):
    rows = n_lat // GRID_W
    row = jnp.repeat(jnp.arange(rows, dtype=F32), GRID_W)
    col = jnp.tile(jnp.arange(GRID_W, dtype=F32), rows)
    n_freq = ATTN_HEAD // 4
    inv_freq = ROPE_THETA ** (-jnp.arange(n_freq, dtype=F32) / n_freq)
    ar, ac = row[:, None] * inv_freq, col[:, None] * inv_freq
    cos = jnp.concatenate([jnp.cos(ar), jnp.cos(ar), jnp.cos(ac), jnp.cos(ac)], axis=1)
    sin = jnp.concatenate([-jnp.sin(ar), jnp.sin(ar), -jnp.sin(ac), jnp.sin(ac)], axis=1)
    cos = jnp.concatenate([jnp.ones((n_ctx, ATTN_HEAD), F32), cos], axis=0)
    sin = jnp.concatenate([jnp.zeros((n_ctx, ATTN_HEAD), F32), sin], axis=0)
    return jnp.tile(cos, (1, 2)), jnp.tile(sin, (1, 2))


def kernel(x, c, ctx, c_ctx, w_ada, b_ada, norm1_g, norm2_g, w_in, w_o, rw_mu, rw_w0, rw_w2, rw_a0, rw_a2, rw_g2, rw_kk, rw_ka, rw_rk, rw_ln_g, rw_ln_b, gm_norm_g, gm_ws, gm_b, at_qn, at_kn, at_sink, moe_router, moe_bias, moe_wg, moe_wu, moe_wd, sh_wg, sh_wu, sh_wd):
    B, S, D = x.shape
    C = ctx.shape[1]
    assert B == 1
    L = w_ada.shape[0]
    T = C + S
    RW = rw_kk.shape[1]
    GW = gm_norm_g.shape[1]
    QW = at_sink.shape[1] * ATTN_HEAD
    KW = QW // ATTN_GROUP
    rw_proj = rw_mu.shape[2]
    RWP = 2048
    assert rw_proj <= RWP and T % 768 == 0 and C % 256 == 0 and S % 256 == 0

    xs = jnp.concatenate([ctx[0], x[0]], axis=0)
    cond8 = jnp.zeros((8, D), F32).at[0].set(c_ctx).at[1].set(c[0])
    mods = _ada(cond8, w_ada, b_ada)

    gh_rw = _seg_indicator(RW, RWKV_HEAD)
    gh_q = _seg_indicator(QW, ATTN_HEAD)
    gh_k = _seg_indicator(KW, ATTN_HEAD)
    cos, sin = _rope_tables(C, S)
    tri = jnp.asarray(np.arange(256)[:, None] < np.arange(256)[None, :], dtype=BF16)
    TB = EXPERT_TILE
    nblk = -(-(T * TOP_K) // TB) + N_EXPERTS

    for l in range(L):
        w_in_l = w_in[l]
        w_in_p = jnp.concatenate([w_in_l[:, :rw_proj], jnp.zeros((D, RWP - rw_proj), F32), w_in_l[:, rw_proj:]],
                                 axis=1).astype(BF16)
        P = _inproj(xs, mods, l, norm1_g[l][None], w_in_p, C, 768)
        gm_block = RWP // (2 * GW)
        q_block = (RWP + 2 * GW) // QW
        k_block = (RWP + 2 * GW + QW) // KW
        v_block = k_block + 1

        pad = lambda a: jnp.pad(a, ((0, 0), (0, RWP - rw_proj)))
        prm = {"mu": pad(rw_mu[l]), "w0": rw_w0[l], "w2": rw_w2[l], "a0": rw_a0[l], "a2": rw_a2[l],
               "g2": rw_g2[l], "kk": rw_kk[l][None], "ka": rw_ka[l][None], "rk": rw_rk[l].reshape(1, RW)}
        r, v, kk, lwf, kf, bf, lwb, kb, bb, gate, bonus = _rwkv_prep(P, prm, gh_rw, C, 256)
        yf, yb = _rwkv_scan(r, v, kk, lwf, kf, bf, lwb, kb, bb, C)
        y_rw = _rwkv_post(yf, yb, bonus, gate, rw_ln_g[l][None], rw_ln_b[l][None], gh_rw, 256)

        gm_bias = jnp.repeat(gm_b[l].T, GW // GM_GROUPS, axis=1)
        y_gm = _gmlp(P, gm_block, gm_norm_g[l][None], gm_ws[l].astype(BF16), gm_bias, 256)

        qg = jnp.tile(at_qn[l], QW // ATTN_HEAD)[None]
        kg = jnp.tile(at_kn[l], KW // ATTN_HEAD)[None]
        qr, kr = _qk_prep(P, q_block, k_block, cos, sin, qg, kg, gh_q, gh_k, 256)
        y_at = _attention(qr, kr, P, v_block, at_sink[l], C)

        xs = _oproj(xs, mods, l, y_rw, y_gm, y_at, w_o[l].astype(BF16), C, 768)

        bias_col = jnp.broadcast_to(moe_bias[l][:, None], (N_EXPERTS, 128))
        h2, idx8, wt8, rank8, cnt = _router(xs, mods, l, norm2_g[l][None], moe_router[l].T, bias_col, tri, C, 256)
        counts = cnt[:, 0].astype(jnp.int32)
        padded = (counts + TB - 1) // TB * TB
        pad_end = jnp.cumsum(padded)
        pad_start = pad_end - padded
        e_ids = jnp.arange(N_EXPERTS, dtype=jnp.int32)
        slot8 = jnp.sum(jnp.where(idx8[:, :, None] == e_ids, pad_start, 0), axis=-1) + rank8
        tok = jnp.zeros((nblk * TB,), jnp.int32).at[slot8.reshape(-1)].set(
            jnp.tile(jnp.arange(T, dtype=jnp.int32), TOP_K), unique_indices=True)
        blk_pos = jnp.arange(nblk, dtype=jnp.int32) * TB
        block_e = jnp.minimum(jnp.sum((pad_end[None, :] <= blk_pos[:, None]).astype(jnp.int32), axis=1), N_EXPERTS - 1)
        n_used = (pad_end[-1] // TB).astype(jnp.int32).reshape(1)
        y = _experts(h2, tok, block_e, n_used, moe_wg, moe_wu, moe_wd, l)
        xs = _combine(y, slot8, wt8.T, xs, h2, mods, l, sh_wg[l].astype(BF16), sh_wu[l].astype(BF16),
                      sh_wd[l].astype(BF16), C, 128)
    return xs[C:].reshape(B, S, D)
```

```python
import functools

import jax
import jax.numpy as jnp
import numpy as np
from jax import lax
from jax.experimental import pallas as pl
from jax.experimental.pallas import tpu as pltpu

F32 = jnp.float32
BF16 = jnp.bfloat16
HI = lax.Precision.HIGHEST

NORM_EPS = 1e-6
GRID_W = 64

RWKV_HEAD = 64
DECAY_LORA = 64
ICLR_LORA = 64
GATE_LORA = 128
RWKV_GN_EPS = 64e-5

GM_GROUPS = 8
GM_CHUNK = 128

ATTN_HEAD = 64
ATTN_GROUP = 4
ATTN_WINDOW = 128
ATTN_BLOCK = 128
ROPE_THETA = 10000.0

N_EXPERTS = 64
TOP_K = 8
N_EXPERT_GROUPS = 8
TOPK_GROUPS = 4
ROUTED_SCALE = 2.5

SCAN_CHUNK = 64
EXPERT_TILE = 256
SLAB = 16
VMEM_LIMIT = 56 * 1024 * 1024


def _cp(sem, vmem=VMEM_LIMIT):
    return pltpu.CompilerParams(dimension_semantics=sem, vmem_limit_bytes=vmem)


def _dot(a, b, prec=None):
    return jnp.dot(a, b, preferred_element_type=F32, precision=prec)


def _dot_nt(a, b, prec=None):
    return lax.dot_general(a, b, (((1,), (1,)), ((), ())), preferred_element_type=F32, precision=prec)


def _dot_tn(a, b, prec=None):
    return lax.dot_general(a, b, (((0,), (0,)), ((), ())), preferred_element_type=F32, precision=prec)


def _seg_sum(x, g):
    xh = x.astype(BF16)
    xl = (x - xh.astype(F32)).astype(BF16)
    return _dot(xh, g) + _dot(xl, g)


def _sigmoid(x):
    return jax.nn.sigmoid(x)


def _norm_mod(x, g, sh, sc):
    y = x * lax.rsqrt(jnp.mean(x * x, axis=-1, keepdims=True) + NORM_EPS)
    return (y * g) * (1.0 + sc) + sh


def _ctx_rows(i, tm, n_ctx):
    row = i * tm + lax.broadcasted_iota(jnp.int32, (tm, 1), 0)
    return row < n_ctx


def _pick_mod(mod_ref, is_ctx):
    return jnp.where(is_ctx, mod_ref[0, 0:1, :], mod_ref[0, 1:2, :])


def _ada_kernel(c_ref, w_ref, b_ref, o_ref):
    c = c_ref[...]
    s = c * _sigmoid(c)
    o_ref[0] = _dot(s, w_ref[0], HI) + b_ref[0]


def _ada(cond8, w_ada, b_ada):
    L, D, N = w_ada.shape
    tn = 1024
    return pl.pallas_call(
        _ada_kernel,
        grid=(L, N // tn),
        in_specs=[pl.BlockSpec((8, D), lambda l, j: (0, 0)),
                  pl.BlockSpec((1, D, tn), lambda l, j: (l, 0, j)),
                  pl.BlockSpec((1, 1, tn), lambda l, j: (l, 0, j))],
        out_specs=pl.BlockSpec((1, 8, tn), lambda l, j: (l, 0, j)),
        out_shape=jax.ShapeDtypeStruct((L, 8, N), F32),
        compiler_params=_cp(("arbitrary", "arbitrary")),
        name="ada",
    )(cond8, w_ada, b_ada.reshape(L, 1, N))


def _inproj_kernel(x_ref, sh_ref, sc_ref, g_ref, w_ref, o_ref, h_scr, *, tm, n_ctx):
    i = pl.program_id(0)

    @pl.when(pl.program_id(1) == 0)
    def _():
        is_ctx = _ctx_rows(i, tm, n_ctx)
        h = _norm_mod(x_ref[...], g_ref[...], _pick_mod(sh_ref, is_ctx), _pick_mod(sc_ref, is_ctx))
        h_scr[...] = h.astype(BF16)

    o_ref[...] = _dot(h_scr[...], w_ref[...])


def _inproj(x, mods, l, g, w, n_ctx, tm):
    T, D = x.shape
    N = w.shape[1]
    tn = 512
    return pl.pallas_call(
        functools.partial(_inproj_kernel, tm=tm, n_ctx=n_ctx),
        grid=(T // tm, N // tn),
        in_specs=[pl.BlockSpec((tm, D), lambda i, j: (i, 0)),
                  pl.BlockSpec((1, 8, D), lambda i, j: (l, 0, 0)),
                  pl.BlockSpec((1, 8, D), lambda i, j: (l, 0, 1)),
                  pl.BlockSpec((1, D), lambda i, j: (0, 0)),
                  pl.BlockSpec((D, tn), lambda i, j: (0, j))],
        out_specs=pl.BlockSpec((tm, tn), lambda i, j: (i, j)),
        out_shape=jax.ShapeDtypeStruct((T, N), F32),
        scratch_shapes=[pltpu.VMEM((tm, D), BF16)],
        compiler_params=_cp(("arbitrary", "arbitrary")),
        name="inproj",
    )(x, mods, mods, g, w)


def _softplus(x):
    return jnp.maximum(x, 0.0) + jnp.log(1.0 + jnp.exp(-jnp.abs(x)))


def _rwkv_prep_kernel(p_ref, pp_ref, pn_ref, mu_ref, w0_ref, w2_ref, a0_ref, a2_ref, g2_ref,
                      kkp_ref, kap_ref, rkp_ref, gh_ref,
                      r_ref, v_ref, kk_ref, lwf_ref, kf_ref, bf_ref, lwb_ref, kb_ref, bb_ref,
                      gate_ref, bonus_ref, *, tm, n_ctx, n_tot, W):
    i = pl.program_id(0)
    p = p_ref[...]
    lrow = lax.broadcasted_iota(jnp.int32, (tm, 1), 0)
    grow = i * tm + lrow
    prev = jnp.where(lrow == 0, pp_ref[7:8, :], pltpu.roll(p, 1, axis=0))
    prev = jnp.where((grow == 0) | (grow == n_ctx), 0.0, prev)
    nxt = jnp.where(lrow == tm - 1, pn_ref[0:1, :], pltpu.roll(p, tm - 1, axis=0))
    nxt = jnp.where((grow == n_ctx - 1) | (grow == n_tot - 1), 0.0, nxt)
    ps = p + mu_ref[0:1, :] * (prev - p) + mu_ref[1:2, :] * (nxt - p)

    r = ps[:, 0:W]
    k = ps[:, W:2 * W]
    v = ps[:, 2 * W:3 * W]
    o = 3 * W
    wd = (ps[:, o:o + DECAY_LORA], ps[:, o + DECAY_LORA:o + 2 * DECAY_LORA])
    o += 2 * DECAY_LORA
    ad = (ps[:, o:o + ICLR_LORA], ps[:, o + ICLR_LORA:o + 2 * ICLR_LORA])
    o += 2 * ICLR_LORA
    gd = ps[:, o:o + GATE_LORA]

    gh = gh_ref[...]
    kk = k * kkp_ref[...]
    kk = kk / jnp.maximum(jnp.sqrt(_seg_sum(kk * kk, gh)), 1e-12)
    r_ref[...] = r
    v_ref[...] = v
    kk_ref[...] = kk
    outs = ((lwf_ref, kf_ref, bf_ref), (lwb_ref, kb_ref, bb_ref))
    for d in range(2):
        z = w0_ref[d:d + 1, :] + _dot(jnp.tanh(wd[d]), w2_ref[d], HI)
        w_log = -_softplus(-z) - 0.5
        a = _sigmoid(a0_ref[d:d + 1, :] + _dot(ad[d], a2_ref[d], HI))
        lw_ref, kd_ref, bd_ref = outs[d]
        lw_ref[...] = -jnp.exp(w_log)
        kd_ref[...] = k * (1.0 + (a - 1.0) * kap_ref[...])
        bd_ref[...] = kk * a
    gate_ref[...] = _dot(_sigmoid(gd), g2_ref[...], HI)
    bonus_ref[...] = _seg_sum(r * k * rkp_ref[...], gh) * v


def _rwkv_prep(P, prm, gh, n_ctx, tm):
    T = P.shape[0]
    W = prm["kk"].shape[1]
    PW = 2048
    nb8 = T // 8
    row = pl.BlockSpec((tm, W), lambda i: (i, 0))
    full = lambda a: pl.BlockSpec(a.shape, lambda i: (0,) * a.ndim)
    args = (prm["mu"], prm["w0"], prm["w2"], prm["a0"], prm["a2"], prm["g2"], prm["kk"], prm["ka"], prm["rk"], gh)
    return pl.pallas_call(
        functools.partial(_rwkv_prep_kernel, tm=tm, n_ctx=n_ctx, n_tot=T, W=W),
        grid=(T // tm,),
        in_specs=[pl.BlockSpec((tm, PW), lambda i: (i, 0)),
                  pl.BlockSpec((8, PW), lambda i: (jnp.maximum(i * (tm // 8) - 1, 0), 0)),
                  pl.BlockSpec((8, PW), lambda i: (jnp.minimum((i + 1) * (tm // 8), nb8 - 1), 0))]
                 + [full(a) for a in args],
        out_specs=[row] * 11,
        out_shape=[jax.ShapeDtypeStruct((T, W), F32)] * 11,
        compiler_params=_cp(("arbitrary",)),
        name="rwkv_prep",
    )(P, P, P, *args)


def _rwkv_scan_kernel(rf, vf, kkf, lwf, kf, bf, rb, vb, kkb, lwb, kb, bb, yf_ref, yb_ref, s_scr, *, C, H, N):
    @pl.when(pl.program_id(0) == 0)
    def _():
        s_scr[...] = jnp.zeros_like(s_scr)

    row = lax.broadcasted_iota(jnp.int32, (C, C), 0)
    col = lax.broadcasted_iota(jnp.int32, (C, C), 1)
    eye = (row == col).astype(F32)
    n_sq = int(np.log2(C)) - 1
    dirs = ((rf, vf, kkf, lwf, kf, bf), (rb, vb, kkb, lwb, kb, bb))
    ch = []
    for d, (r_ref, v_ref, kk_ref, lw_ref, k_ref, b_ref) in enumerate(dirs):
        incl = (col <= row) if d == 0 else (col >= row)
        strict = (col < row) if d == 0 else (col > row)
        lw = lw_ref[...]
        cum = _dot(incl.astype(F32), lw, HI)
        e_pos = jnp.exp(cum)
        e_neg = jnp.exp(-cum)
        rt = (r_ref[...] * e_pos).astype(BF16)
        at = (-kk_ref[...] * jnp.exp(cum - lw)).astype(BF16)
        bt = (b_ref[...] * e_neg).astype(BF16)
        kt = (k_ref[...] * e_neg).astype(BF16)
        vv = v_ref[...].astype(BF16)
        gam = e_pos[C - 1:C, :] if d == 0 else e_pos[0:1, :]
        for h in range(H):
            sl = slice(h * N, (h + 1) * N)
            ch.append(dict(d=d, h=h, incl=incl, strict=strict, at=at[:, sl], rt=rt[:, sl], bt=bt[:, sl],
                           kt=kt[:, sl], v=vv[:, sl], gam=gam[:, sl], s0=s_scr[d, h]))
    for c in ch:
        a_all = _dot_nt(jnp.concatenate([c["at"], c["rt"]], axis=0), jnp.concatenate([c["bt"], c["kt"]], axis=0))
        c["l_pow"] = jnp.where(c["strict"], a_all[:C, :C], 0.0)
        c["l_ak"] = jnp.where(c["strict"], a_all[:C, C:], 0.0).astype(BF16)
        c["m_rb"] = jnp.where(c["incl"], a_all[C:, :C], 0.0).astype(BF16)
        c["m_rk"] = jnp.where(c["incl"], a_all[C:, C:], 0.0).astype(BF16)
        c["t_inv"] = eye + c["l_pow"]
    for _ in range(n_sq):
        for c in ch:
            lb = c["l_pow"].astype(BF16)
            c["l_pow"] = _dot(lb, lb)
        for c in ch:
            c["t_inv"] = c["t_inv"] + _dot(c["t_inv"].astype(BF16), c["l_pow"].astype(BF16))
    for c in ch:
        c["s0b"] = c["s0"].astype(BF16)
        c["w1"] = _dot_nt(c["at"], c["s0b"]) + _dot(c["l_ak"], c["v"])
    for c in ch:
        c["u"] = _dot(c["t_inv"].astype(BF16), c["w1"].astype(BF16)).astype(BF16)
    for c in ch:
        s_scr[c["d"], c["h"]] = (c["s0"] + _dot_tn(c["u"], c["bt"]) + _dot_tn(c["v"], c["kt"])) * c["gam"]
    for c in ch:
        c["y"] = _dot_nt(c["rt"], c["s0b"]) + _dot(c["m_rb"], c["u"]) + _dot(c["m_rk"], c["v"])
    yf_ref[...] = jnp.concatenate([c["y"] for c in ch[:H]], axis=1)
    yb_ref[...] = jnp.concatenate([c["y"] for c in ch[H:]], axis=1)


def _rwkv_scan(r, v, kk, lwf, kf, bf, lwb, kb, bb, n_ctx):
    T, W = r.shape
    C = SCAN_CHUNK
    H = W // RWKV_HEAD
    nch = T // C
    cch = n_ctx // C
    fwd = pl.BlockSpec((C, W), lambda n: (n, 0))
    bwd = pl.BlockSpec((C, W), lambda n: (jnp.where(n < cch, cch - 1 - n, nch - 1 + cch - n), 0))
    return pl.pallas_call(
        functools.partial(_rwkv_scan_kernel, C=C, H=H, N=RWKV_HEAD),
        grid=(nch,),
        in_specs=[fwd] * 6 + [bwd] * 6,
        out_specs=[fwd, bwd],
        out_shape=[jax.ShapeDtypeStruct((T, W), F32)] * 2,
        scratch_shapes=[pltpu.VMEM((2, H, RWKV_HEAD, RWKV_HEAD), F32)],
        compiler_params=_cp(("arbitrary",)),
        name="rwkv_scan",
    )(r, v, kk, lwf, kf, bf, r, v, kk, lwb, kb, bb)


def _rwkv_post_kernel(yf_ref, yb_ref, bonus_ref, gate_ref, lng_ref, lnb_ref, gh_ref, o_ref):
    gh = gh_ref[...]
    y = yf_ref[...] + yb_ref[...]
    mean = _seg_sum(y, gh) * (1.0 / RWKV_HEAD)
    yc = y - mean
    var = _seg_sum(yc * yc, gh) * (1.0 / RWKV_HEAD)
    yn = yc * lax.rsqrt(var + RWKV_GN_EPS) * lng_ref[...] + lnb_ref[...]
    o_ref[...] = ((yn + bonus_ref[...]) * gate_ref[...]).astype(o_ref.dtype)


def _rwkv_post(yf, yb, bonus, gate, lng, lnb, gh, tm):
    T, W = yf.shape
    row = pl.BlockSpec((tm, W), lambda i: (i, 0))
    full = lambda a: pl.BlockSpec(a.shape, lambda i: (0,) * a.ndim)
    return pl.pallas_call(
        _rwkv_post_kernel,
        grid=(T // tm,),
        in_specs=[row] * 4 + [full(lng), full(lnb), full(gh)],
        out_specs=row,
        out_shape=jax.ShapeDtypeStruct((T, W), BF16),
        compiler_params=_cp(("arbitrary",)),
        name="rwkv_post",
    )(yf, yb, bonus, gate, lng, lnb, gh)


def _gmlp_kernel(p_ref, g_ref, ws_ref, b_ref, o_ref, *, tm, W):
    u = jax.nn.gelu(p_ref[:, 0:W])
    v = jax.nn.gelu(p_ref[:, W:2 * W])
    v = v * lax.rsqrt(jnp.mean(v * v, axis=-1, keepdims=True) + NORM_EPS) * g_ref[...]
    vb = v.astype(BF16)
    gw = W // GM_GROUPS
    for c in range(tm // GM_CHUNK):
        rows = slice(c * GM_CHUNK, (c + 1) * GM_CHUNK)
        parts = [_dot(ws_ref[g], vb[rows, g * gw:(g + 1) * gw]) for g in range(GM_GROUPS)]
        s = jnp.concatenate(parts, axis=1) + b_ref[...]
        o_ref[rows, :] = (u[rows, :] * s).astype(o_ref.dtype)


def _gmlp(P, col_block, g, ws, bias, tm):
    T = P.shape[0]
    W = g.shape[1]
    return pl.pallas_call(
        functools.partial(_gmlp_kernel, tm=tm, W=W),
        grid=(T // tm,),
        in_specs=[pl.BlockSpec((tm, 2 * W), lambda i: (i, col_block)),
                  pl.BlockSpec((1, W), lambda i: (0, 0)),
                  pl.BlockSpec(ws.shape, lambda i: (0, 0, 0)),
                  pl.BlockSpec(bias.shape, lambda i: (0, 0))],
        out_specs=pl.BlockSpec((tm, W), lambda i: (i, 0)),
        out_shape=jax.ShapeDtypeStruct((T, W), BF16),
        compiler_params=_cp(("arbitrary",)),
        name="gmlp",
    )(P, g, ws, bias)


def _rope(x, cos, sin, lane):
    w = x.shape[1]
    partner = jnp.where((lane % 32) < 16, pltpu.roll(x, w - 16, axis=1), pltpu.roll(x, 16, axis=1))
    return x * cos + partner * sin


def _qk_prep_kernel(q_ref, k_ref, cos_ref, sin_ref, qg_ref, kg_ref, ghq_ref, ghk_ref, qo_ref, ko_ref, *, scale):
    cos = cos_ref[...]
    sin = sin_ref[...]
    for x_ref, g_ref, gh_ref, o_ref, mul in ((q_ref, qg_ref, ghq_ref, qo_ref, scale), (k_ref, kg_ref, ghk_ref, ko_ref, 1.0)):
        x = x_ref[...]
        w = x.shape[1]
        ss = _seg_sum(x * x, gh_ref[...]) * (1.0 / ATTN_HEAD)
        xn = x * lax.rsqrt(ss + NORM_EPS) * g_ref[...]
        rep = w // cos.shape[1]
        lane = lax.broadcasted_iota(jnp.int32, x.shape, 1)
        xr = _rope(xn, jnp.tile(cos, (1, rep)), jnp.tile(sin, (1, rep)), lane)
        o_ref[...] = (xr * mul).astype(o_ref.dtype)


def _qk_prep(P, q_block, k_block, cos, sin, qg, kg, ghq, ghk, tm):
    T = P.shape[0]
    QW, KW = qg.shape[1], kg.shape[1]
    full = lambda a: pl.BlockSpec(a.shape, lambda i: (0,) * a.ndim)
    return pl.pallas_call(
        functools.partial(_qk_prep_kernel, scale=ATTN_HEAD ** -0.5),
        grid=(T // tm,),
        in_specs=[pl.BlockSpec((tm, QW), lambda i: (i, q_block)),
                  pl.BlockSpec((tm, KW), lambda i: (i, k_block)),
                  pl.BlockSpec((tm, cos.shape[1]), lambda i: (i, 0)),
                  pl.BlockSpec((tm, sin.shape[1]), lambda i: (i, 0)),
                  full(qg), full(kg), full(ghq), full(ghk)],
        out_specs=[pl.BlockSpec((tm, QW), lambda i: (i, 0)), pl.BlockSpec((tm, KW), lambda i: (i, 0))],
        out_shape=[jax.ShapeDtypeStruct((T, QW), BF16), jax.ShapeDtypeStruct((T, KW), BF16)],
        compiler_params=_cp(("arbitrary",)),
        name="qk_prep",
    )(P, P, cos, sin, qg, kg, ghq, ghk)


def _attn_block(i, nb, sink_ref, q_ref, k_refs, v_refs, o_ref, local):
    L = ATTN_BLOCK
    G = ATTN_GROUP
    hd = ATTN_HEAD
    n_kv = k_refs[-1].shape[1] // hd
    R = G * L
    srow = lax.broadcasted_iota(jnp.int32, (R, 1), 0)
    if local:
        qi = lax.broadcasted_iota(jnp.int32, (R, 3 * L), 0) % L
        kj = lax.broadcasted_iota(jnp.int32, (R, 3 * L), 1)
        rel = kj - L - qi
        valid = (rel <= ATTN_WINDOW) & (rel >= -ATTN_WINDOW)
        valid = valid & ((kj >= L) | (i > 0)) & ((kj < 2 * L) | (i < nb - 1))
    for j in range(n_kv):
        ks = [r[:, j * hd:(j + 1) * hd] for r in k_refs]
        vs = [r[:, j * hd:(j + 1) * hd].astype(BF16) for r in v_refs]
        q = jnp.concatenate([q_ref[:, (j * G + g) * hd:(j * G + g + 1) * hd] for g in range(G)], axis=0)
        sink = jnp.zeros((R, 1), F32)
        for g in range(G):
            sink = jnp.where((srow >= g * L) & (srow < (g + 1) * L), sink_ref[j * G + g], sink)
        s_ctx = _dot_nt(q, ks[-1])
        m = jnp.maximum(jnp.max(s_ctx, axis=1, keepdims=True), sink)
        if local:
            s_loc = _dot_nt(q, jnp.concatenate(ks[:3], axis=0))
            s_loc = jnp.where(valid, s_loc, -1e30)
            m = jnp.maximum(m, jnp.max(s_loc, axis=1, keepdims=True))
            p_loc = jnp.exp(s_loc - m)
        p_ctx = jnp.exp(s_ctx - m)
        den = jnp.sum(p_ctx, axis=1, keepdims=True) + jnp.exp(sink - m)
        acc = _dot(p_ctx.astype(BF16), vs[-1])
        if local:
            den = den + jnp.sum(p_loc, axis=1, keepdims=True)
            acc = acc + _dot(p_loc.astype(BF16), jnp.concatenate(vs[:3], axis=0))
        out = acc / den
        for g in range(G):
            h = j * G + g
            o_ref[:, h * hd:(h + 1) * hd] = out[g * L:(g + 1) * L, :].astype(o_ref.dtype)


def _attn_kernel(sink_ref, q_ref, kp_ref, kc_ref, kn_ref, kx_ref, vp_ref, vc_ref, vn_ref, vx_ref, o_ref, *, cb, nb):
    i = pl.program_id(0)

    @pl.when(i < cb)
    def _():
        _attn_block(i, nb, sink_ref, q_ref, (kx_ref,), (vx_ref,), o_ref, False)

    @pl.when(i >= cb)
    def _():
        _attn_block(i - cb, nb, sink_ref, q_ref, (kp_ref, kc_ref, kn_ref, kx_ref),
                    (vp_ref, vc_ref, vn_ref, vx_ref), o_ref, True)


def _attention(qr, kr, P, v_block, sink, n_ctx):
    T, QW = qr.shape
    KW = kr.shape[1]
    L = ATTN_BLOCK
    cb = n_ctx // L
    nb = (T - n_ctx) // L
    lo, hi = cb, cb + nb - 1
    shifts = (lambda i: jnp.clip(i - 1, lo, hi), lambda i: jnp.clip(i, lo, hi), lambda i: jnp.clip(i + 1, lo, hi))
    kspec = lambda f: pl.BlockSpec((L, KW), lambda i: (f(i), 0))
    vspec = lambda f: pl.BlockSpec((L, KW), lambda i: (f(i), v_block))
    in_specs = ([pl.BlockSpec(memory_space=pltpu.SMEM), pl.BlockSpec((L, QW), lambda i: (i, 0))]
                + [kspec(f) for f in shifts] + [pl.BlockSpec((n_ctx, KW), lambda i: (0, 0))]
                + [vspec(f) for f in shifts] + [pl.BlockSpec((n_ctx, KW), lambda i: (0, v_block))])
    return pl.pallas_call(
        functools.partial(_attn_kernel, cb=cb, nb=nb),
        grid=(cb + nb,),
        in_specs=in_specs,
        out_specs=pl.BlockSpec((L, QW), lambda i: (i, 0)),
        out_shape=jax.ShapeDtypeStruct((T, QW), BF16),
        compiler_params=_cp(("arbitrary",)),
        name="attn",
    )(sink, qr, kr, kr, kr, kr, P, P, P, P)


def _oproj_kernel(x_ref, gt_ref, a_ref, b_ref, c_ref, wa_ref, wb_ref, wc_ref, o_ref, *, tm, n_ctx):
    acc = _dot(a_ref[...], wa_ref[...]) + _dot(b_ref[...], wb_ref[...]) + _dot(c_ref[...], wc_ref[...])
    gate = _pick_mod(gt_ref, _ctx_rows(pl.program_id(0), tm, n_ctx))
    o_ref[...] = x_ref[...] + gate * acc


def _oproj(x, mods, l, y_rw, y_gm, y_at, w_o, n_ctx, tm):
    T, D = x.shape
    tn = 512
    W1 = y_rw.shape[1]
    W3 = y_at.shape[1]
    gate_col = 2 * (D // tn)
    return pl.pallas_call(
        functools.partial(_oproj_kernel, tm=tm, n_ctx=n_ctx),
        grid=(T // tm, D // tn),
        in_specs=[pl.BlockSpec((tm, tn), lambda i, j: (i, j)),
                  pl.BlockSpec((1, 8, tn), lambda i, j: (l, 0, gate_col + j)),
                  pl.BlockSpec((tm, W1), lambda i, j: (i, 0)),
                  pl.BlockSpec((tm, W1), lambda i, j: (i, 0)),
                  pl.BlockSpec((tm, W3), lambda i, j: (i, 0)),
                  pl.BlockSpec((W1, tn), lambda i, j: (0, j)),
                  pl.BlockSpec((W1, tn), lambda i, j: (1, j)),
                  pl.BlockSpec((W3, tn), lambda i, j: (1, j))],
        out_specs=pl.BlockSpec((tm, tn), lambda i, j: (i, j)),
        out_shape=jax.ShapeDtypeStruct((T, D), F32),
        compiler_params=_cp(("arbitrary", "arbitrary")),
        name="oproj",
    )(x, mods, y_rw, y_gm, y_at, w_o, w_o, w_o)


def _router_kernel(x_ref, sh_ref, sc_ref, g_ref, rt_ref, bias_ref, tri_ref,
                   h_ref, idx_ref, wt_ref, rank_ref, cnt_ref, carry, *, tm, n_ctx):
    i = pl.program_id(0)

    @pl.when(i == 0)
    def _():
        carry[...] = jnp.zeros_like(carry)

    is_ctx = _ctx_rows(i, tm, n_ctx)
    h = _norm_mod(x_ref[...], g_ref[...], _pick_mod(sh_ref, is_ctx), _pick_mod(sc_ref, is_ctx))
    h_ref[...] = h
    E = N_EXPERTS
    pg = E // N_EXPERT_GROUPS
    neg = -jnp.inf
    scores = _sigmoid(_dot_nt(rt_ref[...], h, HI))
    biased = scores + bias_ref[:, 0:1]
    b3 = biased.reshape(N_EXPERT_GROUPS, pg, tm)
    i3 = lax.broadcasted_iota(jnp.int32, b3.shape, 1)
    m1 = jnp.max(b3, axis=1, keepdims=True)
    first = jnp.min(jnp.where(b3 == m1, i3, pg), axis=1, keepdims=True)
    m2 = jnp.max(jnp.where(i3 == first, neg, b3), axis=1, keepdims=True)
    gs = (m1 + m2).reshape(N_EXPERT_GROUPS, tm)
    gi = lax.broadcasted_iota(jnp.int32, gs.shape, 0)
    gsel = jnp.zeros(gs.shape, jnp.bool_)
    for _ in range(TOPK_GROUPS):
        gm = jnp.max(gs, axis=0, keepdims=True)
        gfirst = jnp.min(jnp.where(gs == gm, gi, N_EXPERT_GROUPS), axis=0, keepdims=True)
        hit = gi == gfirst
        gsel = gsel | hit
        gs = jnp.where(hit, neg, gs)
    masked = jnp.where(gsel.reshape(N_EXPERT_GROUPS, 1, tm), b3, neg).reshape(E, tm)
    ei = lax.broadcasted_iota(jnp.int32, (E, tm), 0)
    sel = jnp.zeros((E, tm), jnp.bool_)
    picks = []
    for _ in range(TOP_K):
        mx = jnp.max(masked, axis=0, keepdims=True)
        efirst = jnp.min(jnp.where(masked == mx, ei, E), axis=0, keepdims=True)
        hit = ei == efirst
        sel = sel | hit
        masked = jnp.where(hit, neg, masked)
        picks.append((efirst, hit, jnp.sum(jnp.where(hit, scores, 0.0), axis=0, keepdims=True)))
    wsum = picks[0][2]
    for pk in picks[1:]:
        wsum = wsum + pk[2]
    self_f = jnp.where(sel, 1.0, 0.0)
    rank_dense = carry[:, 0:1] + _dot(self_f.astype(BF16), tri_ref[...])
    carry[...] = carry[...] + jnp.sum(self_f, axis=1, keepdims=True)
    cnt_ref[...] = carry[...]
    for kx, (efirst, hit, wk) in enumerate(picks):
        idx_ref[kx:kx + 1, :] = efirst
        wt_ref[kx:kx + 1, :] = wk / wsum * ROUTED_SCALE
        rank_ref[kx:kx + 1, :] = jnp.sum(jnp.where(hit, rank_dense, 0.0), axis=0, keepdims=True).astype(jnp.int32)


def _router(x, mods, l, g, router_t, bias, tri, n_ctx, tm):
    T, D = x.shape
    E = router_t.shape[0]
    kspec = pl.BlockSpec((TOP_K, tm), lambda i: (0, i))
    return pl.pallas_call(
        functools.partial(_router_kernel, tm=tm, n_ctx=n_ctx),
        grid=(T // tm,),
        in_specs=[pl.BlockSpec((tm, D), lambda i: (i, 0)),
                  pl.BlockSpec((1, 8, D), lambda i: (l, 0, 3)),
                  pl.BlockSpec((1, 8, D), lambda i: (l, 0, 4)),
                  pl.BlockSpec((1, D), lambda i: (0, 0)),
                  pl.BlockSpec((E, D), lambda i: (0, 0)),
                  pl.BlockSpec((E, 128), lambda i: (0, 0)),
                  pl.BlockSpec((tm, tm), lambda i: (0, 0))],
        out_specs=[pl.BlockSpec((tm, D), lambda i: (i, 0)), kspec, kspec, kspec,
                   pl.BlockSpec((E, 128), lambda i: (0, 0))],
        out_shape=[jax.ShapeDtypeStruct((T, D), F32),
                   jax.ShapeDtypeStruct((TOP_K, T), jnp.int32),
                   jax.ShapeDtypeStruct((TOP_K, T), F32),
                   jax.ShapeDtypeStruct((TOP_K, T), jnp.int32),
                   jax.ShapeDtypeStruct((E, 128), F32)],
        scratch_shapes=[pltpu.VMEM((E, 128), F32)],
        compiler_params=_cp(("arbitrary",)),
        name="router",
    )(x, mods, mods, g, router_t, bias, tri)


def _gather_slabs(idx_ref, n, src_hbm, dst, sem):
    def body(r, carry):
        src = src_hbm.at[pl.ds(pl.multiple_of(idx_ref[r], SLAB), SLAB), :]
        pltpu.make_async_copy(src, dst.at[pl.ds(pl.multiple_of(r * SLAB, SLAB), SLAB), :], sem).start()
        return carry
    lax.fori_loop(0, n, body, 0, unroll=8)


def _experts_kernel(be_ref, nu_ref, tokc_ref, tokn_ref, h_hbm, wg_ref, wu_ref, wd_ref, y_ref,
                    hbuf, sem, wg_s, wu_s, wd_s, *, TB, nblk):
    b = pl.program_id(0)
    slot = b % 2
    n_used = nu_ref[0]
    SW = h_hbm.shape[1]

    @pl.when(b == 0)
    def _():
        _gather_slabs(tokc_ref.at[0, 0], TB, h_hbm, hbuf.at[0], sem.at[0])

    @pl.when(b + 1 < n_used)
    def _():
        _gather_slabs(tokn_ref.at[0, 0], TB, h_hbm, hbuf.at[1 - slot], sem.at[1 - slot])

    new_expert = (b == 0) | (be_ref[b] != be_ref[jnp.maximum(b - 1, 0)])

    @pl.when(new_expert & (b < n_used))
    def _():
        wg_s[...] = wg_ref[0, 0].astype(BF16)
        wu_s[...] = wu_ref[0, 0].astype(BF16)
        wd_s[...] = wd_ref[0, 0].astype(BF16)

    @pl.when(b < n_used)
    def _():
        pltpu.make_async_copy(h_hbm.at[pl.ds(0, TB * SLAB), :], hbuf.at[slot], sem.at[slot]).wait()
        a = u = None
        for s in range(0, SLAB, 2):
            xs = jnp.concatenate([hbuf[slot, pl.ds(s, TB, stride=SLAB), :], hbuf[slot, pl.ds(s + 1, TB, stride=SLAB), :]],
                                 axis=1).astype(BF16)
            da = _dot(xs, wg_s[s * SW:(s + 2) * SW, :])
            du = _dot(xs, wu_s[s * SW:(s + 2) * SW, :])
            a = da if a is None else a + da
            u = du if u is None else u + du
        act = (a * _sigmoid(a) * u).astype(BF16)
        yv = _dot(act, wd_s[...])
        for s in range(SLAB):
            y_ref[pl.ds(s, TB, stride=SLAB), :] = yv[:, s * SW:(s + 1) * SW]

    @pl.when(b >= n_used)
    def _():
        y_ref[...] = jnp.zeros_like(y_ref)


def _experts(h2s, tok_rows, block_e, n_used, wg, wu, wd, l):
    SW = h2s.shape[1]
    D = SW * SLAB
    nblk = block_e.shape[0]
    TB = EXPERT_TILE
    FF = wg.shape[3]
    tok3 = tok_rows.reshape(nblk, 1, TB)
    grid_spec = pltpu.PrefetchScalarGridSpec(
        num_scalar_prefetch=2,
        grid=(nblk,),
        in_specs=[pl.BlockSpec((1, 1, TB), lambda b, be, nu: (b, 0, 0), memory_space=pltpu.SMEM),
                  pl.BlockSpec((1, 1, TB), lambda b, be, nu: (jnp.minimum(b + 1, nblk - 1), 0, 0),
                               memory_space=pltpu.SMEM),
                  pl.BlockSpec(memory_space=pl.ANY),
                  pl.BlockSpec((1, 1, D, FF), lambda b, be, nu: (l, be[b], 0, 0)),
                  pl.BlockSpec((1, 1, D, FF), lambda b, be, nu: (l, be[b], 0, 0)),
                  pl.BlockSpec((1, 1, FF, D), lambda b, be, nu: (l, be[b], 0, 0))],
        out_specs=pl.BlockSpec((TB * SLAB, SW), lambda b, be, nu: (b, 0)),
        scratch_shapes=[pltpu.VMEM((2, TB * SLAB, SW), F32), pltpu.SemaphoreType.DMA((2,)),
                        pltpu.VMEM((D, FF), BF16), pltpu.VMEM((D, FF), BF16), pltpu.VMEM((FF, D), BF16)],
    )
    return pl.pallas_call(
        functools.partial(_experts_kernel, TB=TB, nblk=nblk),
        grid_spec=grid_spec,
        out_shape=jax.ShapeDtypeStruct((nblk * TB * SLAB, SW), F32),
        compiler_params=_cp(("arbitrary",)),
        name="experts",
    )(block_e, n_used, tok3, tok3, h2s, wg, wu, wd)


def _combine_kernel(slc_ref, sln_ref, y_hbm, x_ref, h_ref, wt_ref, gt_ref, swg_ref, swu_ref, swd_ref, o_ref,
                    ybuf, sem, *, tm, n_ctx, nt):
    i = pl.program_id(0)
    slot = i % 2
    SW = y_hbm.shape[1]

    @pl.when(i == 0)
    def _():
        for k in range(TOP_K):
            _gather_slabs(slc_ref.at[0, k], tm, y_hbm, ybuf.at[0, k], sem.at[0])

    @pl.when(i + 1 < nt)
    def _():
        for k in range(TOP_K):
            _gather_slabs(sln_ref.at[0, k], tm, y_hbm, ybuf.at[1 - slot, k], sem.at[1 - slot])

    hb = h_ref[...].astype(BF16)
    a = _dot(hb, swg_ref[...])
    u = _dot(hb, swu_ref[...])
    shared = _dot((a * _sigmoid(a) * u).astype(BF16), swd_ref[...])
    gate = _pick_mod(gt_ref, _ctx_rows(i, tm, n_ctx))
    wk = [jnp.broadcast_to(wt_ref[:, k:k + 1], (tm, SW)) for k in range(TOP_K)]
    for k in range(TOP_K):
        pltpu.make_async_copy(y_hbm.at[pl.ds(0, tm * SLAB), :], ybuf.at[slot, k], sem.at[slot]).wait()
    for s in range(SLAB):
        cols = slice(s * SW, (s + 1) * SW)
        acc = shared[:, cols]
        for k in range(TOP_K):
            acc = acc + wk[k] * ybuf[slot, k, pl.ds(s, tm, stride=SLAB), :]
        o_ref[:, cols] = x_ref[:, cols] + gate[:, cols] * acc


def _combine(y2, slot_rows, wt_t, x, h2, mods, l, swg, swu, swd, n_ctx, tm):
    T, D = x.shape
    SW = y2.shape[1]
    nt = T // tm
    FF = swg.shape[1]
    sl3 = slot_rows.reshape(TOP_K, nt, tm).transpose(1, 0, 2)
    return pl.pallas_call(
        functools.partial(_combine_kernel, tm=tm, n_ctx=n_ctx, nt=nt),
        grid=(nt,),
        in_specs=[pl.BlockSpec((1, TOP_K, tm), lambda i: (i, 0, 0), memory_space=pltpu.SMEM),
                  pl.BlockSpec((1, TOP_K, tm), lambda i: (jnp.minimum(i + 1, nt - 1), 0, 0),
                               memory_space=pltpu.SMEM),
                  pl.BlockSpec(memory_space=pl.ANY),
                  pl.BlockSpec((tm, D), lambda i: (i, 0)),
                  pl.BlockSpec((tm, D), lambda i: (i, 0)),
                  pl.BlockSpec((tm, TOP_K), lambda i: (i, 0)),
                  pl.BlockSpec((1, 8, D), lambda i: (l, 0, 5)),
                  pl.BlockSpec((D, FF), lambda i: (0, 0)),
                  pl.BlockSpec((D, FF), lambda i: (0, 0)),
                  pl.BlockSpec((FF, D), lambda i: (0, 0))],
        out_specs=pl.BlockSpec((tm, D), lambda i: (i, 0)),
        out_shape=jax.ShapeDtypeStruct((T, D), F32),
        scratch_shapes=[pltpu.VMEM((2, TOP_K, tm * SLAB, SW), F32), pltpu.SemaphoreType.DMA((2,))],
        compiler_params=_cp(("arbitrary",)),
        name="combine",
    )(sl3, sl3, y2, x, h2, wt_t, mods, swg, swu, swd)


def _seg_indicator(width, seg):
    idx = np.arange(width) // seg
    return jnp.asarray(idx[:, None] == idx[None, :], dtype=BF16)


def _rope_tables(n_ctx, n_lat):
    rows = n_lat // GRID_W
    row = jnp.repeat(jnp.arange(rows, dtype=F32), GRID_W)
    col = jnp.tile(jnp.arange(GRID_W, dtype=F32), rows)
    n_freq = ATTN_HEAD // 4
    inv_freq = ROPE_THETA ** (-jnp.arange(n_freq, dtype=F32) / n_freq)
    ar, ac = row[:, None] * inv_freq, col[:, None] * inv_freq
    cos = jnp.concatenate([jnp.cos(ar), jnp.cos(ar), jnp.cos(ac), jnp.cos(ac)], axis=1)
    sin = jnp.concatenate([-jnp.sin(ar), jnp.sin(ar), -jnp.sin(ac), jnp.sin(ac)], axis=1)
    cos = jnp.concatenate([jnp.ones((n_ctx, ATTN_HEAD), F32), cos], axis=0)
    sin = jnp.concatenate([jnp.zeros((n_ctx, ATTN_HEAD), F32), sin], axis=0)
    return jnp.tile(cos, (1, 2)), jnp.tile(sin, (1, 2))


def kernel(x, c, ctx, c_ctx, w_ada, b_ada, norm1_g, norm2_g, w_in, w_o, rw_mu, rw_w0, rw_w2, rw_a0, rw_a2, rw_g2, rw_kk, rw_ka, rw_rk, rw_ln_g, rw_ln_b, gm_norm_g, gm_ws, gm_b, at_qn, at_kn, at_sink, moe_router, moe_bias, moe_wg, moe_wu, moe_wd, sh_wg, sh_wu, sh_wd):
    B, S, D = x.shape
    C = ctx.shape[1]
    assert B == 1
    L = w_ada.shape[0]
    T = C + S
    RW = rw_kk.shape[1]
    GW = gm_norm_g.shape[1]
    QW = at_sink.shape[1] * ATTN_HEAD
    KW = QW // ATTN_GROUP
    rw_proj = rw_mu.shape[2]
    RWP = 2048
    assert rw_proj <= RWP and T % 768 == 0 and C % 256 == 0 and S % 256 == 0

    xs = jnp.concatenate([ctx[0], x[0]], axis=0)
    cond8 = jnp.zeros((8, D), F32).at[0].set(c_ctx).at[1].set(c[0])
    mods = _ada(cond8, w_ada, b_ada)

    gh_rw = _seg_indicator(RW, RWKV_HEAD)
    gh_q = _seg_indicator(QW, ATTN_HEAD)
    gh_k = _seg_indicator(KW, ATTN_HEAD)
    cos, sin = _rope_tables(C, S)
    tri = jnp.asarray(np.arange(256)[:, None] < np.arange(256)[None, :], dtype=BF16)
    TB = EXPERT_TILE
    nblk = -(-(T * TOP_K) // TB) + N_EXPERTS

    for l in range(L):
        w_in_l = w_in[l]
        w_in_p = jnp.concatenate([w_in_l[:, :rw_proj], jnp.zeros((D, RWP - rw_proj), F32), w_in_l[:, rw_proj:]],
                                 axis=1).astype(BF16)
        P = _inproj(xs, mods, l, norm1_g[l][None], w_in_p, C, 768)
        gm_block = RWP // (2 * GW)
        q_block = (RWP + 2 * GW) // QW
        k_block = (RWP + 2 * GW + QW) // KW
        v_block = k_block + 1

        pad = lambda a: jnp.pad(a, ((0, 0), (0, RWP - rw_proj)))
        prm = {"mu": pad(rw_mu[l]), "w0": rw_w0[l], "w2": rw_w2[l], "a0": rw_a0[l], "a2": rw_a2[l],
               "g2": rw_g2[l], "kk": rw_kk[l][None], "ka": rw_ka[l][None], "rk": rw_rk[l].reshape(1, RW)}
        r, v, kk, lwf, kf, bf, lwb, kb, bb, gate, bonus = _rwkv_prep(P, prm, gh_rw, C, 256)
        yf, yb = _rwkv_scan(r, v, kk, lwf, kf, bf, lwb, kb, bb, C)
        y_rw = _rwkv_post(yf, yb, bonus, gate, rw_ln_g[l][None], rw_ln_b[l][None], gh_rw, 256)

        gm_bias = jnp.repeat(gm_b[l].T, GW // GM_GROUPS, axis=1)
        y_gm = _gmlp(P, gm_block, gm_norm_g[l][None], gm_ws[l].astype(BF16), gm_bias, 256)

        qg = jnp.tile(at_qn[l], QW // ATTN_HEAD)[None]
        kg = jnp.tile(at_kn[l], KW // ATTN_HEAD)[None]
        qr, kr = _qk_prep(P, q_block, k_block, cos, sin, qg, kg, gh_q, gh_k, 256)
        y_at = _attention(qr, kr, P, v_block, at_sink[l], C)

        xs = _oproj(xs, mods, l, y_rw, y_gm, y_at, w_o[l].astype(BF16), C, 768)

        bias_col = jnp.broadcast_to(moe_bias[l][:, None], (N_EXPERTS, 128))
        h2, idx8, wt8, rank8, cnt = _router(xs, mods, l, norm2_g[l][None], moe_router[l].T, bias_col, tri, C, 256)
        counts = cnt[:, 0].astype(jnp.int32)
        padded = (counts + TB - 1) // TB * TB
        pad_end = jnp.cumsum(padded)
        pad_start = pad_end - padded
        e_ids = jnp.arange(N_EXPERTS, dtype=jnp.int32)
        slot8 = jnp.sum(jnp.where(idx8[:, :, None] == e_ids, pad_start, 0), axis=-1) + rank8
        tok = jnp.zeros((nblk * TB,), jnp.int32).at[slot8.reshape(-1)].set(
            jnp.tile(jnp.arange(T, dtype=jnp.int32), TOP_K), unique_indices=True)
        blk_pos = jnp.arange(nblk, dtype=jnp.int32) * TB
        block_e = jnp.minimum(jnp.sum((pad_end[None, :] <= blk_pos[:, None]).astype(jnp.int32), axis=1), N_EXPERTS - 1)
        n_used = (pad_end[-1] // TB).astype(jnp.int32).reshape(1)
        h2s = h2.reshape(T * SLAB, D // SLAB)
        y2 = _experts(h2s, tok * SLAB, block_e, n_used, moe_wg, moe_wu, moe_wd, l)
        xs = _combine(y2, slot8 * SLAB, wt8.T, xs, h2, mods, l, sh_wg[l].astype(BF16), sh_wu[l].astype(BF16),
                      sh_wd[l].astype(BF16), C, 128)
    return xs[C:].reshape(B, S, D)
```

```python
import functools

import jax
import jax.numpy as jnp
import numpy as np
from jax import lax
from jax.experimental import pallas as pl
from jax.experimental.pallas import tpu as pltpu

F32 = jnp.float32
BF16 = jnp.bfloat16
HI = lax.Precision.HIGHEST

NORM_EPS = 1e-6
GRID_W = 64

RWKV_HEAD = 64
DECAY_LORA = 64
ICLR_LORA = 64
GATE_LORA = 128
RWKV_GN_EPS = 64e-5

GM_GROUPS = 8
GM_CHUNK = 128

ATTN_HEAD = 64
ATTN_GROUP = 4
ATTN_WINDOW = 128
ATTN_BLOCK = 128
ROPE_THETA = 10000.0

N_EXPERTS = 64
TOP_K = 8
N_EXPERT_GROUPS = 8
TOPK_GROUPS = 4
ROUTED_SCALE = 2.5

SCAN_CHUNK = 64
EXPERT_TILE = 256
SLAB = 8
PITCH = 12
VMEM_LIMIT = 56 * 1024 * 1024


def _cp(sem, vmem=VMEM_LIMIT):
    return pltpu.CompilerParams(dimension_semantics=sem, vmem_limit_bytes=vmem)


def _dot(a, b, prec=None):
    return jnp.dot(a, b, preferred_element_type=F32, precision=prec)


def _dot_nt(a, b, prec=None):
    return lax.dot_general(a, b, (((1,), (1,)), ((), ())), preferred_element_type=F32, precision=prec)


def _dot_tn(a, b, prec=None):
    return lax.dot_general(a, b, (((0,), (0,)), ((), ())), preferred_element_type=F32, precision=prec)


def _seg_sum(x, g):
    xh = x.astype(BF16)
    xl = (x - xh.astype(F32)).astype(BF16)
    return _dot(xh, g) + _dot(xl, g)


def _sigmoid(x):
    return jax.nn.sigmoid(x)


def _pack_halves(x):
    half = x.shape[1] // 2
    lo = lax.bitcast_convert_type(x[:, :half].astype(BF16).astype(F32), jnp.uint32)
    hi = lax.bitcast_convert_type(x[:, half:].astype(BF16).astype(F32), jnp.uint32)
    return hi | (lo >> 16)


def _unpack_lo(w):
    return lax.bitcast_convert_type(w << 16, F32)


def _unpack_hi(w):
    return lax.bitcast_convert_type(w & jnp.uint32(0xFFFF0000), F32)


def _store_slabs(ref, words, n):
    for c in range(SLAB):
        ref[pl.ds(c, n, stride=SLAB), :] = words[:, c * 128:(c + 1) * 128]


def _norm_mod(x, g, sh, sc):
    y = x * lax.rsqrt(jnp.mean(x * x, axis=-1, keepdims=True) + NORM_EPS)
    return (y * g) * (1.0 + sc) + sh


def _ctx_rows(i, tm, n_ctx):
    row = i * tm + lax.broadcasted_iota(jnp.int32, (tm, 1), 0)
    return row < n_ctx


def _pick_mod(mod_ref, is_ctx):
    return jnp.where(is_ctx, mod_ref[0, 0:1, :], mod_ref[0, 1:2, :])


def _ada_kernel(c_ref, w_ref, b_ref, o_ref):
    c = c_ref[...]
    s = c * _sigmoid(c)
    o_ref[0] = _dot(s, w_ref[0], HI) + b_ref[0]


def _ada(cond8, w_ada, b_ada):
    L, D, N = w_ada.shape
    tn = 1024
    return pl.pallas_call(
        _ada_kernel,
        grid=(L, N // tn),
        in_specs=[pl.BlockSpec((8, D), lambda l, j: (0, 0)),
                  pl.BlockSpec((1, D, tn), lambda l, j: (l, 0, j)),
                  pl.BlockSpec((1, 1, tn), lambda l, j: (l, 0, j))],
        out_specs=pl.BlockSpec((1, 8, tn), lambda l, j: (l, 0, j)),
        out_shape=jax.ShapeDtypeStruct((L, 8, N), F32),
        compiler_params=_cp(("arbitrary", "arbitrary")),
        name="ada",
    )(cond8, w_ada, b_ada.reshape(L, 1, N))


def _inproj_kernel(x_ref, sh_ref, sc_ref, g_ref, w_ref, o_ref, h_scr, *, tm, n_ctx):
    i = pl.program_id(0)

    @pl.when(pl.program_id(1) == 0)
    def _():
        is_ctx = _ctx_rows(i, tm, n_ctx)
        h = _norm_mod(x_ref[...], g_ref[...], _pick_mod(sh_ref, is_ctx), _pick_mod(sc_ref, is_ctx))
        h_scr[...] = h.astype(BF16)

    o_ref[...] = _dot(h_scr[...], w_ref[...])


def _inproj(x, mods, l, g, w, n_ctx, tm):
    T, D = x.shape
    N = w.shape[1]
    tn = 512
    return pl.pallas_call(
        functools.partial(_inproj_kernel, tm=tm, n_ctx=n_ctx),
        grid=(T // tm, N // tn),
        in_specs=[pl.BlockSpec((tm, D), lambda i, j: (i, 0)),
                  pl.BlockSpec((1, 8, D), lambda i, j: (l, 0, 0)),
                  pl.BlockSpec((1, 8, D), lambda i, j: (l, 0, 1)),
                  pl.BlockSpec((1, D), lambda i, j: (0, 0)),
                  pl.BlockSpec((D, tn), lambda i, j: (0, j))],
        out_specs=pl.BlockSpec((tm, tn), lambda i, j: (i, j)),
        out_shape=jax.ShapeDtypeStruct((T, N), F32),
        scratch_shapes=[pltpu.VMEM((tm, D), BF16)],
        compiler_params=_cp(("arbitrary", "arbitrary")),
        name="inproj",
    )(x, mods, mods, g, w)


def _softplus(x):
    return jnp.maximum(x, 0.0) + jnp.log(1.0 + jnp.exp(-jnp.abs(x)))


def _rwkv_prep_kernel(p_ref, pp_ref, pn_ref, mu_ref, w0_ref, w2_ref, a0_ref, a2_ref, g2_ref,
                      kkp_ref, kap_ref, rkp_ref, gh_ref,
                      r_ref, v_ref, kk_ref, lwf_ref, kf_ref, bf_ref, lwb_ref, kb_ref, bb_ref,
                      gate_ref, bonus_ref, *, tm, n_ctx, n_tot, W):
    i = pl.program_id(0)
    p = p_ref[...]
    lrow = lax.broadcasted_iota(jnp.int32, (tm, 1), 0)
    grow = i * tm + lrow
    prev = jnp.where(lrow == 0, pp_ref[7:8, :], pltpu.roll(p, 1, axis=0))
    prev = jnp.where((grow == 0) | (grow == n_ctx), 0.0, prev)
    nxt = jnp.where(lrow == tm - 1, pn_ref[0:1, :], pltpu.roll(p, tm - 1, axis=0))
    nxt = jnp.where((grow == n_ctx - 1) | (grow == n_tot - 1), 0.0, nxt)
    ps = p + mu_ref[0:1, :] * (prev - p) + mu_ref[1:2, :] * (nxt - p)

    r = ps[:, 0:W]
    k = ps[:, W:2 * W]
    v = ps[:, 2 * W:3 * W]
    o = 3 * W
    wd = (ps[:, o:o + DECAY_LORA], ps[:, o + DECAY_LORA:o + 2 * DECAY_LORA])
    o += 2 * DECAY_LORA
    ad = (ps[:, o:o + ICLR_LORA], ps[:, o + ICLR_LORA:o + 2 * ICLR_LORA])
    o += 2 * ICLR_LORA
    gd = ps[:, o:o + GATE_LORA]

    gh = gh_ref[...]
    kk = k * kkp_ref[...]
    kk = kk / jnp.maximum(jnp.sqrt(_seg_sum(kk * kk, gh)), 1e-12)
    r_ref[...] = r
    v_ref[...] = v
    kk_ref[...] = kk
    outs = ((lwf_ref, kf_ref, bf_ref), (lwb_ref, kb_ref, bb_ref))
    for d in range(2):
        z = w0_ref[d:d + 1, :] + _dot(jnp.tanh(wd[d]), w2_ref[d], HI)
        w_log = -_softplus(-z) - 0.5
        a = _sigmoid(a0_ref[d:d + 1, :] + _dot(ad[d], a2_ref[d], HI))
        lw_ref, kd_ref, bd_ref = outs[d]
        lw_ref[...] = -jnp.exp(w_log)
        kd_ref[...] = k * (1.0 + (a - 1.0) * kap_ref[...])
        bd_ref[...] = kk * a
    gate_ref[...] = _dot(_sigmoid(gd), g2_ref[...], HI)
    bonus_ref[...] = _seg_sum(r * k * rkp_ref[...], gh) * v


def _rwkv_prep(P, prm, gh, n_ctx, tm):
    T = P.shape[0]
    W = prm["kk"].shape[1]
    PW = 2048
    nb8 = T // 8
    row = pl.BlockSpec((tm, W), lambda i: (i, 0))
    full = lambda a: pl.BlockSpec(a.shape, lambda i: (0,) * a.ndim)
    args = (prm["mu"], prm["w0"], prm["w2"], prm["a0"], prm["a2"], prm["g2"], prm["kk"], prm["ka"], prm["rk"], gh)
    return pl.pallas_call(
        functools.partial(_rwkv_prep_kernel, tm=tm, n_ctx=n_ctx, n_tot=T, W=W),
        grid=(T // tm,),
        in_specs=[pl.BlockSpec((tm, PW), lambda i: (i, 0)),
                  pl.BlockSpec((8, PW), lambda i: (jnp.maximum(i * (tm // 8) - 1, 0), 0)),
                  pl.BlockSpec((8, PW), lambda i: (jnp.minimum((i + 1) * (tm // 8), nb8 - 1), 0))]
                 + [full(a) for a in args],
        out_specs=[row] * 11,
        out_shape=[jax.ShapeDtypeStruct((T, W), F32)] * 11,
        compiler_params=_cp(("arbitrary",)),
        name="rwkv_prep",
    )(P, P, P, *args)


def _rwkv_scan_kernel(rf, vf, kkf, lwf, kf, bf, rb, vb, kkb, lwb, kb, bb, yf_ref, yb_ref, s_scr, *, C, H, N):
    @pl.when(pl.program_id(0) == 0)
    def _():
        s_scr[...] = jnp.zeros_like(s_scr)

    row = lax.broadcasted_iota(jnp.int32, (C, C), 0)
    col = lax.broadcasted_iota(jnp.int32, (C, C), 1)
    eye = (row == col).astype(F32)
    n_sq = int(np.log2(C)) - 1
    dirs = ((rf, vf, kkf, lwf, kf, bf), (rb, vb, kkb, lwb, kb, bb))
    ch = []
    for d, (r_ref, v_ref, kk_ref, lw_ref, k_ref, b_ref) in enumerate(dirs):
        incl = (col <= row) if d == 0 else (col >= row)
        strict = (col < row) if d == 0 else (col > row)
        lw = lw_ref[...]
        cum = _dot(incl.astype(F32), lw, HI)
        e_pos = jnp.exp(cum)
        e_neg = jnp.exp(-cum)
        rt = (r_ref[...] * e_pos).astype(BF16)
        at = (-kk_ref[...] * jnp.exp(cum - lw)).astype(BF16)
        bt = (b_ref[...] * e_neg).astype(BF16)
        kt = (k_ref[...] * e_neg).astype(BF16)
        vv = v_ref[...].astype(BF16)
        gam = e_pos[C - 1:C, :] if d == 0 else e_pos[0:1, :]
        for h in range(H):
            sl = slice(h * N, (h + 1) * N)
            ch.append(dict(d=d, h=h, incl=incl, strict=strict, at=at[:, sl], rt=rt[:, sl], bt=bt[:, sl],
                           kt=kt[:, sl], v=vv[:, sl], gam=gam[:, sl], s0=s_scr[d, h]))
    for c in ch:
        a_all = _dot_nt(jnp.concatenate([c["at"], c["rt"]], axis=0), jnp.concatenate([c["bt"], c["kt"]], axis=0))
        c["l_pow"] = jnp.where(c["strict"], a_all[:C, :C], 0.0)
        c["l_ak"] = jnp.where(c["strict"], a_all[:C, C:], 0.0).astype(BF16)
        c["m_rb"] = jnp.where(c["incl"], a_all[C:, :C], 0.0).astype(BF16)
        c["m_rk"] = jnp.where(c["incl"], a_all[C:, C:], 0.0).astype(BF16)
        c["t_inv"] = eye + c["l_pow"]
    for _ in range(n_sq):
        for c in ch:
            lb = c["l_pow"].astype(BF16)
            c["l_pow"] = _dot(lb, lb)
        for c in ch:
            c["t_inv"] = c["t_inv"] + _dot(c["t_inv"].astype(BF16), c["l_pow"].astype(BF16))
    for c in ch:
        c["s0b"] = c["s0"].astype(BF16)
        c["w1"] = _dot_nt(c["at"], c["s0b"]) + _dot(c["l_ak"], c["v"])
    for c in ch:
        c["u"] = _dot(c["t_inv"].astype(BF16), c["w1"].astype(BF16)).astype(BF16)
    for c in ch:
        s_scr[c["d"], c["h"]] = (c["s0"] + _dot_tn(c["u"], c["bt"]) + _dot_tn(c["v"], c["kt"])) * c["gam"]
    for c in ch:
        c["y"] = _dot_nt(c["rt"], c["s0b"]) + _dot(c["m_rb"], c["u"]) + _dot(c["m_rk"], c["v"])
    yf_ref[...] = jnp.concatenate([c["y"] for c in ch[:H]], axis=1)
    yb_ref[...] = jnp.concatenate([c["y"] for c in ch[H:]], axis=1)


def _rwkv_scan(r, v, kk, lwf, kf, bf, lwb, kb, bb, n_ctx):
    T, W = r.shape
    C = SCAN_CHUNK
    H = W // RWKV_HEAD
    nch = T // C
    cch = n_ctx // C
    fwd = pl.BlockSpec((C, W), lambda n: (n, 0))
    bwd = pl.BlockSpec((C, W), lambda n: (jnp.where(n < cch, cch - 1 - n, nch - 1 + cch - n), 0))
    return pl.pallas_call(
        functools.partial(_rwkv_scan_kernel, C=C, H=H, N=RWKV_HEAD),
        grid=(nch,),
        in_specs=[fwd] * 6 + [bwd] * 6,
        out_specs=[fwd, bwd],
        out_shape=[jax.ShapeDtypeStruct((T, W), F32)] * 2,
        scratch_shapes=[pltpu.VMEM((2, H, RWKV_HEAD, RWKV_HEAD), F32)],
        compiler_params=_cp(("arbitrary",)),
        name="rwkv_scan",
    )(r, v, kk, lwf, kf, bf, r, v, kk, lwb, kb, bb)


def _rwkv_post_kernel(yf_ref, yb_ref, bonus_ref, gate_ref, lng_ref, lnb_ref, gh_ref, o_ref):
    gh = gh_ref[...]
    y = yf_ref[...] + yb_ref[...]
    mean = _seg_sum(y, gh) * (1.0 / RWKV_HEAD)
    yc = y - mean
    var = _seg_sum(yc * yc, gh) * (1.0 / RWKV_HEAD)
    yn = yc * lax.rsqrt(var + RWKV_GN_EPS) * lng_ref[...] + lnb_ref[...]
    o_ref[...] = ((yn + bonus_ref[...]) * gate_ref[...]).astype(o_ref.dtype)


def _rwkv_post(yf, yb, bonus, gate, lng, lnb, gh, tm):
    T, W = yf.shape
    row = pl.BlockSpec((tm, W), lambda i: (i, 0))
    full = lambda a: pl.BlockSpec(a.shape, lambda i: (0,) * a.ndim)
    return pl.pallas_call(
        _rwkv_post_kernel,
        grid=(T // tm,),
        in_specs=[row] * 4 + [full(lng), full(lnb), full(gh)],
        out_specs=row,
        out_shape=jax.ShapeDtypeStruct((T, W), BF16),
        compiler_params=_cp(("arbitrary",)),
        name="rwkv_post",
    )(yf, yb, bonus, gate, lng, lnb, gh)


def _gmlp_kernel(p_ref, g_ref, ws_ref, b_ref, o_ref, *, tm, W):
    u = jax.nn.gelu(p_ref[:, 0:W])
    v = jax.nn.gelu(p_ref[:, W:2 * W])
    v = v * lax.rsqrt(jnp.mean(v * v, axis=-1, keepdims=True) + NORM_EPS) * g_ref[...]
    vb = v.astype(BF16)
    gw = W // GM_GROUPS
    for c in range(tm // GM_CHUNK):
        rows = slice(c * GM_CHUNK, (c + 1) * GM_CHUNK)
        parts = [_dot(ws_ref[g], vb[rows, g * gw:(g + 1) * gw]) for g in range(GM_GROUPS)]
        s = jnp.concatenate(parts, axis=1) + b_ref[...]
        o_ref[rows, :] = (u[rows, :] * s).astype(o_ref.dtype)


def _gmlp(P, col_block, g, ws, bias, tm):
    T = P.shape[0]
    W = g.shape[1]
    return pl.pallas_call(
        functools.partial(_gmlp_kernel, tm=tm, W=W),
        grid=(T // tm,),
        in_specs=[pl.BlockSpec((tm, 2 * W), lambda i: (i, col_block)),
                  pl.BlockSpec((1, W), lambda i: (0, 0)),
                  pl.BlockSpec(ws.shape, lambda i: (0, 0, 0)),
                  pl.BlockSpec(bias.shape, lambda i: (0, 0))],
        out_specs=pl.BlockSpec((tm, W), lambda i: (i, 0)),
        out_shape=jax.ShapeDtypeStruct((T, W), BF16),
        compiler_params=_cp(("arbitrary",)),
        name="gmlp",
    )(P, g, ws, bias)


def _rope(x, cos, sin, lane):
    w = x.shape[1]
    partner = jnp.where((lane % 32) < 16, pltpu.roll(x, w - 16, axis=1), pltpu.roll(x, 16, axis=1))
    return x * cos + partner * sin


def _qk_prep_kernel(q_ref, k_ref, cos_ref, sin_ref, qg_ref, kg_ref, ghq_ref, ghk_ref, qo_ref, ko_ref, *, scale):
    cos = cos_ref[...]
    sin = sin_ref[...]
    for x_ref, g_ref, gh_ref, o_ref, mul in ((q_ref, qg_ref, ghq_ref, qo_ref, scale), (k_ref, kg_ref, ghk_ref, ko_ref, 1.0)):
        x = x_ref[...]
        w = x.shape[1]
        ss = _seg_sum(x * x, gh_ref[...]) * (1.0 / ATTN_HEAD)
        xn = x * lax.rsqrt(ss + NORM_EPS) * g_ref[...]
        rep = w // cos.shape[1]
        lane = lax.broadcasted_iota(jnp.int32, x.shape, 1)
        xr = _rope(xn, jnp.tile(cos, (1, rep)), jnp.tile(sin, (1, rep)), lane)
        o_ref[...] = (xr * mul).astype(o_ref.dtype)


def _qk_prep(P, q_block, k_block, cos, sin, qg, kg, ghq, ghk, tm):
    T = P.shape[0]
    QW, KW = qg.shape[1], kg.shape[1]
    full = lambda a: pl.BlockSpec(a.shape, lambda i: (0,) * a.ndim)
    return pl.pallas_call(
        functools.partial(_qk_prep_kernel, scale=ATTN_HEAD ** -0.5),
        grid=(T // tm,),
        in_specs=[pl.BlockSpec((tm, QW), lambda i: (i, q_block)),
                  pl.BlockSpec((tm, KW), lambda i: (i, k_block)),
                  pl.BlockSpec((tm, cos.shape[1]), lambda i: (i, 0)),
                  pl.BlockSpec((tm, sin.shape[1]), lambda i: (i, 0)),
                  full(qg), full(kg), full(ghq), full(ghk)],
        out_specs=[pl.BlockSpec((tm, QW), lambda i: (i, 0)), pl.BlockSpec((tm, KW), lambda i: (i, 0))],
        out_shape=[jax.ShapeDtypeStruct((T, QW), BF16), jax.ShapeDtypeStruct((T, KW), BF16)],
        compiler_params=_cp(("arbitrary",)),
        name="qk_prep",
    )(P, P, cos, sin, qg, kg, ghq, ghk)


def _attn_block(i, nb, sink_ref, q_ref, k_refs, v_refs, o_ref, local):
    L = ATTN_BLOCK
    G = ATTN_GROUP
    hd = ATTN_HEAD
    n_kv = k_refs[-1].shape[1] // hd
    R = G * L
    srow = lax.broadcasted_iota(jnp.int32, (R, 1), 0)
    if local:
        qi = lax.broadcasted_iota(jnp.int32, (R, 3 * L), 0) % L
        kj = lax.broadcasted_iota(jnp.int32, (R, 3 * L), 1)
        rel = kj - L - qi
        valid = (rel <= ATTN_WINDOW) & (rel >= -ATTN_WINDOW)
        valid = valid & ((kj >= L) | (i > 0)) & ((kj < 2 * L) | (i < nb - 1))
    for j in range(n_kv):
        ks = [r[:, j * hd:(j + 1) * hd] for r in k_refs]
        vs = [r[:, j * hd:(j + 1) * hd].astype(BF16) for r in v_refs]
        q = jnp.concatenate([q_ref[:, (j * G + g) * hd:(j * G + g + 1) * hd] for g in range(G)], axis=0)
        sink = jnp.zeros((R, 1), F32)
        for g in range(G):
            sink = jnp.where((srow >= g * L) & (srow < (g + 1) * L), sink_ref[j * G + g], sink)
        s_ctx = _dot_nt(q, ks[-1])
        m = jnp.maximum(jnp.max(s_ctx, axis=1, keepdims=True), sink)
        if local:
            s_loc = _dot_nt(q, jnp.concatenate(ks[:3], axis=0))
            s_loc = jnp.where(valid, s_loc, -1e30)
            m = jnp.maximum(m, jnp.max(s_loc, axis=1, keepdims=True))
            p_loc = jnp.exp(s_loc - m)
        p_ctx = jnp.exp(s_ctx - m)
        den = jnp.sum(p_ctx, axis=1, keepdims=True) + jnp.exp(sink - m)
        acc = _dot(p_ctx.astype(BF16), vs[-1])
        if local:
            den = den + jnp.sum(p_loc, axis=1, keepdims=True)
            acc = acc + _dot(p_loc.astype(BF16), jnp.concatenate(vs[:3], axis=0))
        out = acc / den
        for g in range(G):
            h = j * G + g
            o_ref[:, h * hd:(h + 1) * hd] = out[g * L:(g + 1) * L, :].astype(o_ref.dtype)


def _attn_kernel(sink_ref, q_ref, kp_ref, kc_ref, kn_ref, kx_ref, vp_ref, vc_ref, vn_ref, vx_ref, o_ref, *, cb, nb):
    i = pl.program_id(0)

    @pl.when(i < cb)
    def _():
        _attn_block(i, nb, sink_ref, q_ref, (kx_ref,), (vx_ref,), o_ref, False)

    @pl.when(i >= cb)
    def _():
        _attn_block(i - cb, nb, sink_ref, q_ref, (kp_ref, kc_ref, kn_ref, kx_ref),
                    (vp_ref, vc_ref, vn_ref, vx_ref), o_ref, True)


def _attention(qr, kr, P, v_block, sink, n_ctx):
    T, QW = qr.shape
    KW = kr.shape[1]
    L = ATTN_BLOCK
    cb = n_ctx // L
    nb = (T - n_ctx) // L
    lo, hi = cb, cb + nb - 1
    shifts = (lambda i: jnp.clip(i - 1, lo, hi), lambda i: jnp.clip(i, lo, hi), lambda i: jnp.clip(i + 1, lo, hi))
    kspec = lambda f: pl.BlockSpec((L, KW), lambda i: (f(i), 0))
    vspec = lambda f: pl.BlockSpec((L, KW), lambda i: (f(i), v_block))
    in_specs = ([pl.BlockSpec(memory_space=pltpu.SMEM), pl.BlockSpec((L, QW), lambda i: (i, 0))]
                + [kspec(f) for f in shifts] + [pl.BlockSpec((n_ctx, KW), lambda i: (0, 0))]
                + [vspec(f) for f in shifts] + [pl.BlockSpec((n_ctx, KW), lambda i: (0, v_block))])
    return pl.pallas_call(
        functools.partial(_attn_kernel, cb=cb, nb=nb),
        grid=(cb + nb,),
        in_specs=in_specs,
        out_specs=pl.BlockSpec((L, QW), lambda i: (i, 0)),
        out_shape=jax.ShapeDtypeStruct((T, QW), BF16),
        compiler_params=_cp(("arbitrary",)),
        name="attn",
    )(sink, qr, kr, kr, kr, kr, P, P, P, P)


def _oproj_kernel(x_ref, gt_ref, a_ref, b_ref, c_ref, wa_ref, wb_ref, wc_ref, o_ref, *, tm, n_ctx):
    acc = _dot(a_ref[...], wa_ref[...]) + _dot(b_ref[...], wb_ref[...]) + _dot(c_ref[...], wc_ref[...])
    gate = _pick_mod(gt_ref, _ctx_rows(pl.program_id(0), tm, n_ctx))
    o_ref[...] = x_ref[...] + gate * acc


def _oproj(x, mods, l, y_rw, y_gm, y_at, w_o, n_ctx, tm):
    T, D = x.shape
    tn = 512
    W1 = y_rw.shape[1]
    W3 = y_at.shape[1]
    gate_col = 2 * (D // tn)
    return pl.pallas_call(
        functools.partial(_oproj_kernel, tm=tm, n_ctx=n_ctx),
        grid=(T // tm, D // tn),
        in_specs=[pl.BlockSpec((tm, tn), lambda i, j: (i, j)),
                  pl.BlockSpec((1, 8, tn), lambda i, j: (l, 0, gate_col + j)),
                  pl.BlockSpec((tm, W1), lambda i, j: (i, 0)),
                  pl.BlockSpec((tm, W1), lambda i, j: (i, 0)),
                  pl.BlockSpec((tm, W3), lambda i, j: (i, 0)),
                  pl.BlockSpec((W1, tn), lambda i, j: (0, j)),
                  pl.BlockSpec((W1, tn), lambda i, j: (1, j)),
                  pl.BlockSpec((W3, tn), lambda i, j: (1, j))],
        out_specs=pl.BlockSpec((tm, tn), lambda i, j: (i, j)),
        out_shape=jax.ShapeDtypeStruct((T, D), F32),
        compiler_params=_cp(("arbitrary", "arbitrary")),
        name="oproj",
    )(x, mods, y_rw, y_gm, y_at, w_o, w_o, w_o)


def _router_kernel(x_ref, sh_ref, sc_ref, g_ref, rt_ref, bias_ref, tri_ref,
                   h_ref, hp_ref, idx_ref, wt_ref, rank_ref, cnt_ref, carry, *, tm, n_ctx):
    i = pl.program_id(0)

    @pl.when(i == 0)
    def _():
        carry[...] = jnp.zeros_like(carry)

    is_ctx = _ctx_rows(i, tm, n_ctx)
    h = _norm_mod(x_ref[...], g_ref[...], _pick_mod(sh_ref, is_ctx), _pick_mod(sc_ref, is_ctx))
    h_ref[...] = h.astype(h_ref.dtype)
    _store_slabs(hp_ref, _pack_halves(h), tm)
    E = N_EXPERTS
    pg = E // N_EXPERT_GROUPS
    neg = -jnp.inf
    scores = _sigmoid(_dot_nt(rt_ref[...], h, HI))
    biased = scores + bias_ref[:, 0:1]
    b3 = biased.reshape(N_EXPERT_GROUPS, pg, tm)
    i3 = lax.broadcasted_iota(jnp.int32, b3.shape, 1)
    m1 = jnp.max(b3, axis=1, keepdims=True)
    first = jnp.min(jnp.where(b3 == m1, i3, pg), axis=1, keepdims=True)
    m2 = jnp.max(jnp.where(i3 == first, neg, b3), axis=1, keepdims=True)
    gs = (m1 + m2).reshape(N_EXPERT_GROUPS, tm)
    gi = lax.broadcasted_iota(jnp.int32, gs.shape, 0)
    gsel = jnp.zeros(gs.shape, jnp.bool_)
    for _ in range(TOPK_GROUPS):
        gm = jnp.max(gs, axis=0, keepdims=True)
        gfirst = jnp.min(jnp.where(gs == gm, gi, N_EXPERT_GROUPS), axis=0, keepdims=True)
        hit = gi == gfirst
        gsel = gsel | hit
        gs = jnp.where(hit, neg, gs)
    masked = jnp.where(gsel.reshape(N_EXPERT_GROUPS, 1, tm), b3, neg).reshape(E, tm)
    ei = lax.broadcasted_iota(jnp.int32, (E, tm), 0)
    sel = jnp.zeros((E, tm), jnp.bool_)
    picks = []
    for _ in range(TOP_K):
        mx = jnp.max(masked, axis=0, keepdims=True)
        efirst = jnp.min(jnp.where(masked == mx, ei, E), axis=0, keepdims=True)
        hit = ei == efirst
        sel = sel | hit
        masked = jnp.where(hit, neg, masked)
        picks.append((efirst, hit, jnp.sum(jnp.where(hit, scores, 0.0), axis=0, keepdims=True)))
    wsum = picks[0][2]
    for pk in picks[1:]:
        wsum = wsum + pk[2]
    self_f = jnp.where(sel, 1.0, 0.0)
    rank_dense = carry[:, 0:1] + _dot(self_f.astype(BF16), tri_ref[...])
    carry[...] = carry[...] + jnp.sum(self_f, axis=1, keepdims=True)
    cnt_ref[...] = carry[...]
    for kx, (efirst, hit, wk) in enumerate(picks):
        idx_ref[kx:kx + 1, :] = efirst
        wt_ref[kx:kx + 1, :] = wk / wsum * ROUTED_SCALE
        rank_ref[kx:kx + 1, :] = jnp.sum(jnp.where(hit, rank_dense, 0.0), axis=0, keepdims=True).astype(jnp.int32)


def _router(x, mods, l, g, router_t, bias, tri, n_ctx, tm):
    T, D = x.shape
    E = router_t.shape[0]
    kspec = pl.BlockSpec((TOP_K, tm), lambda i: (0, i))
    return pl.pallas_call(
        functools.partial(_router_kernel, tm=tm, n_ctx=n_ctx),
        grid=(T // tm,),
        in_specs=[pl.BlockSpec((tm, D), lambda i: (i, 0)),
                  pl.BlockSpec((1, 8, D), lambda i: (l, 0, 3)),
                  pl.BlockSpec((1, 8, D), lambda i: (l, 0, 4)),
                  pl.BlockSpec((1, D), lambda i: (0, 0)),
                  pl.BlockSpec((E, D), lambda i: (0, 0)),
                  pl.BlockSpec((E, 128), lambda i: (0, 0)),
                  pl.BlockSpec((tm, tm), lambda i: (0, 0))],
        out_specs=[pl.BlockSpec((tm, D), lambda i: (i, 0)), pl.BlockSpec((tm * SLAB, 128), lambda i: (i, 0)),
                   kspec, kspec, kspec, pl.BlockSpec((E, 128), lambda i: (0, 0))],
        out_shape=[jax.ShapeDtypeStruct((T, D), BF16),
                   jax.ShapeDtypeStruct((T * SLAB, 128), jnp.uint32),
                   jax.ShapeDtypeStruct((TOP_K, T), jnp.int32),
                   jax.ShapeDtypeStruct((TOP_K, T), F32),
                   jax.ShapeDtypeStruct((TOP_K, T), jnp.int32),
                   jax.ShapeDtypeStruct((E, 128), F32)],
        scratch_shapes=[pltpu.VMEM((E, 128), F32)],
        compiler_params=_cp(("arbitrary",)),
        name="router",
    )(x, mods, mods, g, router_t, bias, tri)


def _gather_slabs(idx_ref, n, src_hbm, dst, sem):
    def body(r, carry):
        src = src_hbm.at[pl.ds(pl.multiple_of(idx_ref[r], SLAB), SLAB), :]
        pltpu.make_async_copy(src, dst.at[pl.ds(r * PITCH, SLAB), :], sem).start()
        return carry
    lax.fori_loop(0, n, body, 0, unroll=8)


def _wait_slabs(n, src_hbm, dst, sem):
    pltpu.make_async_copy(src_hbm.at[pl.ds(0, n * SLAB), :], dst.at[pl.ds(0, n * SLAB), :], sem).wait()


def _experts_kernel(be_ref, nu_ref, tokc_ref, tokn_ref, h_hbm, wg_ref, wu_ref, wd_ref, y_ref,
                    hbuf, sem, wg_s, wu_s, wd_s, *, TB, nblk):
    b = pl.program_id(0)
    slot = b % 2
    n_used = nu_ref[0]
    SW = h_hbm.shape[1]

    @pl.when(b == 0)
    def _():
        _gather_slabs(tokc_ref.at[0, 0], TB, h_hbm, hbuf.at[0], sem.at[0])

    @pl.when(b + 1 < n_used)
    def _():
        _gather_slabs(tokn_ref.at[0, 0], TB, h_hbm, hbuf.at[1 - slot], sem.at[1 - slot])

    new_expert = (b == 0) | (be_ref[b] != be_ref[jnp.maximum(b - 1, 0)])

    @pl.when(new_expert & (b < n_used))
    def _():
        wg_s[...] = wg_ref[0, 0].astype(BF16)
        wu_s[...] = wu_ref[0, 0].astype(BF16)
        wd_s[...] = wd_ref[0, 0].astype(BF16)

    @pl.when(b < n_used)
    def _():
        _wait_slabs(TB, h_hbm, hbuf.at[slot], sem.at[slot])
        half = SLAB * SW
        a = u = None
        for s in range(0, SLAB, 2):
            w0 = hbuf[slot, pl.ds(s, TB, stride=PITCH), :]
            w1 = hbuf[slot, pl.ds(s + 1, TB, stride=PITCH), :]
            for unpack, base in ((_unpack_lo, 0), (_unpack_hi, half)):
                xs = jnp.concatenate([unpack(w0), unpack(w1)], axis=1).astype(BF16)
                k0 = base + s * SW
                da = _dot(xs, wg_s[k0:k0 + 2 * SW, :])
                du = _dot(xs, wu_s[k0:k0 + 2 * SW, :])
                a = da if a is None else a + da
                u = du if u is None else u + du
        act = (a * _sigmoid(a) * u).astype(BF16)
        _store_slabs(y_ref, _pack_halves(_dot(act, wd_s[...])), TB)

    @pl.when(b >= n_used)
    def _():
        y_ref[...] = jnp.zeros_like(y_ref)


def _experts(h2s, tok_rows, block_e, n_used, wg, wu, wd, l):
    SW = h2s.shape[1]
    D = 2 * SW * SLAB
    nblk = block_e.shape[0]
    TB = EXPERT_TILE
    FF = wg.shape[3]
    tok3 = tok_rows.reshape(nblk, 1, TB)
    grid_spec = pltpu.PrefetchScalarGridSpec(
        num_scalar_prefetch=2,
        grid=(nblk,),
        in_specs=[pl.BlockSpec((1, 1, TB), lambda b, be, nu: (b, 0, 0), memory_space=pltpu.SMEM),
                  pl.BlockSpec((1, 1, TB), lambda b, be, nu: (jnp.minimum(b + 1, nblk - 1), 0, 0),
                               memory_space=pltpu.SMEM),
                  pl.BlockSpec(memory_space=pl.ANY),
                  pl.BlockSpec((1, 1, D, FF), lambda b, be, nu: (l, be[b], 0, 0)),
                  pl.BlockSpec((1, 1, D, FF), lambda b, be, nu: (l, be[b], 0, 0)),
                  pl.BlockSpec((1, 1, FF, D), lambda b, be, nu: (l, be[b], 0, 0))],
        out_specs=pl.BlockSpec((TB * SLAB, SW), lambda b, be, nu: (b, 0)),
        scratch_shapes=[pltpu.VMEM((2, TB * PITCH, SW), jnp.uint32), pltpu.SemaphoreType.DMA((2,)),
                        pltpu.VMEM((D, FF), BF16), pltpu.VMEM((D, FF), BF16), pltpu.VMEM((FF, D), BF16)],
    )
    return pl.pallas_call(
        functools.partial(_experts_kernel, TB=TB, nblk=nblk),
        grid_spec=grid_spec,
        out_shape=jax.ShapeDtypeStruct((nblk * TB * SLAB, SW), jnp.uint32),
        compiler_params=_cp(("arbitrary",)),
        name="experts",
    )(block_e, n_used, tok3, tok3, h2s, wg, wu, wd)


def _combine_kernel(slc_ref, sln_ref, y_hbm, x_ref, h_ref, wt_ref, gt_ref, swg_ref, swu_ref, swd_ref, o_ref,
                    ybuf, sem, *, tm, n_ctx, nt):
    i = pl.program_id(0)
    slot = i % 2
    SW = y_hbm.shape[1]

    @pl.when(i == 0)
    def _():
        for k in range(TOP_K):
            _gather_slabs(slc_ref.at[0, k], tm, y_hbm, ybuf.at[0, k], sem.at[0])

    @pl.when(i + 1 < nt)
    def _():
        for k in range(TOP_K):
            _gather_slabs(sln_ref.at[0, k], tm, y_hbm, ybuf.at[1 - slot, k], sem.at[1 - slot])

    hb = h_ref[...]
    a = _dot(hb, swg_ref[...])
    u = _dot(hb, swu_ref[...])
    shared = _dot((a * _sigmoid(a) * u).astype(BF16), swd_ref[...])
    gate = _pick_mod(gt_ref, _ctx_rows(i, tm, n_ctx))
    wk = [jnp.broadcast_to(wt_ref[:, k:k + 1], (tm, SW)) for k in range(TOP_K)]
    for k in range(TOP_K):
        _wait_slabs(tm, y_hbm, ybuf.at[slot, k], sem.at[slot])
    half = SLAB * SW
    for s in range(SLAB):
        lo = slice(s * SW, (s + 1) * SW)
        hi = slice(half + s * SW, half + (s + 1) * SW)
        acc_lo = shared[:, lo]
        acc_hi = shared[:, hi]
        for k in range(TOP_K):
            w = ybuf[slot, k, pl.ds(s, tm, stride=PITCH), :]
            acc_lo = acc_lo + wk[k] * _unpack_lo(w)
            acc_hi = acc_hi + wk[k] * _unpack_hi(w)
        o_ref[:, lo] = x_ref[:, lo] + gate[:, lo] * acc_lo
        o_ref[:, hi] = x_ref[:, hi] + gate[:, hi] * acc_hi


def _combine(y2, slot_rows, wt_t, x, h2, mods, l, swg, swu, swd, n_ctx, tm):
    T, D = x.shape
    SW = y2.shape[1]
    nt = T // tm
    FF = swg.shape[1]
    sl3 = slot_rows.reshape(TOP_K, nt, tm).transpose(1, 0, 2)
    return pl.pallas_call(
        functools.partial(_combine_kernel, tm=tm, n_ctx=n_ctx, nt=nt),
        grid=(nt,),
        in_specs=[pl.BlockSpec((1, TOP_K, tm), lambda i: (i, 0, 0), memory_space=pltpu.SMEM),
                  pl.BlockSpec((1, TOP_K, tm), lambda i: (jnp.minimum(i + 1, nt - 1), 0, 0),
                               memory_space=pltpu.SMEM),
                  pl.BlockSpec(memory_space=pl.ANY),
                  pl.BlockSpec((tm, D), lambda i: (i, 0)),
                  pl.BlockSpec((tm, D), lambda i: (i, 0)),
                  pl.BlockSpec((tm, TOP_K), lambda i: (i, 0)),
                  pl.BlockSpec((1, 8, D), lambda i: (l, 0, 5)),
                  pl.BlockSpec((D, FF), lambda i: (0, 0)),
                  pl.BlockSpec((D, FF), lambda i: (0, 0)),
                  pl.BlockSpec((FF, D), lambda i: (0, 0))],
        out_specs=pl.BlockSpec((tm, D), lambda i: (i, 0)),
        out_shape=jax.ShapeDtypeStruct((T, D), F32),
        scratch_shapes=[pltpu.VMEM((2, TOP_K, tm * PITCH, SW), jnp.uint32), pltpu.SemaphoreType.DMA((2,))],
        compiler_params=_cp(("arbitrary",)),
        name="combine",
    )(sl3, sl3, y2, x, h2, wt_t, mods, swg, swu, swd)


def _seg_indicator(width, seg):
    idx = np.arange(width) // seg
    return jnp.asarray(idx[:, None] == idx[None, :], dtype=BF16)


def _rope_tables(n_ctx, n_lat):
    rows = n_lat // GRID_W
    row = jnp.repeat(jnp.arange(rows, dtype=F32), GRID_W)
    col = jnp.tile(jnp.arange(GRID_W, dtype=F32), rows)
    n_freq = ATTN_HEAD // 4
    inv_freq = ROPE_THETA ** (-jnp.arange(n_freq, dtype=F32) / n_freq)
    ar, ac = row[:, None] * inv_freq, col[:, None] * inv_freq
    cos = jnp.concatenate([jnp.cos(ar), jnp.cos(ar), jnp.cos(ac), jnp.cos(ac)], axis=1)
    sin = jnp.concatenate([-jnp.sin(ar), jnp.sin(ar), -jnp.sin(ac), jnp.sin(ac)], axis=1)
    cos = jnp.concatenate([jnp.ones((n_ctx, ATTN_HEAD), F32), cos], axis=0)
    sin = jnp.concatenate([jnp.zeros((n_ctx, ATTN_HEAD), F32), sin], axis=0)
    return jnp.tile(cos, (1, 2)), jnp.tile(sin, (1, 2))


def kernel(x, c, ctx, c_ctx, w_ada, b_ada, norm1_g, norm2_g, w_in, w_o, rw_mu, rw_w0, rw_w2, rw_a0, rw_a2, rw_g2, rw_kk, rw_ka, rw_rk, rw_ln_g, rw_ln_b, gm_norm_g, gm_ws, gm_b, at_qn, at_kn, at_sink, moe_router, moe_bias, moe_wg, moe_wu, moe_wd, sh_wg, sh_wu, sh_wd):
    B, S, D = x.shape
    C = ctx.shape[1]
    assert B == 1
    L = w_ada.shape[0]
    T = C + S
    RW = rw_kk.shape[1]
    GW = gm_norm_g.shape[1]
    QW = at_sink.shape[1] * ATTN_HEAD
    KW = QW // ATTN_GROUP
    rw_proj = rw_mu.shape[2]
    RWP = 2048
    assert rw_proj <= RWP and T % 768 == 0 and C % 256 == 0 and S % 256 == 0 and D == 2 * SLAB * 128

    xs = jnp.concatenate([ctx[0], x[0]], axis=0)
    cond8 = jnp.zeros((8, D), F32).at[0].set(c_ctx).at[1].set(c[0])
    mods = _ada(cond8, w_ada, b_ada)

    gh_rw = _seg_indicator(RW, RWKV_HEAD)
    gh_q = _seg_indicator(QW, ATTN_HEAD)
    gh_k = _seg_indicator(KW, ATTN_HEAD)
    cos, sin = _rope_tables(C, S)
    tri = jnp.asarray(np.arange(256)[:, None] < np.arange(256)[None, :], dtype=BF16)
    TB = EXPERT_TILE
    nblk = -(-(T * TOP_K) // TB) + N_EXPERTS

    for l in range(L):
        w_in_l = w_in[l]
        w_in_p = jnp.concatenate([w_in_l[:, :rw_proj], jnp.zeros((D, RWP - rw_proj), F32), w_in_l[:, rw_proj:]],
                                 axis=1).astype(BF16)
        P = _inproj(xs, mods, l, norm1_g[l][None], w_in_p, C, 768)
        gm_block = RWP // (2 * GW)
        q_block = (RWP + 2 * GW) // QW
        k_block = (RWP + 2 * GW + QW) // KW
        v_block = k_block + 1

        pad = lambda a: jnp.pad(a, ((0, 0), (0, RWP - rw_proj)))
        prm = {"mu": pad(rw_mu[l]), "w0": rw_w0[l], "w2": rw_w2[l], "a0": rw_a0[l], "a2": rw_a2[l],
               "g2": rw_g2[l], "kk": rw_kk[l][None], "ka": rw_ka[l][None], "rk": rw_rk[l].reshape(1, RW)}
        r, v, kk, lwf, kf, bf, lwb, kb, bb, gate, bonus = _rwkv_prep(P, prm, gh_rw, C, 256)
        yf, yb = _rwkv_scan(r, v, kk, lwf, kf, bf, lwb, kb, bb, C)
        y_rw = _rwkv_post(yf, yb, bonus, gate, rw_ln_g[l][None], rw_ln_b[l][None], gh_rw, 256)

        gm_bias = jnp.repeat(gm_b[l].T, GW // GM_GROUPS, axis=1)
        y_gm = _gmlp(P, gm_block, gm_norm_g[l][None], gm_ws[l].astype(BF16), gm_bias, 256)

        qg = jnp.tile(at_qn[l], QW // ATTN_HEAD)[None]
        kg = jnp.tile(at_kn[l], KW // ATTN_HEAD)[None]
        qr, kr = _qk_prep(P, q_block, k_block, cos, sin, qg, kg, gh_q, gh_k, 256)
        y_at = _attention(qr, kr, P, v_block, at_sink[l], C)

        xs = _oproj(xs, mods, l, y_rw, y_gm, y_at, w_o[l].astype(BF16), C, 768)

        bias_col = jnp.broadcast_to(moe_bias[l][:, None], (N_EXPERTS, 128))
        h2, h2s, idx8, wt8, rank8, cnt = _router(xs, mods, l, norm2_g[l][None], moe_router[l].T, bias_col, tri, C, 256)
        counts = cnt[:, 0].astype(jnp.int32)
        padded = (counts + TB - 1) // TB * TB
        pad_end = jnp.cumsum(padded)
        pad_start = pad_end - padded
        e_ids = jnp.arange(N_EXPERTS, dtype=jnp.int32)
        slot8 = jnp.sum(jnp.where(idx8[:, :, None] == e_ids, pad_start, 0), axis=-1) + rank8
        tok = jnp.zeros((nblk * TB,), jnp.int32).at[slot8.reshape(-1)].set(
            jnp.tile(jnp.arange(T, dtype=jnp.int32), TOP_K), unique_indices=True)
        blk_pos = jnp.arange(nblk, dtype=jnp.int32) * TB
        block_e = jnp.minimum(jnp.sum((pad_end[None, :] <= blk_pos[:, None]).astype(jnp.int32), axis=1), N_EXPERTS - 1)
        n_used = (pad_end[-1] // TB).astype(jnp.int32).reshape(1)
        y2 = _experts(h2s, tok * SLAB, block_e, n_used, moe_wg, moe_wu, moe_wd, l)
        xs = _combine(y2, slot8 * SLAB, wt8.T, xs, h2, mods, l, sh_wg[l].astype(BF16), sh_wu[l].astype(BF16),
                      sh_wd[l].astype(BF16), C, 128)
    return xs[C:].reshape(B, S, D)
```

```python
import functools

import jax
import jax.numpy as jnp
import numpy as np
from jax import lax
from jax.experimental import pallas as pl
from jax.experimental.pallas import tpu as pltpu

F32 = jnp.float32
BF16 = jnp.bfloat16
HI = lax.Precision.HIGHEST

NORM_EPS = 1e-6
GRID_W = 64

RWKV_HEAD = 64
DECAY_LORA = 64
ICLR_LORA = 64
GATE_LORA = 128
RWKV_GN_EPS = 64e-5

GM_GROUPS = 8
GM_CHUNK = 128

ATTN_HEAD = 64
ATTN_GROUP = 4
ATTN_WINDOW = 128
ATTN_BLOCK = 128
ROPE_THETA = 10000.0

N_EXPERTS = 64
TOP_K = 8
N_EXPERT_GROUPS = 8
TOPK_GROUPS = 4
ROUTED_SCALE = 2.5

SCAN_CHUNK = 64
EXPERT_TILE = 256
SLAB = 8
PITCH = 12
VMEM_LIMIT = 56 * 1024 * 1024


def _cp(sem, vmem=VMEM_LIMIT):
    return pltpu.CompilerParams(dimension_semantics=sem, vmem_limit_bytes=vmem)


def _dot(a, b, prec=None):
    return jnp.dot(a, b, preferred_element_type=F32, precision=prec)


def _dot_nt(a, b, prec=None):
    return lax.dot_general(a, b, (((1,), (1,)), ((), ())), preferred_element_type=F32, precision=prec)


def _dot_tn(a, b, prec=None):
    return lax.dot_general(a, b, (((0,), (0,)), ((), ())), preferred_element_type=F32, precision=prec)


def _seg_sum(x, g):
    xh = x.astype(BF16)
    xl = (x - xh.astype(F32)).astype(BF16)
    return _dot(xh, g) + _dot(xl, g)


def _sigmoid(x):
    return jax.nn.sigmoid(x)


def _pack_halves(x):
    half = x.shape[1] // 2
    lo = lax.bitcast_convert_type(x[:, :half].astype(BF16).astype(F32), jnp.uint32)
    hi = lax.bitcast_convert_type(x[:, half:].astype(BF16).astype(F32), jnp.uint32)
    return hi | (lo >> 16)


def _unpack_lo(w):
    return lax.bitcast_convert_type(w << 16, F32)


def _unpack_hi(w):
    return lax.bitcast_convert_type(w & jnp.uint32(0xFFFF0000), F32)


def _store_slabs(ref, words, n):
    for c in range(SLAB):
        ref[pl.ds(c, n, stride=SLAB), :] = words[:, c * 128:(c + 1) * 128]


def _norm_mod(x, g, sh, sc):
    y = x * lax.rsqrt(jnp.mean(x * x, axis=-1, keepdims=True) + NORM_EPS)
    return (y * g) * (1.0 + sc) + sh


def _ctx_rows(i, tm, n_ctx):
    row = i * tm + lax.broadcasted_iota(jnp.int32, (tm, 1), 0)
    return row < n_ctx


def _pick_mod(mod_ref, is_ctx):
    return jnp.where(is_ctx, mod_ref[0, 0:1, :], mod_ref[0, 1:2, :])


def _ada_kernel(c_ref, w_ref, b_ref, o_ref):
    c = c_ref[...]
    s = c * _sigmoid(c)
    o_ref[0] = _dot(s, w_ref[0], HI) + b_ref[0]


def _ada(cond8, w_ada, b_ada):
    L, D, N = w_ada.shape
    tn = 1024
    return pl.pallas_call(
        _ada_kernel,
        grid=(L, N // tn),
        in_specs=[pl.BlockSpec((8, D), lambda l, j: (0, 0)),
                  pl.BlockSpec((1, D, tn), lambda l, j: (l, 0, j)),
                  pl.BlockSpec((1, 1, tn), lambda l, j: (l, 0, j))],
        out_specs=pl.BlockSpec((1, 8, tn), lambda l, j: (l, 0, j)),
        out_shape=jax.ShapeDtypeStruct((L, 8, N), F32),
        compiler_params=_cp(("arbitrary", "arbitrary")),
        name="ada",
    )(cond8, w_ada, b_ada.reshape(L, 1, N))


def _inproj_kernel(x_ref, sh_ref, sc_ref, g_ref, w_ref, o_ref, h_scr, *, tm, n_ctx):
    i = pl.program_id(0)

    @pl.when(pl.program_id(1) == 0)
    def _():
        is_ctx = _ctx_rows(i, tm, n_ctx)
        h = _norm_mod(x_ref[...], g_ref[...], _pick_mod(sh_ref, is_ctx), _pick_mod(sc_ref, is_ctx))
        h_scr[...] = h.astype(BF16)

    o_ref[...] = _dot(h_scr[...], w_ref[...])


def _inproj(x, mods, l, g, w, n_ctx, tm):
    T, D = x.shape
    N = w.shape[1]
    tn = 512
    return pl.pallas_call(
        functools.partial(_inproj_kernel, tm=tm, n_ctx=n_ctx),
        grid=(T // tm, N // tn),
        in_specs=[pl.BlockSpec((tm, D), lambda i, j: (i, 0)),
                  pl.BlockSpec((1, 8, D), lambda i, j: (l, 0, 0)),
                  pl.BlockSpec((1, 8, D), lambda i, j: (l, 0, 1)),
                  pl.BlockSpec((1, D), lambda i, j: (0, 0)),
                  pl.BlockSpec((D, tn), lambda i, j: (0, j))],
        out_specs=pl.BlockSpec((tm, tn), lambda i, j: (i, j)),
        out_shape=jax.ShapeDtypeStruct((T, N), F32),
        scratch_shapes=[pltpu.VMEM((tm, D), BF16)],
        compiler_params=_cp(("arbitrary", "arbitrary")),
        name="inproj",
    )(x, mods, mods, g, w)


def _softplus(x):
    return jnp.maximum(x, 0.0) + jnp.log(1.0 + jnp.exp(-jnp.abs(x)))


def _rwkv_prep_kernel(p_ref, pp_ref, pn_ref, mu_ref, w0_ref, w2_ref, a0_ref, a2_ref, g2_ref,
                      kkp_ref, kap_ref, rkp_ref, gh_ref,
                      r_ref, v_ref, kk_ref, lwf_ref, kf_ref, bf_ref, lwb_ref, kb_ref, bb_ref,
                      gate_ref, bonus_ref, *, tm, n_ctx, n_tot, W):
    i = pl.program_id(0)
    p = p_ref[...]
    lrow = lax.broadcasted_iota(jnp.int32, (tm, 1), 0)
    grow = i * tm + lrow
    prev = jnp.where(lrow == 0, pp_ref[7:8, :], pltpu.roll(p, 1, axis=0))
    prev = jnp.where((grow == 0) | (grow == n_ctx), 0.0, prev)
    nxt = jnp.where(lrow == tm - 1, pn_ref[0:1, :], pltpu.roll(p, tm - 1, axis=0))
    nxt = jnp.where((grow == n_ctx - 1) | (grow == n_tot - 1), 0.0, nxt)
    ps = p + mu_ref[0:1, :] * (prev - p) + mu_ref[1:2, :] * (nxt - p)

    r = ps[:, 0:W]
    k = ps[:, W:2 * W]
    v = ps[:, 2 * W:3 * W]
    o = 3 * W
    wd = (ps[:, o:o + DECAY_LORA], ps[:, o + DECAY_LORA:o + 2 * DECAY_LORA])
    o += 2 * DECAY_LORA
    ad = (ps[:, o:o + ICLR_LORA], ps[:, o + ICLR_LORA:o + 2 * ICLR_LORA])
    o += 2 * ICLR_LORA
    gd = ps[:, o:o + GATE_LORA]

    gh = gh_ref[...]
    kk = k * kkp_ref[...]
    kk = kk / jnp.maximum(jnp.sqrt(_seg_sum(kk * kk, gh)), 1e-12)
    r_ref[...] = r
    v_ref[...] = v
    kk_ref[...] = kk
    outs = ((lwf_ref, kf_ref, bf_ref), (lwb_ref, kb_ref, bb_ref))
    for d in range(2):
        z = w0_ref[d:d + 1, :] + _dot(jnp.tanh(wd[d]), w2_ref[d], HI)
        w_log = -_softplus(-z) - 0.5
        a = _sigmoid(a0_ref[d:d + 1, :] + _dot(ad[d], a2_ref[d], HI))
        lw_ref, kd_ref, bd_ref = outs[d]
        lw_ref[...] = -jnp.exp(w_log)
        kd_ref[...] = k * (1.0 + (a - 1.0) * kap_ref[...])
        bd_ref[...] = kk * a
    gate_ref[...] = _dot(_sigmoid(gd), g2_ref[...], HI)
    bonus_ref[...] = _seg_sum(r * k * rkp_ref[...], gh) * v


def _rwkv_prep(P, prm, gh, n_ctx, tm):
    T = P.shape[0]
    W = prm["kk"].shape[1]
    PW = 2048
    nb8 = T // 8
    row = pl.BlockSpec((tm, W), lambda i: (i, 0))
    full = lambda a: pl.BlockSpec(a.shape, lambda i: (0,) * a.ndim)
    args = (prm["mu"], prm["w0"], prm["w2"], prm["a0"], prm["a2"], prm["g2"], prm["kk"], prm["ka"], prm["rk"], gh)
    return pl.pallas_call(
        functools.partial(_rwkv_prep_kernel, tm=tm, n_ctx=n_ctx, n_tot=T, W=W),
        grid=(T // tm,),
        in_specs=[pl.BlockSpec((tm, PW), lambda i: (i, 0)),
                  pl.BlockSpec((8, PW), lambda i: (jnp.maximum(i * (tm // 8) - 1, 0), 0)),
                  pl.BlockSpec((8, PW), lambda i: (jnp.minimum((i + 1) * (tm // 8), nb8 - 1), 0))]
                 + [full(a) for a in args],
        out_specs=[row] * 11,
        out_shape=[jax.ShapeDtypeStruct((T, W), F32)] * 11,
        compiler_params=_cp(("arbitrary",)),
        name="rwkv_prep",
    )(P, P, P, *args)


def _rwkv_scan_kernel(rf, vf, kkf, lwf, kf, bf, rb, vb, kkb, lwb, kb, bb, yf_ref, yb_ref, s_scr, *, C, H, N):
    @pl.when(pl.program_id(0) == 0)
    def _():
        s_scr[...] = jnp.zeros_like(s_scr)

    row = lax.broadcasted_iota(jnp.int32, (C, C), 0)
    col = lax.broadcasted_iota(jnp.int32, (C, C), 1)
    eye = (row == col).astype(F32)
    n_sq = int(np.log2(C)) - 1
    dirs = ((rf, vf, kkf, lwf, kf, bf), (rb, vb, kkb, lwb, kb, bb))
    ch = []
    for d, (r_ref, v_ref, kk_ref, lw_ref, k_ref, b_ref) in enumerate(dirs):
        incl = (col <= row) if d == 0 else (col >= row)
        strict = (col < row) if d == 0 else (col > row)
        lw = lw_ref[...]
        cum = _dot(incl.astype(F32), lw, HI)
        e_pos = jnp.exp(cum)
        e_neg = jnp.exp(-cum)
        rt = (r_ref[...] * e_pos).astype(BF16)
        at = (-kk_ref[...] * jnp.exp(cum - lw)).astype(BF16)
        bt = (b_ref[...] * e_neg).astype(BF16)
        kt = (k_ref[...] * e_neg).astype(BF16)
        vv = v_ref[...].astype(BF16)
        gam = e_pos[C - 1:C, :] if d == 0 else e_pos[0:1, :]
        for h in range(H):
            sl = slice(h * N, (h + 1) * N)
            ch.append(dict(d=d, h=h, incl=incl, strict=strict, at=at[:, sl], rt=rt[:, sl], bt=bt[:, sl],
                           kt=kt[:, sl], v=vv[:, sl], gam=gam[:, sl], s0=s_scr[d, h]))
    for c in ch:
        a_all = _dot_nt(jnp.concatenate([c["at"], c["rt"]], axis=0), jnp.concatenate([c["bt"], c["kt"]], axis=0))
        c["l_pow"] = jnp.where(c["strict"], a_all[:C, :C], 0.0)
        c["l_ak"] = jnp.where(c["strict"], a_all[:C, C:], 0.0).astype(BF16)
        c["m_rb"] = jnp.where(c["incl"], a_all[C:, :C], 0.0).astype(BF16)
        c["m_rk"] = jnp.where(c["incl"], a_all[C:, C:], 0.0).astype(BF16)
        c["t_inv"] = eye + c["l_pow"]
    for _ in range(n_sq):
        for c in ch:
            lb = c["l_pow"].astype(BF16)
            c["l_pow"] = _dot(lb, lb)
        for c in ch:
            c["t_inv"] = c["t_inv"] + _dot(c["t_inv"].astype(BF16), c["l_pow"].astype(BF16))
    for c in ch:
        c["s0b"] = c["s0"].astype(BF16)
        c["w1"] = _dot_nt(c["at"], c["s0b"]) + _dot(c["l_ak"], c["v"])
    for c in ch:
        c["u"] = _dot(c["t_inv"].astype(BF16), c["w1"].astype(BF16)).astype(BF16)
    for c in ch:
        s_scr[c["d"], c["h"]] = (c["s0"] + _dot_tn(c["u"], c["bt"]) + _dot_tn(c["v"], c["kt"])) * c["gam"]
    for c in ch:
        c["y"] = _dot_nt(c["rt"], c["s0b"]) + _dot(c["m_rb"], c["u"]) + _dot(c["m_rk"], c["v"])
    yf_ref[...] = jnp.concatenate([c["y"] for c in ch[:H]], axis=1)
    yb_ref[...] = jnp.concatenate([c["y"] for c in ch[H:]], axis=1)


def _rwkv_scan(r, v, kk, lwf, kf, bf, lwb, kb, bb, n_ctx):
    T, W = r.shape
    C = SCAN_CHUNK
    H = W // RWKV_HEAD
    nch = T // C
    cch = n_ctx // C
    fwd = pl.BlockSpec((C, W), lambda n: (n, 0))
    bwd = pl.BlockSpec((C, W), lambda n: (jnp.where(n < cch, cch - 1 - n, nch - 1 + cch - n), 0))
    return pl.pallas_call(
        functools.partial(_rwkv_scan_kernel, C=C, H=H, N=RWKV_HEAD),
        grid=(nch,),
        in_specs=[fwd] * 6 + [bwd] * 6,
        out_specs=[fwd, bwd],
        out_shape=[jax.ShapeDtypeStruct((T, W), F32)] * 2,
        scratch_shapes=[pltpu.VMEM((2, H, RWKV_HEAD, RWKV_HEAD), F32)],
        compiler_params=_cp(("arbitrary",)),
        name="rwkv_scan",
    )(r, v, kk, lwf, kf, bf, r, v, kk, lwb, kb, bb)


def _rwkv_post_kernel(yf_ref, yb_ref, bonus_ref, gate_ref, lng_ref, lnb_ref, gh_ref, o_ref):
    gh = gh_ref[...]
    y = yf_ref[...] + yb_ref[...]
    mean = _seg_sum(y, gh) * (1.0 / RWKV_HEAD)
    yc = y - mean
    var = _seg_sum(yc * yc, gh) * (1.0 / RWKV_HEAD)
    yn = yc * lax.rsqrt(var + RWKV_GN_EPS) * lng_ref[...] + lnb_ref[...]
    o_ref[...] = ((yn + bonus_ref[...]) * gate_ref[...]).astype(o_ref.dtype)


def _rwkv_post(yf, yb, bonus, gate, lng, lnb, gh, tm):
    T, W = yf.shape
    row = pl.BlockSpec((tm, W), lambda i: (i, 0))
    full = lambda a: pl.BlockSpec(a.shape, lambda i: (0,) * a.ndim)
    return pl.pallas_call(
        _rwkv_post_kernel,
        grid=(T // tm,),
        in_specs=[row] * 4 + [full(lng), full(lnb), full(gh)],
        out_specs=row,
        out_shape=jax.ShapeDtypeStruct((T, W), BF16),
        compiler_params=_cp(("arbitrary",)),
        name="rwkv_post",
    )(yf, yb, bonus, gate, lng, lnb, gh)


def _gmlp_kernel(p_ref, g_ref, ws_ref, b_ref, o_ref, *, tm, W):
    u = jax.nn.gelu(p_ref[:, 0:W])
    v = jax.nn.gelu(p_ref[:, W:2 * W])
    v = v * lax.rsqrt(jnp.mean(v * v, axis=-1, keepdims=True) + NORM_EPS) * g_ref[...]
    vb = v.astype(BF16)
    gw = W // GM_GROUPS
    for c in range(tm // GM_CHUNK):
        rows = slice(c * GM_CHUNK, (c + 1) * GM_CHUNK)
        parts = [_dot(ws_ref[g], vb[rows, g * gw:(g + 1) * gw]) for g in range(GM_GROUPS)]
        s = jnp.concatenate(parts, axis=1) + b_ref[...]
        o_ref[rows, :] = (u[rows, :] * s).astype(o_ref.dtype)


def _gmlp(P, col_block, g, ws, bias, tm):
    T = P.shape[0]
    W = g.shape[1]
    return pl.pallas_call(
        functools.partial(_gmlp_kernel, tm=tm, W=W),
        grid=(T // tm,),
        in_specs=[pl.BlockSpec((tm, 2 * W), lambda i: (i, col_block)),
                  pl.BlockSpec((1, W), lambda i: (0, 0)),
                  pl.BlockSpec(ws.shape, lambda i: (0, 0, 0)),
                  pl.BlockSpec(bias.shape, lambda i: (0, 0))],
        out_specs=pl.BlockSpec((tm, W), lambda i: (i, 0)),
        out_shape=jax.ShapeDtypeStruct((T, W), BF16),
        compiler_params=_cp(("arbitrary",)),
        name="gmlp",
    )(P, g, ws, bias)


def _rope(x, cos, sin, lane):
    w = x.shape[1]
    partner = jnp.where((lane % 32) < 16, pltpu.roll(x, w - 16, axis=1), pltpu.roll(x, 16, axis=1))
    return x * cos + partner * sin


def _qk_prep_kernel(q_ref, k_ref, cos_ref, sin_ref, qg_ref, kg_ref, ghq_ref, ghk_ref, qo_ref, ko_ref, *, scale):
    cos = cos_ref[...]
    sin = sin_ref[...]
    for x_ref, g_ref, gh_ref, o_ref, mul in ((q_ref, qg_ref, ghq_ref, qo_ref, scale), (k_ref, kg_ref, ghk_ref, ko_ref, 1.0)):
        x = x_ref[...]
        w = x.shape[1]
        ss = _seg_sum(x * x, gh_ref[...]) * (1.0 / ATTN_HEAD)
        xn = x * lax.rsqrt(ss + NORM_EPS) * g_ref[...]
        rep = w // cos.shape[1]
        lane = lax.broadcasted_iota(jnp.int32, x.shape, 1)
        xr = _rope(xn, jnp.tile(cos, (1, rep)), jnp.tile(sin, (1, rep)), lane)
        o_ref[...] = (xr * mul).astype(o_ref.dtype)


def _qk_prep(P, q_block, k_block, cos, sin, qg, kg, ghq, ghk, tm):
    T = P.shape[0]
    QW, KW = qg.shape[1], kg.shape[1]
    full = lambda a: pl.BlockSpec(a.shape, lambda i: (0,) * a.ndim)
    return pl.pallas_call(
        functools.partial(_qk_prep_kernel, scale=ATTN_HEAD ** -0.5),
        grid=(T // tm,),
        in_specs=[pl.BlockSpec((tm, QW), lambda i: (i, q_block)),
                  pl.BlockSpec((tm, KW), lambda i: (i, k_block)),
                  pl.BlockSpec((tm, cos.shape[1]), lambda i: (i, 0)),
                  pl.BlockSpec((tm, sin.shape[1]), lambda i: (i, 0)),
                  full(qg), full(kg), full(ghq), full(ghk)],
        out_specs=[pl.BlockSpec((tm, QW), lambda i: (i, 0)), pl.BlockSpec((tm, KW), lambda i: (i, 0))],
        out_shape=[jax.ShapeDtypeStruct((T, QW), BF16), jax.ShapeDtypeStruct((T, KW), BF16)],
        compiler_params=_cp(("arbitrary",)),
        name="qk_prep",
    )(P, P, cos, sin, qg, kg, ghq, ghk)


def _attn_block(i, nb, sink_ref, q_ref, k_refs, v_refs, o_ref, local):
    L = ATTN_BLOCK
    G = ATTN_GROUP
    hd = ATTN_HEAD
    n_kv = k_refs[-1].shape[1] // hd
    R = G * L
    srow = lax.broadcasted_iota(jnp.int32, (R, 1), 0)
    if local:
        qi = lax.broadcasted_iota(jnp.int32, (R, 3 * L), 0) % L
        kj = lax.broadcasted_iota(jnp.int32, (R, 3 * L), 1)
        rel = kj - L - qi
        valid = (rel <= ATTN_WINDOW) & (rel >= -ATTN_WINDOW)
        valid = valid & ((kj >= L) | (i > 0)) & ((kj < 2 * L) | (i < nb - 1))
    for j in range(n_kv):
        ks = [r[:, j * hd:(j + 1) * hd] for r in k_refs]
        vs = [r[:, j * hd:(j + 1) * hd].astype(BF16) for r in v_refs]
        q = jnp.concatenate([q_ref[:, (j * G + g) * hd:(j * G + g + 1) * hd] for g in range(G)], axis=0)
        sink = jnp.zeros((R, 1), F32)
        for g in range(G):
            sink = jnp.where((srow >= g * L) & (srow < (g + 1) * L), sink_ref[j * G + g], sink)
        s_ctx = _dot_nt(q, ks[-1])
        m = jnp.maximum(jnp.max(s_ctx, axis=1, keepdims=True), sink)
        if local:
            s_loc = _dot_nt(q, jnp.concatenate(ks[:3], axis=0))
            s_loc = jnp.where(valid, s_loc, -1e30)
            m = jnp.maximum(m, jnp.max(s_loc, axis=1, keepdims=True))
            p_loc = jnp.exp(s_loc - m)
        p_ctx = jnp.exp(s_ctx - m)
        den = jnp.sum(p_ctx, axis=1, keepdims=True) + jnp.exp(sink - m)
        acc = _dot(p_ctx.astype(BF16), vs[-1])
        if local:
            den = den + jnp.sum(p_loc, axis=1, keepdims=True)
            acc = acc + _dot(p_loc.astype(BF16), jnp.concatenate(vs[:3], axis=0))
        out = acc / den
        for g in range(G):
            h = j * G + g
            o_ref[:, h * hd:(h + 1) * hd] = out[g * L:(g + 1) * L, :].astype(o_ref.dtype)


def _attn_kernel(sink_ref, q_ref, kp_ref, kc_ref, kn_ref, kx_ref, vp_ref, vc_ref, vn_ref, vx_ref, o_ref, *, cb, nb):
    i = pl.program_id(0)

    @pl.when(i < cb)
    def _():
        _attn_block(i, nb, sink_ref, q_ref, (kx_ref,), (vx_ref,), o_ref, False)

    @pl.when(i >= cb)
    def _():
        _attn_block(i - cb, nb, sink_ref, q_ref, (kp_ref, kc_ref, kn_ref, kx_ref),
                    (vp_ref, vc_ref, vn_ref, vx_ref), o_ref, True)


def _attention(qr, kr, P, v_block, sink, n_ctx):
    T, QW = qr.shape
    KW = kr.shape[1]
    L = ATTN_BLOCK
    cb = n_ctx // L
    nb = (T - n_ctx) // L
    lo, hi = cb, cb + nb - 1
    shifts = (lambda i: jnp.clip(i - 1, lo, hi), lambda i: jnp.clip(i, lo, hi), lambda i: jnp.clip(i + 1, lo, hi))
    kspec = lambda f: pl.BlockSpec((L, KW), lambda i: (f(i), 0))
    vspec = lambda f: pl.BlockSpec((L, KW), lambda i: (f(i), v_block))
    in_specs = ([pl.BlockSpec(memory_space=pltpu.SMEM), pl.BlockSpec((L, QW), lambda i: (i, 0))]
                + [kspec(f) for f in shifts] + [pl.BlockSpec((n_ctx, KW), lambda i: (0, 0))]
                + [vspec(f) for f in shifts] + [pl.BlockSpec((n_ctx, KW), lambda i: (0, v_block))])
    return pl.pallas_call(
        functools.partial(_attn_kernel, cb=cb, nb=nb),
        grid=(cb + nb,),
        in_specs=in_specs,
        out_specs=pl.BlockSpec((L, QW), lambda i: (i, 0)),
        out_shape=jax.ShapeDtypeStruct((T, QW), BF16),
        compiler_params=_cp(("arbitrary",)),
        name="attn",
    )(sink, qr, kr, kr, kr, kr, P, P, P, P)


def _oproj_kernel(x_ref, gt_ref, a_ref, b_ref, c_ref, wa_ref, wb_ref, wc_ref, o_ref, *, tm, n_ctx):
    acc = _dot(a_ref[...], wa_ref[...]) + _dot(b_ref[...], wb_ref[...]) + _dot(c_ref[...], wc_ref[...])
    gate = _pick_mod(gt_ref, _ctx_rows(pl.program_id(0), tm, n_ctx))
    o_ref[...] = x_ref[...] + gate * acc


def _oproj(x, mods, l, y_rw, y_gm, y_at, w_o, n_ctx, tm):
    T, D = x.shape
    tn = 512
    W1 = y_rw.shape[1]
    W3 = y_at.shape[1]
    gate_col = 2 * (D // tn)
    return pl.pallas_call(
        functools.partial(_oproj_kernel, tm=tm, n_ctx=n_ctx),
        grid=(T // tm, D // tn),
        in_specs=[pl.BlockSpec((tm, tn), lambda i, j: (i, j)),
                  pl.BlockSpec((1, 8, tn), lambda i, j: (l, 0, gate_col + j)),
                  pl.BlockSpec((tm, W1), lambda i, j: (i, 0)),
                  pl.BlockSpec((tm, W1), lambda i, j: (i, 0)),
                  pl.BlockSpec((tm, W3), lambda i, j: (i, 0)),
                  pl.BlockSpec((W1, tn), lambda i, j: (0, j)),
                  pl.BlockSpec((W1, tn), lambda i, j: (1, j)),
                  pl.BlockSpec((W3, tn), lambda i, j: (1, j))],
        out_specs=pl.BlockSpec((tm, tn), lambda i, j: (i, j)),
        out_shape=jax.ShapeDtypeStruct((T, D), F32),
        compiler_params=_cp(("arbitrary", "arbitrary")),
        name="oproj",
    )(x, mods, y_rw, y_gm, y_at, w_o, w_o, w_o)


def _router_kernel(x_ref, sh_ref, sc_ref, g_ref, rt_ref, bias_ref, tri_ref,
                   h_ref, hp_ref, idx_ref, wt_ref, rank_ref, cnt_ref, carry, *, tm, n_ctx):
    i = pl.program_id(0)

    @pl.when(i == 0)
    def _():
        carry[...] = jnp.zeros_like(carry)

    is_ctx = _ctx_rows(i, tm, n_ctx)
    h = _norm_mod(x_ref[...], g_ref[...], _pick_mod(sh_ref, is_ctx), _pick_mod(sc_ref, is_ctx))
    h_ref[...] = h.astype(h_ref.dtype)
    _store_slabs(hp_ref, _pack_halves(h), tm)
    E = N_EXPERTS
    pg = E // N_EXPERT_GROUPS
    neg = -jnp.inf
    scores = _sigmoid(_dot_nt(rt_ref[...], h, HI))
    biased = scores + bias_ref[:, 0:1]
    b3 = biased.reshape(N_EXPERT_GROUPS, pg, tm)
    i3 = lax.broadcasted_iota(jnp.int32, b3.shape, 1)
    m1 = jnp.max(b3, axis=1, keepdims=True)
    first = jnp.min(jnp.where(b3 == m1, i3, pg), axis=1, keepdims=True)
    m2 = jnp.max(jnp.where(i3 == first, neg, b3), axis=1, keepdims=True)
    gs = (m1 + m2).reshape(N_EXPERT_GROUPS, tm)
    gi = lax.broadcasted_iota(jnp.int32, gs.shape, 0)
    gsel = jnp.zeros(gs.shape, jnp.bool_)
    for _ in range(TOPK_GROUPS):
        gm = jnp.max(gs, axis=0, keepdims=True)
        gfirst = jnp.min(jnp.where(gs == gm, gi, N_EXPERT_GROUPS), axis=0, keepdims=True)
        hit = gi == gfirst
        gsel = gsel | hit
        gs = jnp.where(hit, neg, gs)
    masked = jnp.where(gsel.reshape(N_EXPERT_GROUPS, 1, tm), b3, neg).reshape(E, tm)
    ei = lax.broadcasted_iota(jnp.int32, (E, tm), 0)
    sel = jnp.zeros((E, tm), jnp.bool_)
    picks = []
    for _ in range(TOP_K):
        mx = jnp.max(masked, axis=0, keepdims=True)
        efirst = jnp.min(jnp.where(masked == mx, ei, E), axis=0, keepdims=True)
        hit = ei == efirst
        sel = sel | hit
        masked = jnp.where(hit, neg, masked)
        picks.append((efirst, hit, jnp.sum(jnp.where(hit, scores, 0.0), axis=0, keepdims=True)))
    wsum = picks[0][2]
    for pk in picks[1:]:
        wsum = wsum + pk[2]
    self_f = jnp.where(sel, 1.0, 0.0)
    rank_dense = carry[:, 0:1] + _dot(self_f.astype(BF16), tri_ref[...])
    carry[...] = carry[...] + jnp.sum(self_f, axis=1, keepdims=True)
    cnt_ref[...] = carry[...]
    for kx, (efirst, hit, wk) in enumerate(picks):
        idx_ref[kx:kx + 1, :] = efirst
        wt_ref[kx:kx + 1, :] = wk / wsum * ROUTED_SCALE
        rank_ref[kx:kx + 1, :] = jnp.sum(jnp.where(hit, rank_dense, 0.0), axis=0, keepdims=True).astype(jnp.int32)


def _router(x, mods, l, g, router_t, bias, tri, n_ctx, tm):
    T, D = x.shape
    E = router_t.shape[0]
    kspec = pl.BlockSpec((TOP_K, tm), lambda i: (0, i))
    return pl.pallas_call(
        functools.partial(_router_kernel, tm=tm, n_ctx=n_ctx),
        grid=(T // tm,),
        in_specs=[pl.BlockSpec((tm, D), lambda i: (i, 0)),
                  pl.BlockSpec((1, 8, D), lambda i: (l, 0, 3)),
                  pl.BlockSpec((1, 8, D), lambda i: (l, 0, 4)),
                  pl.BlockSpec((1, D), lambda i: (0, 0)),
                  pl.BlockSpec((E, D), lambda i: (0, 0)),
                  pl.BlockSpec((E, 128), lambda i: (0, 0)),
                  pl.BlockSpec((tm, tm), lambda i: (0, 0))],
        out_specs=[pl.BlockSpec((tm, D), lambda i: (i, 0)), pl.BlockSpec((tm * SLAB, 128), lambda i: (i, 0)),
                   kspec, kspec, kspec, pl.BlockSpec((E, 128), lambda i: (0, 0))],
        out_shape=[jax.ShapeDtypeStruct((T, D), BF16),
                   jax.ShapeDtypeStruct((T * SLAB, 128), jnp.uint32),
                   jax.ShapeDtypeStruct((TOP_K, T), jnp.int32),
                   jax.ShapeDtypeStruct((TOP_K, T), F32),
                   jax.ShapeDtypeStruct((TOP_K, T), jnp.int32),
                   jax.ShapeDtypeStruct((E, 128), F32)],
        scratch_shapes=[pltpu.VMEM((E, 128), F32)],
        compiler_params=_cp(("arbitrary",)),
        name="router",
    )(x, mods, mods, g, router_t, bias, tri)


def _gather_slabs(idx_ref, n, src_hbm, dst, sem):
    def body(r, carry):
        src = src_hbm.at[pl.ds(pl.multiple_of(idx_ref[r], SLAB), SLAB), :]
        pltpu.make_async_copy(src, dst.at[pl.ds(r * PITCH, SLAB), :], sem).start()
        return carry
    lax.fori_loop(0, n, body, 0, unroll=8)


def _issue_slabs(idx_ref, lo, hi, src_hbm, dst, sem):
    for r in range(lo, hi):
        src = src_hbm.at[pl.ds(pl.multiple_of(idx_ref[r], SLAB), SLAB), :]
        pltpu.make_async_copy(src, dst.at[pl.ds(r * PITCH, SLAB), :], sem).start(priority=r % 2)


def _wait_slabs(n, src_hbm, dst, sem):
    pltpu.make_async_copy(src_hbm.at[pl.ds(0, n * SLAB), :], dst.at[pl.ds(0, n * SLAB), :], sem).wait()


def _experts_kernel(be_ref, nu_ref, tokc_ref, tokn_ref, h_hbm, wg_ref, wu_ref, wd_ref, y_ref,
                    hbuf, sem, wg_s, wu_s, wd_s, *, TB, nblk):
    b = pl.program_id(0)
    slot = b % 2
    n_used = nu_ref[0]
    SW = h_hbm.shape[1]

    @pl.when(b == 0)
    def _():
        _gather_slabs(tokc_ref.at[0, 0], TB, h_hbm, hbuf.at[0], sem.at[0])

    new_expert = (b == 0) | (be_ref[b] != be_ref[jnp.maximum(b - 1, 0)])

    @pl.when(new_expert & (b < n_used))
    def _():
        wg_s[...] = wg_ref[0, 0].astype(BF16)
        wu_s[...] = wu_ref[0, 0].astype(BF16)
        wd_s[...] = wd_ref[0, 0].astype(BF16)

    @pl.when(b < n_used)
    def _():
        _wait_slabs(TB, h_hbm, hbuf.at[slot], sem.at[slot])
        nxt = (tokn_ref.at[0, 0], h_hbm, hbuf.at[1 - slot], sem.at[1 - slot])
        burst = TB // (2 * SLAB)
        issued = 0
        half = SLAB * SW
        a = u = None
        for s in range(0, SLAB, 2):
            w0 = hbuf[slot, pl.ds(s, TB, stride=PITCH), :]
            w1 = hbuf[slot, pl.ds(s + 1, TB, stride=PITCH), :]
            for unpack, base in ((_unpack_lo, 0), (_unpack_hi, half)):
                xs = jnp.concatenate([unpack(w0), unpack(w1)], axis=1).astype(BF16)
                k0 = base + s * SW
                da = _dot(xs, wg_s[k0:k0 + 2 * SW, :])
                _issue_slabs(nxt[0], issued, issued + burst, *nxt[1:])
                du = _dot(xs, wu_s[k0:k0 + 2 * SW, :])
                _issue_slabs(nxt[0], issued + burst, issued + 2 * burst, *nxt[1:])
                issued += 2 * burst
                a = da if a is None else a + da
                u = du if u is None else u + du
        act = (a * _sigmoid(a) * u).astype(BF16)
        _store_slabs(y_ref, _pack_halves(_dot(act, wd_s[...])), TB)

        @pl.when(b + 1 >= n_used)
        def _():
            _wait_slabs(TB, h_hbm, hbuf.at[1 - slot], sem.at[1 - slot])

    @pl.when(b >= n_used)
    def _():
        y_ref[...] = jnp.zeros_like(y_ref)


def _experts(h2s, tok_rows, block_e, n_used, wg, wu, wd, l):
    SW = h2s.shape[1]
    D = 2 * SW * SLAB
    nblk = block_e.shape[0]
    TB = EXPERT_TILE
    FF = wg.shape[3]
    tok3 = tok_rows.reshape(nblk, 1, TB)
    grid_spec = pltpu.PrefetchScalarGridSpec(
        num_scalar_prefetch=2,
        grid=(nblk,),
        in_specs=[pl.BlockSpec((1, 1, TB), lambda b, be, nu: (b, 0, 0), memory_space=pltpu.SMEM),
                  pl.BlockSpec((1, 1, TB), lambda b, be, nu: (jnp.minimum(b + 1, nblk - 1), 0, 0),
                               memory_space=pltpu.SMEM),
                  pl.BlockSpec(memory_space=pl.ANY),
                  pl.BlockSpec((1, 1, D, FF), lambda b, be, nu: (l, be[b], 0, 0)),
                  pl.BlockSpec((1, 1, D, FF), lambda b, be, nu: (l, be[b], 0, 0)),
                  pl.BlockSpec((1, 1, FF, D), lambda b, be, nu: (l, be[b], 0, 0))],
        out_specs=pl.BlockSpec((TB * SLAB, SW), lambda b, be, nu: (b, 0)),
        scratch_shapes=[pltpu.VMEM((2, TB * PITCH, SW), jnp.uint32), pltpu.SemaphoreType.DMA((2,)),
                        pltpu.VMEM((D, FF), BF16), pltpu.VMEM((D, FF), BF16), pltpu.VMEM((FF, D), BF16)],
    )
    return pl.pallas_call(
        functools.partial(_experts_kernel, TB=TB, nblk=nblk),
        grid_spec=grid_spec,
        out_shape=jax.ShapeDtypeStruct((nblk * TB * SLAB, SW), jnp.uint32),
        compiler_params=_cp(("arbitrary",)),
        name="experts",
    )(block_e, n_used, tok3, tok3, h2s, wg, wu, wd)


def _combine_kernel(slc_ref, sln_ref, y_hbm, x_ref, h_ref, wt_ref, gt_ref, swg_ref, swu_ref, swd_ref, o_ref,
                    ybuf, sem, *, tm, n_ctx, nt):
    i = pl.program_id(0)
    slot = i % 2
    SW = y_hbm.shape[1]

    @pl.when(i == 0)
    def _():
        for k in range(TOP_K):
            _gather_slabs(slc_ref.at[0, k], tm, y_hbm, ybuf.at[0, k], sem.at[0])

    hb = h_ref[...]
    a = _dot(hb, swg_ref[...])
    u = _dot(hb, swu_ref[...])
    shared = _dot((a * _sigmoid(a) * u).astype(BF16), swd_ref[...])
    gate = _pick_mod(gt_ref, _ctx_rows(i, tm, n_ctx))
    wk = [jnp.broadcast_to(wt_ref[:, k:k + 1], (tm, SW)) for k in range(TOP_K)]
    for k in range(TOP_K):
        _wait_slabs(tm, y_hbm, ybuf.at[slot, k], sem.at[slot])
    half = SLAB * SW
    burst = tm // SLAB
    for s in range(SLAB):
        lo = slice(s * SW, (s + 1) * SW)
        hi = slice(half + s * SW, half + (s + 1) * SW)
        acc_lo = shared[:, lo]
        acc_hi = shared[:, hi]
        for k in range(TOP_K):
            w = ybuf[slot, k, pl.ds(s, tm, stride=PITCH), :]
            acc_lo = acc_lo + wk[k] * _unpack_lo(w)
            acc_hi = acc_hi + wk[k] * _unpack_hi(w)
            _issue_slabs(sln_ref.at[0, k], s * burst, (s + 1) * burst, y_hbm, ybuf.at[1 - slot, k], sem.at[1 - slot])
        o_ref[:, lo] = x_ref[:, lo] + gate[:, lo] * acc_lo
        o_ref[:, hi] = x_ref[:, hi] + gate[:, hi] * acc_hi

    @pl.when(i + 1 >= nt)
    def _():
        for k in range(TOP_K):
            _wait_slabs(tm, y_hbm, ybuf.at[1 - slot, k], sem.at[1 - slot])


def _combine(y2, slot_rows, wt_t, x, h2, mods, l, swg, swu, swd, n_ctx, tm):
    T, D = x.shape
    SW = y2.shape[1]
    nt = T // tm
    FF = swg.shape[1]
    sl3 = slot_rows.reshape(TOP_K, nt, tm).transpose(1, 0, 2)
    return pl.pallas_call(
        functools.partial(_combine_kernel, tm=tm, n_ctx=n_ctx, nt=nt),
        grid=(nt,),
        in_specs=[pl.BlockSpec((1, TOP_K, tm), lambda i: (i, 0, 0), memory_space=pltpu.SMEM),
                  pl.BlockSpec((1, TOP_K, tm), lambda i: (jnp.minimum(i + 1, nt - 1), 0, 0),
                               memory_space=pltpu.SMEM),
                  pl.BlockSpec(memory_space=pl.ANY),
                  pl.BlockSpec((tm, D), lambda i: (i, 0)),
                  pl.BlockSpec((tm, D), lambda i: (i, 0)),
                  pl.BlockSpec((tm, TOP_K), lambda i: (i, 0)),
                  pl.BlockSpec((1, 8, D), lambda i: (l, 0, 5)),
                  pl.BlockSpec((D, FF), lambda i: (0, 0)),
                  pl.BlockSpec((D, FF), lambda i: (0, 0)),
                  pl.BlockSpec((FF, D), lambda i: (0, 0))],
        out_specs=pl.BlockSpec((tm, D), lambda i: (i, 0)),
        out_shape=jax.ShapeDtypeStruct((T, D), F32),
        scratch_shapes=[pltpu.VMEM((2, TOP_K, tm * PITCH, SW), jnp.uint32), pltpu.SemaphoreType.DMA((2,))],
        compiler_params=_cp(("arbitrary",)),
        name="combine",
    )(sl3, sl3, y2, x, h2, wt_t, mods, swg, swu, swd)


def _seg_indicator(width, seg):
    idx = np.arange(width) // seg
    return jnp.asarray(idx[:, None] == idx[None, :], dtype=BF16)


def _rope_tables(n_ctx, n_lat):
    rows = n_lat // GRID_W
    row = jnp.repeat(jnp.arange(rows, dtype=F32), GRID_W)
    col = jnp.tile(jnp.arange(GRID_W, dtype=F32), rows)
    n_freq = ATTN_HEAD // 4
    inv_freq = ROPE_THETA ** (-jnp.arange(n_freq, dtype=F32) / n_freq)
    ar, ac = row[:, None] * inv_freq, col[:, None] * inv_freq
    cos = jnp.concatenate([jnp.cos(ar), jnp.cos(ar), jnp.cos(ac), jnp.cos(ac)], axis=1)
    sin = jnp.concatenate([-jnp.sin(ar), jnp.sin(ar), -jnp.sin(ac), jnp.sin(ac)], axis=1)
    cos = jnp.concatenate([jnp.ones((n_ctx, ATTN_HEAD), F32), cos], axis=0)
    sin = jnp.concatenate([jnp.zeros((n_ctx, ATTN_HEAD), F32), sin], axis=0)
    return jnp.tile(cos, (1, 2)), jnp.tile(sin, (1, 2))


def kernel(x, c, ctx, c_ctx, w_ada, b_ada, norm1_g, norm2_g, w_in, w_o, rw_mu, rw_w0, rw_w2, rw_a0, rw_a2, rw_g2, rw_kk, rw_ka, rw_rk, rw_ln_g, rw_ln_b, gm_norm_g, gm_ws, gm_b, at_qn, at_kn, at_sink, moe_router, moe_bias, moe_wg, moe_wu, moe_wd, sh_wg, sh_wu, sh_wd):
    B, S, D = x.shape
    C = ctx.shape[1]
    assert B == 1
    L = w_ada.shape[0]
    T = C + S
    RW = rw_kk.shape[1]
    GW = gm_norm_g.shape[1]
    QW = at_sink.shape[1] * ATTN_HEAD
    KW = QW // ATTN_GROUP
    rw_proj = rw_mu.shape[2]
    RWP = 2048
    assert rw_proj <= RWP and T % 768 == 0 and C % 256 == 0 and S % 256 == 0 and D == 2 * SLAB * 128

    xs = jnp.concatenate([ctx[0], x[0]], axis=0)
    cond8 = jnp.zeros((8, D), F32).at[0].set(c_ctx).at[1].set(c[0])
    mods = _ada(cond8, w_ada, b_ada)

    gh_rw = _seg_indicator(RW, RWKV_HEAD)
    gh_q = _seg_indicator(QW, ATTN_HEAD)
    gh_k = _seg_indicator(KW, ATTN_HEAD)
    cos, sin = _rope_tables(C, S)
    tri = jnp.asarray(np.arange(256)[:, None] < np.arange(256)[None, :], dtype=BF16)
    TB = EXPERT_TILE
    nblk = -(-(T * TOP_K) // TB) + N_EXPERTS

    for l in range(L):
        w_in_l = w_in[l]
        w_in_p = jnp.concatenate([w_in_l[:, :rw_proj], jnp.zeros((D, RWP - rw_proj), F32), w_in_l[:, rw_proj:]],
                                 axis=1).astype(BF16)
        P = _inproj(xs, mods, l, norm1_g[l][None], w_in_p, C, 768)
        gm_block = RWP // (2 * GW)
        q_block = (RWP + 2 * GW) // QW
        k_block = (RWP + 2 * GW + QW) // KW
        v_block = k_block + 1

        pad = lambda a: jnp.pad(a, ((0, 0), (0, RWP - rw_proj)))
        prm = {"mu": pad(rw_mu[l]), "w0": rw_w0[l], "w2": rw_w2[l], "a0": rw_a0[l], "a2": rw_a2[l],
               "g2": rw_g2[l], "kk": rw_kk[l][None], "ka": rw_ka[l][None], "rk": rw_rk[l].reshape(1, RW)}
        r, v, kk, lwf, kf, bf, lwb, kb, bb, gate, bonus = _rwkv_prep(P, prm, gh_rw, C, 256)
        yf, yb = _rwkv_scan(r, v, kk, lwf, kf, bf, lwb, kb, bb, C)
        y_rw = _rwkv_post(yf, yb, bonus, gate, rw_ln_g[l][None], rw_ln_b[l][None], gh_rw, 256)

        gm_bias = jnp.repeat(gm_b[l].T, GW // GM_GROUPS, axis=1)
        y_gm = _gmlp(P, gm_block, gm_norm_g[l][None], gm_ws[l].astype(BF16), gm_bias, 256)

        qg = jnp.tile(at_qn[l], QW // ATTN_HEAD)[None]
        kg = jnp.tile(at_kn[l], KW // ATTN_HEAD)[None]
        qr, kr = _qk_prep(P, q_block, k_block, cos, sin, qg, kg, gh_q, gh_k, 256)
        y_at = _attention(qr, kr, P, v_block, at_sink[l], C)

        xs = _oproj(xs, mods, l, y_rw, y_gm, y_at, w_o[l].astype(BF16), C, 768)

        bias_col = jnp.broadcast_to(moe_bias[l][:, None], (N_EXPERTS, 128))
        h2, h2s, idx8, wt8, rank8, cnt = _router(xs, mods, l, norm2_g[l][None], moe_router[l].T, bias_col, tri, C, 256)
        counts = cnt[:, 0].astype(jnp.int32)
        padded = (counts + TB - 1) // TB * TB
        pad_end = jnp.cumsum(padded)
        pad_start = pad_end - padded
        e_ids = jnp.arange(N_EXPERTS, dtype=jnp.int32)
        slot8 = jnp.sum(jnp.where(idx8[:, :, None] == e_ids, pad_start, 0), axis=-1) + rank8
        tok = jnp.zeros((nblk * TB,), jnp.int32).at[slot8.reshape(-1)].set(
            jnp.tile(jnp.arange(T, dtype=jnp.int32), TOP_K), unique_indices=True)
        blk_pos = jnp.arange(nblk, dtype=jnp.int32) * TB
        block_e = jnp.minimum(jnp.sum((pad_end[None, :] <= blk_pos[:, None]).astype(jnp.int32), axis=1), N_EXPERTS - 1)
        n_used = (pad_end[-1] // TB).astype(jnp.int32).reshape(1)
        y2 = _experts(h2s, tok * SLAB, block_e, n_used, moe_wg, moe_wu, moe_wd, l)
        xs = _combine(y2, slot8 * SLAB, wt8.T, xs, h2, mods, l, sh_wg[l].astype(BF16), sh_wu[l].astype(BF16),
                      sh_wd[l].astype(BF16), C, 128)
    return xs[C:].reshape(B, S, D)
```

```python
import functools

import jax
import jax.numpy as jnp
import numpy as np
from jax import lax
from jax.experimental import pallas as pl
from jax.experimental.pallas import tpu as pltpu

F32 = jnp.float32
BF16 = jnp.bfloat16
HI = lax.Precision.HIGHEST

NORM_EPS = 1e-6
GRID_W = 64

RWKV_HEAD = 64
DECAY_LORA = 64
ICLR_LORA = 64
GATE_LORA = 128
RWKV_GN_EPS = 64e-5

GM_GROUPS = 8
GM_CHUNK = 128

ATTN_HEAD = 64
ATTN_GROUP = 4
ATTN_WINDOW = 128
ATTN_BLOCK = 128
ROPE_THETA = 10000.0

N_EXPERTS = 64
TOP_K = 8
N_EXPERT_GROUPS = 8
TOPK_GROUPS = 4
ROUTED_SCALE = 2.5

SCAN_CHUNK = 64
EXPERT_TILE = 256
SLAB = 8
PITCH = 12
VMEM_LIMIT = 56 * 1024 * 1024


def _cp(sem, vmem=VMEM_LIMIT):
    return pltpu.CompilerParams(dimension_semantics=sem, vmem_limit_bytes=vmem)


def _dot(a, b, prec=None):
    return jnp.dot(a, b, preferred_element_type=F32, precision=prec)


def _dot_nt(a, b, prec=None):
    return lax.dot_general(a, b, (((1,), (1,)), ((), ())), preferred_element_type=F32, precision=prec)


def _dot_tn(a, b, prec=None):
    return lax.dot_general(a, b, (((0,), (0,)), ((), ())), preferred_element_type=F32, precision=prec)


def _seg_sum(x, g):
    xh = x.astype(BF16)
    xl = (x - xh.astype(F32)).astype(BF16)
    return _dot(xh, g) + _dot(xl, g)


def _sigmoid(x):
    return jax.nn.sigmoid(x)


def _pack_halves(x):
    half = x.shape[1] // 2
    lo = lax.bitcast_convert_type(x[:, :half].astype(BF16).astype(F32), jnp.uint32)
    hi = lax.bitcast_convert_type(x[:, half:].astype(BF16).astype(F32), jnp.uint32)
    return hi | (lo >> 16)


def _unpack_lo(w):
    return lax.bitcast_convert_type(w << 16, F32)


def _unpack_hi(w):
    return lax.bitcast_convert_type(w & jnp.uint32(0xFFFF0000), F32)


def _store_slabs(ref, words, n):
    for c in range(SLAB):
        ref[pl.ds(c, n, stride=SLAB), :] = words[:, c * 128:(c + 1) * 128]


def _norm_mod(x, g, sh, sc):
    y = x * lax.rsqrt(jnp.mean(x * x, axis=-1, keepdims=True) + NORM_EPS)
    return (y * g) * (1.0 + sc) + sh


def _ctx_rows(i, tm, n_ctx):
    row = i * tm + lax.broadcasted_iota(jnp.int32, (tm, 1), 0)
    return row < n_ctx


def _pick_mod(mod_ref, is_ctx):
    return jnp.where(is_ctx, mod_ref[0, 0:1, :], mod_ref[0, 1:2, :])


def _ada_kernel(c_ref, w_ref, b_ref, o_ref):
    c = c_ref[...]
    s = c * _sigmoid(c)
    o_ref[0] = _dot(s, w_ref[0], HI) + b_ref[0]


def _ada(cond8, w_ada, b_ada):
    L, D, N = w_ada.shape
    tn = 1024
    return pl.pallas_call(
        _ada_kernel,
        grid=(L, N // tn),
        in_specs=[pl.BlockSpec((8, D), lambda l, j: (0, 0)),
                  pl.BlockSpec((1, D, tn), lambda l, j: (l, 0, j)),
                  pl.BlockSpec((1, 1, tn), lambda l, j: (l, 0, j))],
        out_specs=pl.BlockSpec((1, 8, tn), lambda l, j: (l, 0, j)),
        out_shape=jax.ShapeDtypeStruct((L, 8, N), F32),
        compiler_params=_cp(("arbitrary", "arbitrary")),
        name="ada",
    )(cond8, w_ada, b_ada.reshape(L, 1, N))


def _inproj_kernel(x_ref, sh_ref, sc_ref, g_ref, w_ref, o_ref, h_scr, *, tm, n_ctx):
    i = pl.program_id(0)

    @pl.when(pl.program_id(1) == 0)
    def _():
        is_ctx = _ctx_rows(i, tm, n_ctx)
        h = _norm_mod(x_ref[...], g_ref[...], _pick_mod(sh_ref, is_ctx), _pick_mod(sc_ref, is_ctx))
        h_scr[...] = h.astype(BF16)

    o_ref[...] = _dot(h_scr[...], w_ref[...])


def _inproj(x, mods, l, g, w, n_ctx, tm):
    T, D = x.shape
    N = w.shape[1]
    tn = 512
    return pl.pallas_call(
        functools.partial(_inproj_kernel, tm=tm, n_ctx=n_ctx),
        grid=(T // tm, N // tn),
        in_specs=[pl.BlockSpec((tm, D), lambda i, j: (i, 0)),
                  pl.BlockSpec((1, 8, D), lambda i, j: (l, 0, 0)),
                  pl.BlockSpec((1, 8, D), lambda i, j: (l, 0, 1)),
                  pl.BlockSpec((1, D), lambda i, j: (0, 0)),
                  pl.BlockSpec((D, tn), lambda i, j: (0, j))],
        out_specs=pl.BlockSpec((tm, tn), lambda i, j: (i, j)),
        out_shape=jax.ShapeDtypeStruct((T, N), F32),
        scratch_shapes=[pltpu.VMEM((tm, D), BF16)],
        compiler_params=_cp(("arbitrary", "arbitrary")),
        name="inproj",
    )(x, mods, mods, g, w)


def _softplus(x):
    return jnp.maximum(x, 0.0) + jnp.log(1.0 + jnp.exp(-jnp.abs(x)))


def _rwkv_prep_kernel(p_ref, pp_ref, pn_ref, mu_ref, w0_ref, w2_ref, a0_ref, a2_ref, g2_ref,
                      kkp_ref, kap_ref, rkp_ref, gh_ref,
                      r_ref, v_ref, kk_ref, lwf_ref, kf_ref, bf_ref, lwb_ref, kb_ref, bb_ref,
                      gate_ref, bonus_ref, *, tm, n_ctx, n_tot, W):
    i = pl.program_id(0)
    p = p_ref[...]
    lrow = lax.broadcasted_iota(jnp.int32, (tm, 1), 0)
    grow = i * tm + lrow
    prev = jnp.where(lrow == 0, pp_ref[7:8, :], pltpu.roll(p, 1, axis=0))
    prev = jnp.where((grow == 0) | (grow == n_ctx), 0.0, prev)
    nxt = jnp.where(lrow == tm - 1, pn_ref[0:1, :], pltpu.roll(p, tm - 1, axis=0))
    nxt = jnp.where((grow == n_ctx - 1) | (grow == n_tot - 1), 0.0, nxt)
    ps = p + mu_ref[0:1, :] * (prev - p) + mu_ref[1:2, :] * (nxt - p)

    r = ps[:, 0:W]
    k = ps[:, W:2 * W]
    v = ps[:, 2 * W:3 * W]
    o = 3 * W
    wd = (ps[:, o:o + DECAY_LORA], ps[:, o + DECAY_LORA:o + 2 * DECAY_LORA])
    o += 2 * DECAY_LORA
    ad = (ps[:, o:o + ICLR_LORA], ps[:, o + ICLR_LORA:o + 2 * ICLR_LORA])
    o += 2 * ICLR_LORA
    gd = ps[:, o:o + GATE_LORA]

    gh = gh_ref[...]
    kk = k * kkp_ref[...]
    kk = kk / jnp.maximum(jnp.sqrt(_seg_sum(kk * kk, gh)), 1e-12)
    r_ref[...] = r
    v_ref[...] = v
    kk_ref[...] = kk
    outs = ((lwf_ref, kf_ref, bf_ref), (lwb_ref, kb_ref, bb_ref))
    for d in range(2):
        z = w0_ref[d:d + 1, :] + _dot(jnp.tanh(wd[d]), w2_ref[d], HI)
        w_log = -_softplus(-z) - 0.5
        a = _sigmoid(a0_ref[d:d + 1, :] + _dot(ad[d], a2_ref[d], HI))
        lw_ref, kd_ref, bd_ref = outs[d]
        lw_ref[...] = -jnp.exp(w_log)
        kd_ref[...] = k * (1.0 + (a - 1.0) * kap_ref[...])
        bd_ref[...] = kk * a
    gate_ref[...] = _dot(_sigmoid(gd), g2_ref[...], HI)
    bonus_ref[...] = _seg_sum(r * k * rkp_ref[...], gh) * v


def _rwkv_prep(P, prm, gh, n_ctx, tm):
    T = P.shape[0]
    W = prm["kk"].shape[1]
    PW = 2048
    nb8 = T // 8
    row = pl.BlockSpec((tm, W), lambda i: (i, 0))
    full = lambda a: pl.BlockSpec(a.shape, lambda i: (0,) * a.ndim)
    args = (prm["mu"], prm["w0"], prm["w2"], prm["a0"], prm["a2"], prm["g2"], prm["kk"], prm["ka"], prm["rk"], gh)
    return pl.pallas_call(
        functools.partial(_rwkv_prep_kernel, tm=tm, n_ctx=n_ctx, n_tot=T, W=W),
        grid=(T // tm,),
        in_specs=[pl.BlockSpec((tm, PW), lambda i: (i, 0)),
                  pl.BlockSpec((8, PW), lambda i: (jnp.maximum(i * (tm // 8) - 1, 0), 0)),
                  pl.BlockSpec((8, PW), lambda i: (jnp.minimum((i + 1) * (tm // 8), nb8 - 1), 0))]
                 + [full(a) for a in args],
        out_specs=[row] * 11,
        out_shape=[jax.ShapeDtypeStruct((T, W), F32)] * 11,
        compiler_params=_cp(("arbitrary",)),
        name="rwkv_prep",
    )(P, P, P, *args)


def _rwkv_scan_kernel(rf, vf, kkf, lwf, kf, bf, rb, vb, kkb, lwb, kb, bb, yf_ref, yb_ref, s_scr, *, C, H, N):
    @pl.when(pl.program_id(0) == 0)
    def _():
        s_scr[...] = jnp.zeros_like(s_scr)

    row = lax.broadcasted_iota(jnp.int32, (C, C), 0)
    col = lax.broadcasted_iota(jnp.int32, (C, C), 1)
    eye = (row == col).astype(F32)
    n_sq = int(np.log2(C)) - 1
    dirs = ((rf, vf, kkf, lwf, kf, bf), (rb, vb, kkb, lwb, kb, bb))
    ch = []
    for d, (r_ref, v_ref, kk_ref, lw_ref, k_ref, b_ref) in enumerate(dirs):
        incl = (col <= row) if d == 0 else (col >= row)
        strict = (col < row) if d == 0 else (col > row)
        lw = lw_ref[...]
        cum = _dot(incl.astype(F32), lw, HI)
        e_pos = jnp.exp(cum)
        e_neg = jnp.exp(-cum)
        rt = (r_ref[...] * e_pos).astype(BF16)
        at = (-kk_ref[...] * jnp.exp(cum - lw)).astype(BF16)
        bt = (b_ref[...] * e_neg).astype(BF16)
        kt = (k_ref[...] * e_neg).astype(BF16)
        vv = v_ref[...].astype(BF16)
        gam = e_pos[C - 1:C, :] if d == 0 else e_pos[0:1, :]
        for h in range(H):
            sl = slice(h * N, (h + 1) * N)
            ch.append(dict(d=d, h=h, incl=incl, strict=strict, at=at[:, sl], rt=rt[:, sl], bt=bt[:, sl],
                           kt=kt[:, sl], v=vv[:, sl], gam=gam[:, sl], s0=s_scr[d, h]))
    for c in ch:
        a_all = _dot_nt(jnp.concatenate([c["at"], c["rt"]], axis=0), jnp.concatenate([c["bt"], c["kt"]], axis=0))
        c["l_pow"] = jnp.where(c["strict"], a_all[:C, :C], 0.0)
        c["l_ak"] = jnp.where(c["strict"], a_all[:C, C:], 0.0).astype(BF16)
        c["m_rb"] = jnp.where(c["incl"], a_all[C:, :C], 0.0).astype(BF16)
        c["m_rk"] = jnp.where(c["incl"], a_all[C:, C:], 0.0).astype(BF16)
        c["t_inv"] = eye + c["l_pow"]
    for _ in range(n_sq):
        for c in ch:
            lb = c["l_pow"].astype(BF16)
            c["l_pow"] = _dot(lb, lb)
        for c in ch:
            c["t_inv"] = c["t_inv"] + _dot(c["t_inv"].astype(BF16), c["l_pow"].astype(BF16))
    for c in ch:
        c["s0b"] = c["s0"].astype(BF16)
        c["w1"] = _dot_nt(c["at"], c["s0b"]) + _dot(c["l_ak"], c["v"])
    for c in ch:
        c["u"] = _dot(c["t_inv"].astype(BF16), c["w1"].astype(BF16)).astype(BF16)
    for c in ch:
        s_scr[c["d"], c["h"]] = (c["s0"] + _dot_tn(c["u"], c["bt"]) + _dot_tn(c["v"], c["kt"])) * c["gam"]
    for c in ch:
        c["y"] = _dot_nt(c["rt"], c["s0b"]) + _dot(c["m_rb"], c["u"]) + _dot(c["m_rk"], c["v"])
    yf_ref[...] = jnp.concatenate([c["y"] for c in ch[:H]], axis=1)
    yb_ref[...] = jnp.concatenate([c["y"] for c in ch[H:]], axis=1)


def _rwkv_scan(r, v, kk, lwf, kf, bf, lwb, kb, bb, n_ctx):
    T, W = r.shape
    C = SCAN_CHUNK
    H = W // RWKV_HEAD
    nch = T // C
    cch = n_ctx // C
    fwd = pl.BlockSpec((C, W), lambda n: (n, 0))
    bwd = pl.BlockSpec((C, W), lambda n: (jnp.where(n < cch, cch - 1 - n, nch - 1 + cch - n), 0))
    return pl.pallas_call(
        functools.partial(_rwkv_scan_kernel, C=C, H=H, N=RWKV_HEAD),
        grid=(nch,),
        in_specs=[fwd] * 6 + [bwd] * 6,
        out_specs=[fwd, bwd],
        out_shape=[jax.ShapeDtypeStruct((T, W), F32)] * 2,
        scratch_shapes=[pltpu.VMEM((2, H, RWKV_HEAD, RWKV_HEAD), F32)],
        compiler_params=_cp(("arbitrary",)),
        name="rwkv_scan",
    )(r, v, kk, lwf, kf, bf, r, v, kk, lwb, kb, bb)


def _rwkv_post_kernel(yf_ref, yb_ref, bonus_ref, gate_ref, lng_ref, lnb_ref, gh_ref, o_ref):
    gh = gh_ref[...]
    y = yf_ref[...] + yb_ref[...]
    mean = _seg_sum(y, gh) * (1.0 / RWKV_HEAD)
    yc = y - mean
    var = _seg_sum(yc * yc, gh) * (1.0 / RWKV_HEAD)
    yn = yc * lax.rsqrt(var + RWKV_GN_EPS) * lng_ref[...] + lnb_ref[...]
    o_ref[...] = ((yn + bonus_ref[...]) * gate_ref[...]).astype(o_ref.dtype)


def _rwkv_post(yf, yb, bonus, gate, lng, lnb, gh, tm):
    T, W = yf.shape
    row = pl.BlockSpec((tm, W), lambda i: (i, 0))
    full = lambda a: pl.BlockSpec(a.shape, lambda i: (0,) * a.ndim)
    return pl.pallas_call(
        _rwkv_post_kernel,
        grid=(T // tm,),
        in_specs=[row] * 4 + [full(lng), full(lnb), full(gh)],
        out_specs=row,
        out_shape=jax.ShapeDtypeStruct((T, W), BF16),
        compiler_params=_cp(("arbitrary",)),
        name="rwkv_post",
    )(yf, yb, bonus, gate, lng, lnb, gh)


def _gmlp_kernel(p_ref, g_ref, ws_ref, b_ref, o_ref, *, tm, W):
    u = jax.nn.gelu(p_ref[:, 0:W])
    v = jax.nn.gelu(p_ref[:, W:2 * W])
    v = v * lax.rsqrt(jnp.mean(v * v, axis=-1, keepdims=True) + NORM_EPS) * g_ref[...]
    vb = v.astype(BF16)
    gw = W // GM_GROUPS
    for c in range(tm // GM_CHUNK):
        rows = slice(c * GM_CHUNK, (c + 1) * GM_CHUNK)
        parts = [_dot(ws_ref[g], vb[rows, g * gw:(g + 1) * gw]) for g in range(GM_GROUPS)]
        s = jnp.concatenate(parts, axis=1) + b_ref[...]
        o_ref[rows, :] = (u[rows, :] * s).astype(o_ref.dtype)


def _gmlp(P, col_block, g, ws, bias, tm):
    T = P.shape[0]
    W = g.shape[1]
    return pl.pallas_call(
        functools.partial(_gmlp_kernel, tm=tm, W=W),
        grid=(T // tm,),
        in_specs=[pl.BlockSpec((tm, 2 * W), lambda i: (i, col_block)),
                  pl.BlockSpec((1, W), lambda i: (0, 0)),
                  pl.BlockSpec(ws.shape, lambda i: (0, 0, 0)),
                  pl.BlockSpec(bias.shape, lambda i: (0, 0))],
        out_specs=pl.BlockSpec((tm, W), lambda i: (i, 0)),
        out_shape=jax.ShapeDtypeStruct((T, W), BF16),
        compiler_params=_cp(("arbitrary",)),
        name="gmlp",
    )(P, g, ws, bias)


def _rope(x, cos, sin, lane):
    w = x.shape[1]
    partner = jnp.where((lane % 32) < 16, pltpu.roll(x, w - 16, axis=1), pltpu.roll(x, 16, axis=1))
    return x * cos + partner * sin


def _qk_prep_kernel(q_ref, k_ref, cos_ref, sin_ref, qg_ref, kg_ref, ghq_ref, ghk_ref, qo_ref, ko_ref, *, scale):
    cos = cos_ref[...]
    sin = sin_ref[...]
    for x_ref, g_ref, gh_ref, o_ref, mul in ((q_ref, qg_ref, ghq_ref, qo_ref, scale), (k_ref, kg_ref, ghk_ref, ko_ref, 1.0)):
        x = x_ref[...]
        w = x.shape[1]
        ss = _seg_sum(x * x, gh_ref[...]) * (1.0 / ATTN_HEAD)
        xn = x * lax.rsqrt(ss + NORM_EPS) * g_ref[...]
        rep = w // cos.shape[1]
        lane = lax.broadcasted_iota(jnp.int32, x.shape, 1)
        xr = _rope(xn, jnp.tile(cos, (1, rep)), jnp.tile(sin, (1, rep)), lane)
        o_ref[...] = (xr * mul).astype(o_ref.dtype)


def _qk_prep(P, q_block, k_block, cos, sin, qg, kg, ghq, ghk, tm):
    T = P.shape[0]
    QW, KW = qg.shape[1], kg.shape[1]
    full = lambda a: pl.BlockSpec(a.shape, lambda i: (0,) * a.ndim)
    return pl.pallas_call(
        functools.partial(_qk_prep_kernel, scale=ATTN_HEAD ** -0.5),
        grid=(T // tm,),
        in_specs=[pl.BlockSpec((tm, QW), lambda i: (i, q_block)),
                  pl.BlockSpec((tm, KW), lambda i: (i, k_block)),
                  pl.BlockSpec((tm, cos.shape[1]), lambda i: (i, 0)),
                  pl.BlockSpec((tm, sin.shape[1]), lambda i: (i, 0)),
                  full(qg), full(kg), full(ghq), full(ghk)],
        out_specs=[pl.BlockSpec((tm, QW), lambda i: (i, 0)), pl.BlockSpec((tm, KW), lambda i: (i, 0))],
        out_shape=[jax.ShapeDtypeStruct((T, QW), BF16), jax.ShapeDtypeStruct((T, KW), BF16)],
        compiler_params=_cp(("arbitrary",)),
        name="qk_prep",
    )(P, P, cos, sin, qg, kg, ghq, ghk)


def _attn_block(i, nb, sink_ref, q_ref, k_refs, v_refs, o_ref, local):
    L = ATTN_BLOCK
    G = ATTN_GROUP
    hd = ATTN_HEAD
    n_kv = k_refs[-1].shape[1] // hd
    R = G * L
    srow = lax.broadcasted_iota(jnp.int32, (R, 1), 0)
    if local:
        qi = lax.broadcasted_iota(jnp.int32, (R, 3 * L), 0) % L
        kj = lax.broadcasted_iota(jnp.int32, (R, 3 * L), 1)
        rel = kj - L - qi
        valid = (rel <= ATTN_WINDOW) & (rel >= -ATTN_WINDOW)
        valid = valid & ((kj >= L) | (i > 0)) & ((kj < 2 * L) | (i < nb - 1))
    for j in range(n_kv):
        ks = [r[:, j * hd:(j + 1) * hd] for r in k_refs]
        vs = [r[:, j * hd:(j + 1) * hd].astype(BF16) for r in v_refs]
        q = jnp.concatenate([q_ref[:, (j * G + g) * hd:(j * G + g + 1) * hd] for g in range(G)], axis=0)
        sink = jnp.zeros((R, 1), F32)
        for g in range(G):
            sink = jnp.where((srow >= g * L) & (srow < (g + 1) * L), sink_ref[j * G + g], sink)
        s_ctx = _dot_nt(q, ks[-1])
        m = jnp.maximum(jnp.max(s_ctx, axis=1, keepdims=True), sink)
        if local:
            s_loc = _dot_nt(q, jnp.concatenate(ks[:3], axis=0))
            s_loc = jnp.where(valid, s_loc, -1e30)
            m = jnp.maximum(m, jnp.max(s_loc, axis=1, keepdims=True))
            p_loc = jnp.exp(s_loc - m)
        p_ctx = jnp.exp(s_ctx - m)
        den = jnp.sum(p_ctx, axis=1, keepdims=True) + jnp.exp(sink - m)
        acc = _dot(p_ctx.astype(BF16), vs[-1])
        if local:
            den = den + jnp.sum(p_loc, axis=1, keepdims=True)
            acc = acc + _dot(p_loc.astype(BF16), jnp.concatenate(vs[:3], axis=0))
        out = acc / den
        for g in range(G):
            h = j * G + g
            o_ref[:, h * hd:(h + 1) * hd] = out[g * L:(g + 1) * L, :].astype(o_ref.dtype)


def _attn_kernel(sink_ref, q_ref, kp_ref, kc_ref, kn_ref, kx_ref, vp_ref, vc_ref, vn_ref, vx_ref, o_ref, *, cb, nb):
    i = pl.program_id(0)

    @pl.when(i < cb)
    def _():
        _attn_block(i, nb, sink_ref, q_ref, (kx_ref,), (vx_ref,), o_ref, False)

    @pl.when(i >= cb)
    def _():
        _attn_block(i - cb, nb, sink_ref, q_ref, (kp_ref, kc_ref, kn_ref, kx_ref),
                    (vp_ref, vc_ref, vn_ref, vx_ref), o_ref, True)


def _attention(qr, kr, P, v_block, sink, n_ctx):
    T, QW = qr.shape
    KW = kr.shape[1]
    L = ATTN_BLOCK
    cb = n_ctx // L
    nb = (T - n_ctx) // L
    lo, hi = cb, cb + nb - 1
    shifts = (lambda i: jnp.clip(i - 1, lo, hi), lambda i: jnp.clip(i, lo, hi), lambda i: jnp.clip(i + 1, lo, hi))
    kspec = lambda f: pl.BlockSpec((L, KW), lambda i: (f(i), 0))
    vspec = lambda f: pl.BlockSpec((L, KW), lambda i: (f(i), v_block))
    in_specs = ([pl.BlockSpec(memory_space=pltpu.SMEM), pl.BlockSpec((L, QW), lambda i: (i, 0))]
                + [kspec(f) for f in shifts] + [pl.BlockSpec((n_ctx, KW), lambda i: (0, 0))]
                + [vspec(f) for f in shifts] + [pl.BlockSpec((n_ctx, KW), lambda i: (0, v_block))])
    return pl.pallas_call(
        functools.partial(_attn_kernel, cb=cb, nb=nb),
        grid=(cb + nb,),
        in_specs=in_specs,
        out_specs=pl.BlockSpec((L, QW), lambda i: (i, 0)),
        out_shape=jax.ShapeDtypeStruct((T, QW), BF16),
        compiler_params=_cp(("arbitrary",)),
        name="attn",
    )(sink, qr, kr, kr, kr, kr, P, P, P, P)


def _oproj_kernel(x_ref, gt_ref, a_ref, b_ref, c_ref, wa_ref, wb_ref, wc_ref, o_ref, *, tm, n_ctx):
    acc = _dot(a_ref[...], wa_ref[...]) + _dot(b_ref[...], wb_ref[...]) + _dot(c_ref[...], wc_ref[...])
    gate = _pick_mod(gt_ref, _ctx_rows(pl.program_id(0), tm, n_ctx))
    o_ref[...] = x_ref[...] + gate * acc


def _oproj(x, mods, l, y_rw, y_gm, y_at, w_o, n_ctx, tm):
    T, D = x.shape
    tn = 512
    W1 = y_rw.shape[1]
    W3 = y_at.shape[1]
    gate_col = 2 * (D // tn)
    return pl.pallas_call(
        functools.partial(_oproj_kernel, tm=tm, n_ctx=n_ctx),
        grid=(T // tm, D // tn),
        in_specs=[pl.BlockSpec((tm, tn), lambda i, j: (i, j)),
                  pl.BlockSpec((1, 8, tn), lambda i, j: (l, 0, gate_col + j)),
                  pl.BlockSpec((tm, W1), lambda i, j: (i, 0)),
                  pl.BlockSpec((tm, W1), lambda i, j: (i, 0)),
                  pl.BlockSpec((tm, W3), lambda i, j: (i, 0)),
                  pl.BlockSpec((W1, tn), lambda i, j: (0, j)),
                  pl.BlockSpec((W1, tn), lambda i, j: (1, j)),
                  pl.BlockSpec((W3, tn), lambda i, j: (1, j))],
        out_specs=pl.BlockSpec((tm, tn), lambda i, j: (i, j)),
        out_shape=jax.ShapeDtypeStruct((T, D), F32),
        compiler_params=_cp(("arbitrary", "arbitrary")),
        name="oproj",
    )(x, mods, y_rw, y_gm, y_at, w_o, w_o, w_o)


def _router_kernel(x_ref, sh_ref, sc_ref, g_ref, rt_ref, bias_ref, tri_ref,
                   h_ref, hp_ref, idx_ref, wt_ref, rank_ref, cnt_ref, carry, *, tm, n_ctx):
    i = pl.program_id(0)

    @pl.when(i == 0)
    def _():
        carry[...] = jnp.zeros_like(carry)

    is_ctx = _ctx_rows(i, tm, n_ctx)
    h = _norm_mod(x_ref[...], g_ref[...], _pick_mod(sh_ref, is_ctx), _pick_mod(sc_ref, is_ctx))
    h_ref[...] = h.astype(h_ref.dtype)
    _store_slabs(hp_ref, _pack_halves(h), tm)
    E = N_EXPERTS
    pg = E // N_EXPERT_GROUPS
    neg = -jnp.inf
    scores = _sigmoid(_dot_nt(rt_ref[...], h, HI))
    biased = scores + bias_ref[:, 0:1]
    b3 = biased.reshape(N_EXPERT_GROUPS, pg, tm)
    i3 = lax.broadcasted_iota(jnp.int32, b3.shape, 1)
    m1 = jnp.max(b3, axis=1, keepdims=True)
    first = jnp.min(jnp.where(b3 == m1, i3, pg), axis=1, keepdims=True)
    m2 = jnp.max(jnp.where(i3 == first, neg, b3), axis=1, keepdims=True)
    gs = (m1 + m2).reshape(N_EXPERT_GROUPS, tm)
    gi = lax.broadcasted_iota(jnp.int32, gs.shape, 0)
    gsel = jnp.zeros(gs.shape, jnp.bool_)
    for _ in range(TOPK_GROUPS):
        gm = jnp.max(gs, axis=0, keepdims=True)
        gfirst = jnp.min(jnp.where(gs == gm, gi, N_EXPERT_GROUPS), axis=0, keepdims=True)
        hit = gi == gfirst
        gsel = gsel | hit
        gs = jnp.where(hit, neg, gs)
    masked = jnp.where(gsel.reshape(N_EXPERT_GROUPS, 1, tm), b3, neg).reshape(E, tm)
    ei = lax.broadcasted_iota(jnp.int32, (E, tm), 0)
    sel = jnp.zeros((E, tm), jnp.bool_)
    picks = []
    for _ in range(TOP_K):
        mx = jnp.max(masked, axis=0, keepdims=True)
        efirst = jnp.min(jnp.where(masked == mx, ei, E), axis=0, keepdims=True)
        hit = ei == efirst
        sel = sel | hit
        masked = jnp.where(hit, neg, masked)
        picks.append((efirst, hit, jnp.sum(jnp.where(hit, scores, 0.0), axis=0, keepdims=True)))
    wsum = picks[0][2]
    for pk in picks[1:]:
        wsum = wsum + pk[2]
    self_f = jnp.where(sel, 1.0, 0.0)
    rank_dense = carry[:, 0:1] + _dot(self_f.astype(BF16), tri_ref[...])
    carry[...] = carry[...] + jnp.sum(self_f, axis=1, keepdims=True)
    cnt_ref[...] = carry[...]
    for kx, (efirst, hit, wk) in enumerate(picks):
        idx_ref[kx:kx + 1, :] = efirst
        wt_ref[kx:kx + 1, :] = wk / wsum * ROUTED_SCALE
        rank_ref[kx:kx + 1, :] = jnp.sum(jnp.where(hit, rank_dense, 0.0), axis=0, keepdims=True).astype(jnp.int32)


def _router(x, mods, l, g, router_t, bias, tri, n_ctx, tm):
    T, D = x.shape
    E = router_t.shape[0]
    kspec = pl.BlockSpec((TOP_K, tm), lambda i: (0, i))
    return pl.pallas_call(
        functools.partial(_router_kernel, tm=tm, n_ctx=n_ctx),
        grid=(T // tm,),
        in_specs=[pl.BlockSpec((tm, D), lambda i: (i, 0)),
                  pl.BlockSpec((1, 8, D), lambda i: (l, 0, 3)),
                  pl.BlockSpec((1, 8, D), lambda i: (l, 0, 4)),
                  pl.BlockSpec((1, D), lambda i: (0, 0)),
                  pl.BlockSpec((E, D), lambda i: (0, 0)),
                  pl.BlockSpec((E, 128), lambda i: (0, 0)),
                  pl.BlockSpec((tm, tm), lambda i: (0, 0))],
        out_specs=[pl.BlockSpec((tm, D), lambda i: (i, 0)), pl.BlockSpec((tm * SLAB, 128), lambda i: (i, 0)),
                   kspec, kspec, kspec, pl.BlockSpec((E, 128), lambda i: (0, 0))],
        out_shape=[jax.ShapeDtypeStruct((T, D), BF16),
                   jax.ShapeDtypeStruct((T * SLAB, 128), jnp.uint32),
                   jax.ShapeDtypeStruct((TOP_K, T), jnp.int32),
                   jax.ShapeDtypeStruct((TOP_K, T), F32),
                   jax.ShapeDtypeStruct((TOP_K, T), jnp.int32),
                   jax.ShapeDtypeStruct((E, 128), F32)],
        scratch_shapes=[pltpu.VMEM((E, 128), F32)],
        compiler_params=_cp(("arbitrary",)),
        name="router",
    )(x, mods, mods, g, router_t, bias, tri)


def _gather_slabs(idx_ref, n, src_hbm, dst, sem):
    def body(r, carry):
        src = src_hbm.at[pl.ds(pl.multiple_of(idx_ref[r], SLAB), SLAB), :]
        pltpu.make_async_copy(src, dst.at[pl.ds(r * PITCH, SLAB), :], sem).start()
        return carry
    lax.fori_loop(0, n, body, 0, unroll=8)


def _issue_slabs(idx_ref, lo, hi, src_hbm, dst, sem):
    for r in range(lo, hi):
        src = src_hbm.at[pl.ds(pl.multiple_of(idx_ref[r], SLAB), SLAB), :]
        pltpu.make_async_copy(src, dst.at[pl.ds(r * PITCH, SLAB), :], sem).start(priority=r % 2)


def _wait_slabs(n, src_hbm, dst, sem):
    pltpu.make_async_copy(src_hbm.at[pl.ds(0, n * SLAB), :], dst.at[pl.ds(0, n * SLAB), :], sem).wait()


def _experts_kernel(be_ref, nu_ref, first_ref, wslot_ref, nxe_ref, hasn_ref, tokc_ref, tokn_ref, h_hbm,
                    wg_hbm, wu_hbm, wd_hbm, y_ref, hbuf, sem, wfg, wfu, wfd, wsem, wg_s, wu_s, wd_s, *, TB, l):
    b = pl.program_id(0)
    slot = b % 2
    n_used = nu_ref[0]
    SW = h_hbm.shape[1]

    def weight_copies(e, ws):
        return [pltpu.make_async_copy(src.at[l, e], dst.at[ws], wsem.at[ws])
                for src, dst in ((wg_hbm, wfg), (wu_hbm, wfu), (wd_hbm, wfd))]

    @pl.when(b == 0)
    def _():
        _gather_slabs(tokc_ref.at[0, 0], TB, h_hbm, hbuf.at[0], sem.at[0])
        for cp in weight_copies(be_ref[0], 0):
            cp.start(priority=1)

    @pl.when((first_ref[b] == 1) & (b < n_used))
    def _():
        ws = wslot_ref[b]

        @pl.when(hasn_ref[b] == 1)
        def _():
            for cp in weight_copies(nxe_ref[b], 1 - ws):
                cp.start(priority=1)

        for cp in weight_copies(be_ref[b], ws):
            cp.wait()
        wg_s[...] = wfg[ws].astype(BF16)
        wu_s[...] = wfu[ws].astype(BF16)
        wd_s[...] = wfd[ws].astype(BF16)

    @pl.when(b < n_used)
    def _():
        _wait_slabs(TB, h_hbm, hbuf.at[slot], sem.at[slot])
        nxt = (tokn_ref.at[0, 0], h_hbm, hbuf.at[1 - slot], sem.at[1 - slot])
        burst = TB // (2 * SLAB)
        issued = 0
        half = SLAB * SW
        a = u = None
        for s in range(0, SLAB, 2):
            w0 = hbuf[slot, pl.ds(s, TB, stride=PITCH), :]
            w1 = hbuf[slot, pl.ds(s + 1, TB, stride=PITCH), :]
            for unpack, base in ((_unpack_lo, 0), (_unpack_hi, half)):
                xs = jnp.concatenate([unpack(w0), unpack(w1)], axis=1).astype(BF16)
                k0 = base + s * SW
                da = _dot(xs, wg_s[k0:k0 + 2 * SW, :])
                _issue_slabs(nxt[0], issued, issued + burst, *nxt[1:])
                du = _dot(xs, wu_s[k0:k0 + 2 * SW, :])
                _issue_slabs(nxt[0], issued + burst, issued + 2 * burst, *nxt[1:])
                issued += 2 * burst
                a = da if a is None else a + da
                u = du if u is None else u + du
        act = (a * _sigmoid(a) * u).astype(BF16)
        _store_slabs(y_ref, _pack_halves(_dot(act, wd_s[...])), TB)

        @pl.when(b + 1 >= n_used)
        def _():
            _wait_slabs(TB, h_hbm, hbuf.at[1 - slot], sem.at[1 - slot])

    @pl.when(b >= n_used)
    def _():
        y_ref[...] = jnp.zeros_like(y_ref)


def _experts(h2s, tok_rows, tables, wg, wu, wd, l):
    SW = h2s.shape[1]
    D = 2 * SW * SLAB
    nblk = tables[0].shape[0]
    TB = EXPERT_TILE
    FF = wg.shape[3]
    tok3 = tok_rows.reshape(nblk, 1, TB)
    grid_spec = pltpu.PrefetchScalarGridSpec(
        num_scalar_prefetch=len(tables),
        grid=(nblk,),
        in_specs=[pl.BlockSpec((1, 1, TB), lambda b, *_: (b, 0, 0), memory_space=pltpu.SMEM),
                  pl.BlockSpec((1, 1, TB), lambda b, *_: (jnp.minimum(b + 1, nblk - 1), 0, 0), memory_space=pltpu.SMEM),
                  pl.BlockSpec(memory_space=pl.ANY), pl.BlockSpec(memory_space=pl.ANY),
                  pl.BlockSpec(memory_space=pl.ANY), pl.BlockSpec(memory_space=pl.ANY)],
        out_specs=pl.BlockSpec((TB * SLAB, SW), lambda b, *_: (b, 0)),
        scratch_shapes=[pltpu.VMEM((2, TB * PITCH, SW), jnp.uint32), pltpu.SemaphoreType.DMA((2,)),
                        pltpu.VMEM((2, D, FF), F32), pltpu.VMEM((2, D, FF), F32), pltpu.VMEM((2, FF, D), F32),
                        pltpu.SemaphoreType.DMA((2,)),
                        pltpu.VMEM((D, FF), BF16), pltpu.VMEM((D, FF), BF16), pltpu.VMEM((FF, D), BF16)],
    )
    return pl.pallas_call(
        functools.partial(_experts_kernel, TB=TB, l=l),
        grid_spec=grid_spec,
        out_shape=jax.ShapeDtypeStruct((nblk * TB * SLAB, SW), jnp.uint32),
        compiler_params=_cp(("arbitrary",)),
        name="experts",
    )(*tables, tok3, tok3, h2s, wg, wu, wd)


def _combine_kernel(slc_ref, sln_ref, y_hbm, x_ref, h_ref, wt_ref, gt_ref, swg_ref, swu_ref, swd_ref, o_ref,
                    ybuf, sem, *, tm, n_ctx, nt):
    i = pl.program_id(0)
    slot = i % 2
    SW = y_hbm.shape[1]

    @pl.when(i == 0)
    def _():
        for k in range(TOP_K):
            _gather_slabs(slc_ref.at[0, k], tm, y_hbm, ybuf.at[0, k], sem.at[0])

    hb = h_ref[...]
    a = _dot(hb, swg_ref[...])
    u = _dot(hb, swu_ref[...])
    shared = _dot((a * _sigmoid(a) * u).astype(BF16), swd_ref[...])
    gate = _pick_mod(gt_ref, _ctx_rows(i, tm, n_ctx))
    wk = [jnp.broadcast_to(wt_ref[:, k:k + 1], (tm, SW)) for k in range(TOP_K)]
    for k in range(TOP_K):
        _wait_slabs(tm, y_hbm, ybuf.at[slot, k], sem.at[slot])
    half = SLAB * SW
    burst = tm // SLAB
    for s in range(SLAB):
        lo = slice(s * SW, (s + 1) * SW)
        hi = slice(half + s * SW, half + (s + 1) * SW)
        acc_lo = shared[:, lo]
        acc_hi = shared[:, hi]
        for k in range(TOP_K):
            w = ybuf[slot, k, pl.ds(s, tm, stride=PITCH), :]
            acc_lo = acc_lo + wk[k] * _unpack_lo(w)
            acc_hi = acc_hi + wk[k] * _unpack_hi(w)
            _issue_slabs(sln_ref.at[0, k], s * burst, (s + 1) * burst, y_hbm, ybuf.at[1 - slot, k], sem.at[1 - slot])
        o_ref[:, lo] = x_ref[:, lo] + gate[:, lo] * acc_lo
        o_ref[:, hi] = x_ref[:, hi] + gate[:, hi] * acc_hi

    @pl.when(i + 1 >= nt)
    def _():
        for k in range(TOP_K):
            _wait_slabs(tm, y_hbm, ybuf.at[1 - slot, k], sem.at[1 - slot])


def _combine(y2, slot_rows, wt_t, x, h2, mods, l, swg, swu, swd, n_ctx, tm):
    T, D = x.shape
    SW = y2.shape[1]
    nt = T // tm
    FF = swg.shape[1]
    sl3 = slot_rows.reshape(TOP_K, nt, tm).transpose(1, 0, 2)
    return pl.pallas_call(
        functools.partial(_combine_kernel, tm=tm, n_ctx=n_ctx, nt=nt),
        grid=(nt,),
        in_specs=[pl.BlockSpec((1, TOP_K, tm), lambda i: (i, 0, 0), memory_space=pltpu.SMEM),
                  pl.BlockSpec((1, TOP_K, tm), lambda i: (jnp.minimum(i + 1, nt - 1), 0, 0),
                               memory_space=pltpu.SMEM),
                  pl.BlockSpec(memory_space=pl.ANY),
                  pl.BlockSpec((tm, D), lambda i: (i, 0)),
                  pl.BlockSpec((tm, D), lambda i: (i, 0)),
                  pl.BlockSpec((tm, TOP_K), lambda i: (i, 0)),
                  pl.BlockSpec((1, 8, D), lambda i: (l, 0, 5)),
                  pl.BlockSpec((D, FF), lambda i: (0, 0)),
                  pl.BlockSpec((D, FF), lambda i: (0, 0)),
                  pl.BlockSpec((FF, D), lambda i: (0, 0))],
        out_specs=pl.BlockSpec((tm, D), lambda i: (i, 0)),
        out_shape=jax.ShapeDtypeStruct((T, D), F32),
        scratch_shapes=[pltpu.VMEM((2, TOP_K, tm * PITCH, SW), jnp.uint32), pltpu.SemaphoreType.DMA((2,))],
        compiler_params=_cp(("arbitrary",)),
        name="combine",
    )(sl3, sl3, y2, x, h2, wt_t, mods, swg, swu, swd)


def _seg_indicator(width, seg):
    idx = np.arange(width) // seg
    return jnp.asarray(idx[:, None] == idx[None, :], dtype=BF16)


def _rope_tables(n_ctx, n_lat):
    rows = n_lat // GRID_W
    row = jnp.repeat(jnp.arange(rows, dtype=F32), GRID_W)
    col = jnp.tile(jnp.arange(GRID_W, dtype=F32), rows)
    n_freq = ATTN_HEAD // 4
    inv_freq = ROPE_THETA ** (-jnp.arange(n_freq, dtype=F32) / n_freq)
    ar, ac = row[:, None] * inv_freq, col[:, None] * inv_freq
    cos = jnp.concatenate([jnp.cos(ar), jnp.cos(ar), jnp.cos(ac), jnp.cos(ac)], axis=1)
    sin = jnp.concatenate([-jnp.sin(ar), jnp.sin(ar), -jnp.sin(ac), jnp.sin(ac)], axis=1)
    cos = jnp.concatenate([jnp.ones((n_ctx, ATTN_HEAD), F32), cos], axis=0)
    sin = jnp.concatenate([jnp.zeros((n_ctx, ATTN_HEAD), F32), sin], axis=0)
    return jnp.tile(cos, (1, 2)), jnp.tile(sin, (1, 2))


def kernel(x, c, ctx, c_ctx, w_ada, b_ada, norm1_g, norm2_g, w_in, w_o, rw_mu, rw_w0, rw_w2, rw_a0, rw_a2, rw_g2, rw_kk, rw_ka, rw_rk, rw_ln_g, rw_ln_b, gm_norm_g, gm_ws, gm_b, at_qn, at_kn, at_sink, moe_router, moe_bias, moe_wg, moe_wu, moe_wd, sh_wg, sh_wu, sh_wd):
    B, S, D = x.shape
    C = ctx.shape[1]
    assert B == 1
    L = w_ada.shape[0]
    T = C + S
    RW = rw_kk.shape[1]
    GW = gm_norm_g.shape[1]
    QW = at_sink.shape[1] * ATTN_HEAD
    KW = QW // ATTN_GROUP
    rw_proj = rw_mu.shape[2]
    RWP = 2048
    assert rw_proj <= RWP and T % 768 == 0 and C % 256 == 0 and S % 256 == 0 and D == 2 * SLAB * 128

    xs = jnp.concatenate([ctx[0], x[0]], axis=0)
    cond8 = jnp.zeros((8, D), F32).at[0].set(c_ctx).at[1].set(c[0])
    mods = _ada(cond8, w_ada, b_ada)

    gh_rw = _seg_indicator(RW, RWKV_HEAD)
    gh_q = _seg_indicator(QW, ATTN_HEAD)
    gh_k = _seg_indicator(KW, ATTN_HEAD)
    cos, sin = _rope_tables(C, S)
    tri = jnp.asarray(np.arange(256)[:, None] < np.arange(256)[None, :], dtype=BF16)
    TB = EXPERT_TILE
    nblk = -(-(T * TOP_K) // TB) + N_EXPERTS

    for l in range(L):
        w_in_l = w_in[l]
        w_in_p = jnp.concatenate([w_in_l[:, :rw_proj], jnp.zeros((D, RWP - rw_proj), F32), w_in_l[:, rw_proj:]],
                                 axis=1).astype(BF16)
        P = _inproj(xs, mods, l, norm1_g[l][None], w_in_p, C, 768)
        gm_block = RWP // (2 * GW)
        q_block = (RWP + 2 * GW) // QW
        k_block = (RWP + 2 * GW + QW) // KW
        v_block = k_block + 1

        pad = lambda a: jnp.pad(a, ((0, 0), (0, RWP - rw_proj)))
        prm = {"mu": pad(rw_mu[l]), "w0": rw_w0[l], "w2": rw_w2[l], "a0": rw_a0[l], "a2": rw_a2[l],
               "g2": rw_g2[l], "kk": rw_kk[l][None], "ka": rw_ka[l][None], "rk": rw_rk[l].reshape(1, RW)}
        r, v, kk, lwf, kf, bf, lwb, kb, bb, gate, bonus = _rwkv_prep(P, prm, gh_rw, C, 256)
        yf, yb = _rwkv_scan(r, v, kk, lwf, kf, bf, lwb, kb, bb, C)
        y_rw = _rwkv_post(yf, yb, bonus, gate, rw_ln_g[l][None], rw_ln_b[l][None], gh_rw, 256)

        gm_bias = jnp.repeat(gm_b[l].T, GW // GM_GROUPS, axis=1)
        y_gm = _gmlp(P, gm_block, gm_norm_g[l][None], gm_ws[l].astype(BF16), gm_bias, 256)

        qg = jnp.tile(at_qn[l], QW // ATTN_HEAD)[None]
        kg = jnp.tile(at_kn[l], KW // ATTN_HEAD)[None]
        qr, kr = _qk_prep(P, q_block, k_block, cos, sin, qg, kg, gh_q, gh_k, 256)
        y_at = _attention(qr, kr, P, v_block, at_sink[l], C)

        xs = _oproj(xs, mods, l, y_rw, y_gm, y_at, w_o[l].astype(BF16), C, 768)

        bias_col = jnp.broadcast_to(moe_bias[l][:, None], (N_EXPERTS, 128))
        h2, h2s, idx8, wt8, rank8, cnt = _router(xs, mods, l, norm2_g[l][None], moe_router[l].T, bias_col, tri, C, 256)
        counts = cnt[:, 0].astype(jnp.int32)
        padded = (counts + TB - 1) // TB * TB
        pad_end = jnp.cumsum(padded)
        pad_start = pad_end - padded
        e_ids = jnp.arange(N_EXPERTS, dtype=jnp.int32)
        slot8 = jnp.sum(jnp.where(idx8[:, :, None] == e_ids, pad_start, 0), axis=-1) + rank8
        tok = jnp.zeros((nblk * TB,), jnp.int32).at[slot8.reshape(-1)].set(
            jnp.tile(jnp.arange(T, dtype=jnp.int32), TOP_K), unique_indices=True)
        blk_pos = jnp.arange(nblk, dtype=jnp.int32) * TB
        block_e = jnp.minimum(jnp.sum((pad_end[None, :] <= blk_pos[:, None]).astype(jnp.int32), axis=1), N_EXPERTS - 1)
        n_used = (pad_end[-1] // TB).astype(jnp.int32)
        first = (blk_pos < pad_end[-1]) & jnp.concatenate([jnp.ones((1,), jnp.bool_), block_e[1:] != block_e[:-1]])
        wslot = (jnp.cumsum(first.astype(jnp.int32)) - 1) % 2
        later = jnp.where((counts[None, :] > 0) & (e_ids[None, :] > e_ids[:, None]), e_ids[None, :], N_EXPERTS)
        nxt_of = jnp.min(later, axis=1)
        nxt_blk = jnp.sum(jnp.where(block_e[:, None] == e_ids, nxt_of, 0), axis=1)
        tables = (block_e, n_used.reshape(1), first.astype(jnp.int32), wslot.astype(jnp.int32),
                  jnp.minimum(nxt_blk, N_EXPERTS - 1).astype(jnp.int32), (nxt_blk < N_EXPERTS).astype(jnp.int32))
        y2 = _experts(h2s, tok * SLAB, tables, moe_wg, moe_wu, moe_wd, l)
        xs = _combine(y2, slot8 * SLAB, wt8.T, xs, h2, mods, l, sh_wg[l].astype(BF16), sh_wu[l].astype(BF16),
                      sh_wd[l].astype(BF16), C, 128)
    return xs[C:].reshape(B, S, D)
```

```python
import functools

import jax
import jax.numpy as jnp
import numpy as np
from jax import lax
from jax.experimental import pallas as pl
from jax.experimental.pallas import tpu as pltpu

F32 = jnp.float32
BF16 = jnp.bfloat16
HI = lax.Precision.HIGHEST

NORM_EPS = 1e-6
GRID_W = 64

RWKV_HEAD = 64
DECAY_LORA = 64
ICLR_LORA = 64
GATE_LORA = 128
RWKV_GN_EPS = 64e-5

GM_GROUPS = 8
GM_CHUNK = 128

ATTN_HEAD = 64
ATTN_GROUP = 4
ATTN_WINDOW = 128
ATTN_BLOCK = 128
ROPE_THETA = 10000.0

N_EXPERTS = 64
TOP_K = 8
N_EXPERT_GROUPS = 8
TOPK_GROUPS = 4
ROUTED_SCALE = 2.5

SCAN_CHUNK = 64
EXPERT_TILE = 256
SLAB = 8
PITCH = 12
VMEM_LIMIT = 56 * 1024 * 1024


def _cp(sem, vmem=VMEM_LIMIT):
    return pltpu.CompilerParams(dimension_semantics=sem, vmem_limit_bytes=vmem)


def _dot(a, b, prec=None):
    return jnp.dot(a, b, preferred_element_type=F32, precision=prec)


def _dot_nt(a, b, prec=None):
    return lax.dot_general(a, b, (((1,), (1,)), ((), ())), preferred_element_type=F32, precision=prec)


def _dot_tn(a, b, prec=None):
    return lax.dot_general(a, b, (((0,), (0,)), ((), ())), preferred_element_type=F32, precision=prec)


def _seg_sum(x, g):
    xh = x.astype(BF16)
    xl = (x - xh.astype(F32)).astype(BF16)
    return _dot(xh, g) + _dot(xl, g)


def _sigmoid(x):
    return jax.nn.sigmoid(x)


def _pack_halves(x):
    half = x.shape[1] // 2
    lo = lax.bitcast_convert_type(x[:, :half].astype(BF16).astype(F32), jnp.uint32)
    hi = lax.bitcast_convert_type(x[:, half:].astype(BF16).astype(F32), jnp.uint32)
    return hi | (lo >> 16)


def _unpack_lo(w):
    return lax.bitcast_convert_type(w << 16, F32)


def _unpack_hi(w):
    return lax.bitcast_convert_type(w & jnp.uint32(0xFFFF0000), F32)


def _store_slabs(ref, words, n):
    for c in range(SLAB):
        ref[pl.ds(c, n, stride=SLAB), :] = words[:, c * 128:(c + 1) * 128]


def _norm_mod(x, g, sh, sc):
    y = x * lax.rsqrt(jnp.mean(x * x, axis=-1, keepdims=True) + NORM_EPS)
    return (y * g) * (1.0 + sc) + sh


def _ctx_rows(i, tm, n_ctx):
    row = i * tm + lax.broadcasted_iota(jnp.int32, (tm, 1), 0)
    return row < n_ctx


def _pick_mod(mod_ref, is_ctx):
    return jnp.where(is_ctx, mod_ref[0, 0:1, :], mod_ref[0, 1:2, :])


def _ada_kernel(c_ref, w_ref, b_ref, o_ref):
    c = c_ref[...]
    s = c * _sigmoid(c)
    o_ref[0] = _dot(s, w_ref[0], HI) + b_ref[0]


def _ada(cond8, w_ada, b_ada):
    L, D, N = w_ada.shape
    tn = 1024
    return pl.pallas_call(
        _ada_kernel,
        grid=(L, N // tn),
        in_specs=[pl.BlockSpec((8, D), lambda l, j: (0, 0)),
                  pl.BlockSpec((1, D, tn), lambda l, j: (l, 0, j)),
                  pl.BlockSpec((1, 1, tn), lambda l, j: (l, 0, j))],
        out_specs=pl.BlockSpec((1, 8, tn), lambda l, j: (l, 0, j)),
        out_shape=jax.ShapeDtypeStruct((L, 8, N), F32),
        compiler_params=_cp(("arbitrary", "arbitrary")),
        name="ada",
    )(cond8, w_ada, b_ada.reshape(L, 1, N))


def _inproj_kernel(x_ref, sh_ref, sc_ref, g_ref, w_ref, o_ref, h_scr, *, tm, n_ctx):
    i = pl.program_id(0)

    @pl.when(pl.program_id(1) == 0)
    def _():
        is_ctx = _ctx_rows(i, tm, n_ctx)
        h = _norm_mod(x_ref[...], g_ref[...], _pick_mod(sh_ref, is_ctx), _pick_mod(sc_ref, is_ctx))
        h_scr[...] = h.astype(BF16)

    o_ref[...] = _dot(h_scr[...], w_ref[...])


def _inproj(x, mods, l, g, w, n_ctx, tm):
    T, D = x.shape
    N = w.shape[1]
    tn = 512
    return pl.pallas_call(
        functools.partial(_inproj_kernel, tm=tm, n_ctx=n_ctx),
        grid=(T // tm, N // tn),
        in_specs=[pl.BlockSpec((tm, D), lambda i, j: (i, 0)),
                  pl.BlockSpec((1, 8, D), lambda i, j: (l, 0, 0)),
                  pl.BlockSpec((1, 8, D), lambda i, j: (l, 0, 1)),
                  pl.BlockSpec((1, D), lambda i, j: (0, 0)),
                  pl.BlockSpec((D, tn), lambda i, j: (0, j))],
        out_specs=pl.BlockSpec((tm, tn), lambda i, j: (i, j)),
        out_shape=jax.ShapeDtypeStruct((T, N), F32),
        scratch_shapes=[pltpu.VMEM((tm, D), BF16)],
        compiler_params=_cp(("arbitrary", "arbitrary")),
        name="inproj",
    )(x, mods, mods, g, w)


def _softplus(x):
    return jnp.maximum(x, 0.0) + jnp.log(1.0 + jnp.exp(-jnp.abs(x)))


def _rwkv_prep_kernel(p_ref, pp_ref, pn_ref, mu_ref, w0_ref, w2_ref, a0_ref, a2_ref, g2_ref,
                      kkp_ref, kap_ref, rkp_ref, gh_ref,
                      r_ref, v_ref, kk_ref, lwf_ref, kf_ref, bf_ref, lwb_ref, kb_ref, bb_ref,
                      gate_ref, bonus_ref, *, tm, n_ctx, n_tot, W):
    i = pl.program_id(0)
    p = p_ref[...]
    lrow = lax.broadcasted_iota(jnp.int32, (tm, 1), 0)
    grow = i * tm + lrow
    prev = jnp.where(lrow == 0, pp_ref[7:8, :], pltpu.roll(p, 1, axis=0))
    prev = jnp.where((grow == 0) | (grow == n_ctx), 0.0, prev)
    nxt = jnp.where(lrow == tm - 1, pn_ref[0:1, :], pltpu.roll(p, tm - 1, axis=0))
    nxt = jnp.where((grow == n_ctx - 1) | (grow == n_tot - 1), 0.0, nxt)
    ps = p + mu_ref[0:1, :] * (prev - p) + mu_ref[1:2, :] * (nxt - p)

    r = ps[:, 0:W]
    k = ps[:, W:2 * W]
    v = ps[:, 2 * W:3 * W]
    o = 3 * W
    wd = (ps[:, o:o + DECAY_LORA], ps[:, o + DECAY_LORA:o + 2 * DECAY_LORA])
    o += 2 * DECAY_LORA
    ad = (ps[:, o:o + ICLR_LORA], ps[:, o + ICLR_LORA:o + 2 * ICLR_LORA])
    o += 2 * ICLR_LORA
    gd = ps[:, o:o + GATE_LORA]

    gh = gh_ref[...]
    kk = k * kkp_ref[...]
    kk = kk / jnp.maximum(jnp.sqrt(_seg_sum(kk * kk, gh)), 1e-12)
    r_ref[...] = r
    v_ref[...] = v
    kk_ref[...] = kk
    outs = ((lwf_ref, kf_ref, bf_ref), (lwb_ref, kb_ref, bb_ref))
    for d in range(2):
        z = w0_ref[d:d + 1, :] + _dot(jnp.tanh(wd[d]), w2_ref[d], HI)
        w_log = -_softplus(-z) - 0.5
        a = _sigmoid(a0_ref[d:d + 1, :] + _dot(ad[d], a2_ref[d], HI))
        lw_ref, kd_ref, bd_ref = outs[d]
        lw_ref[...] = -jnp.exp(w_log)
        kd_ref[...] = k * (1.0 + (a - 1.0) * kap_ref[...])
        bd_ref[...] = kk * a
    gate_ref[...] = _dot(_sigmoid(gd), g2_ref[...], HI)
    bonus_ref[...] = _seg_sum(r * k * rkp_ref[...], gh) * v


def _rwkv_prep(P, prm, gh, n_ctx, tm):
    T = P.shape[0]
    W = prm["kk"].shape[1]
    PW = 2048
    nb8 = T // 8
    row = pl.BlockSpec((tm, W), lambda i: (i, 0))
    full = lambda a: pl.BlockSpec(a.shape, lambda i: (0,) * a.ndim)
    args = (prm["mu"], prm["w0"], prm["w2"], prm["a0"], prm["a2"], prm["g2"], prm["kk"], prm["ka"], prm["rk"], gh)
    return pl.pallas_call(
        functools.partial(_rwkv_prep_kernel, tm=tm, n_ctx=n_ctx, n_tot=T, W=W),
        grid=(T // tm,),
        in_specs=[pl.BlockSpec((tm, PW), lambda i: (i, 0)),
                  pl.BlockSpec((8, PW), lambda i: (jnp.maximum(i * (tm // 8) - 1, 0), 0)),
                  pl.BlockSpec((8, PW), lambda i: (jnp.minimum((i + 1) * (tm // 8), nb8 - 1), 0))]
                 + [full(a) for a in args],
        out_specs=[row] * 11,
        out_shape=[jax.ShapeDtypeStruct((T, W), F32)] * 11,
        compiler_params=_cp(("arbitrary",)),
        name="rwkv_prep",
    )(P, P, P, *args)


def _rwkv_scan_kernel(rf, vf, kkf, lwf, kf, bf, rb, vb, kkb, lwb, kb, bb, yf_ref, yb_ref, s_scr, *, C, H, N):
    @pl.when(pl.program_id(0) == 0)
    def _():
        s_scr[...] = jnp.zeros_like(s_scr)

    row = lax.broadcasted_iota(jnp.int32, (C, C), 0)
    col = lax.broadcasted_iota(jnp.int32, (C, C), 1)
    eye = (row == col).astype(F32)
    n_sq = int(np.log2(C)) - 1
    dirs = ((rf, vf, kkf, lwf, kf, bf), (rb, vb, kkb, lwb, kb, bb))
    ch = []
    for d, (r_ref, v_ref, kk_ref, lw_ref, k_ref, b_ref) in enumerate(dirs):
        incl = (col <= row) if d == 0 else (col >= row)
        strict = (col < row) if d == 0 else (col > row)
        lw = lw_ref[...]
        cum = _dot(incl.astype(F32), lw, HI)
        e_pos = jnp.exp(cum)
        e_neg = jnp.exp(-cum)
        rt = (r_ref[...] * e_pos).astype(BF16)
        at = (-kk_ref[...] * jnp.exp(cum - lw)).astype(BF16)
        bt = (b_ref[...] * e_neg).astype(BF16)
        kt = (k_ref[...] * e_neg).astype(BF16)
        vv = v_ref[...].astype(BF16)
        gam = e_pos[C - 1:C, :] if d == 0 else e_pos[0:1, :]
        for h in range(H):
            sl = slice(h * N, (h + 1) * N)
            ch.append(dict(d=d, h=h, incl=incl, strict=strict, at=at[:, sl], rt=rt[:, sl], bt=bt[:, sl],
                           kt=kt[:, sl], v=vv[:, sl], gam=gam[:, sl], s0=s_scr[d, h]))
    for c in ch:
        a_all = _dot_nt(jnp.concatenate([c["at"], c["rt"]], axis=0), jnp.concatenate([c["bt"], c["kt"]], axis=0))
        c["l_pow"] = jnp.where(c["strict"], a_all[:C, :C], 0.0)
        c["l_ak"] = jnp.where(c["strict"], a_all[:C, C:], 0.0).astype(BF16)
        c["m_rb"] = jnp.where(c["incl"], a_all[C:, :C], 0.0).astype(BF16)
        c["m_rk"] = jnp.where(c["incl"], a_all[C:, C:], 0.0).astype(BF16)
        c["t_inv"] = eye + c["l_pow"]
    for _ in range(n_sq):
        for c in ch:
            lb = c["l_pow"].astype(BF16)
            c["l_pow"] = _dot(lb, lb)
        for c in ch:
            c["t_inv"] = c["t_inv"] + _dot(c["t_inv"].astype(BF16), c["l_pow"].astype(BF16))
    for c in ch:
        c["s0b"] = c["s0"].astype(BF16)
        c["w1"] = _dot_nt(c["at"], c["s0b"]) + _dot(c["l_ak"], c["v"])
    for c in ch:
        c["u"] = _dot(c["t_inv"].astype(BF16), c["w1"].astype(BF16)).astype(BF16)
    for c in ch:
        s_scr[c["d"], c["h"]] = (c["s0"] + _dot_tn(c["u"], c["bt"]) + _dot_tn(c["v"], c["kt"])) * c["gam"]
    for c in ch:
        c["y"] = _dot_nt(c["rt"], c["s0b"]) + _dot(c["m_rb"], c["u"]) + _dot(c["m_rk"], c["v"])
    yf_ref[...] = jnp.concatenate([c["y"] for c in ch[:H]], axis=1)
    yb_ref[...] = jnp.concatenate([c["y"] for c in ch[H:]], axis=1)


def _rwkv_scan(r, v, kk, lwf, kf, bf, lwb, kb, bb, n_ctx):
    T, W = r.shape
    C = SCAN_CHUNK
    H = W // RWKV_HEAD
    nch = T // C
    cch = n_ctx // C
    fwd = pl.BlockSpec((C, W), lambda n: (n, 0))
    bwd = pl.BlockSpec((C, W), lambda n: (jnp.where(n < cch, cch - 1 - n, nch - 1 + cch - n), 0))
    return pl.pallas_call(
        functools.partial(_rwkv_scan_kernel, C=C, H=H, N=RWKV_HEAD),
        grid=(nch,),
        in_specs=[fwd] * 6 + [bwd] * 6,
        out_specs=[fwd, bwd],
        out_shape=[jax.ShapeDtypeStruct((T, W), F32)] * 2,
        scratch_shapes=[pltpu.VMEM((2, H, RWKV_HEAD, RWKV_HEAD), F32)],
        compiler_params=_cp(("arbitrary",)),
        name="rwkv_scan",
    )(r, v, kk, lwf, kf, bf, r, v, kk, lwb, kb, bb)


def _rwkv_post_kernel(yf_ref, yb_ref, bonus_ref, gate_ref, lng_ref, lnb_ref, gh_ref, o_ref):
    gh = gh_ref[...]
    y = yf_ref[...] + yb_ref[...]
    mean = _seg_sum(y, gh) * (1.0 / RWKV_HEAD)
    yc = y - mean
    var = _seg_sum(yc * yc, gh) * (1.0 / RWKV_HEAD)
    yn = yc * lax.rsqrt(var + RWKV_GN_EPS) * lng_ref[...] + lnb_ref[...]
    o_ref[...] = ((yn + bonus_ref[...]) * gate_ref[...]).astype(o_ref.dtype)


def _rwkv_post(yf, yb, bonus, gate, lng, lnb, gh, tm):
    T, W = yf.shape
    row = pl.BlockSpec((tm, W), lambda i: (i, 0))
    full = lambda a: pl.BlockSpec(a.shape, lambda i: (0,) * a.ndim)
    return pl.pallas_call(
        _rwkv_post_kernel,
        grid=(T // tm,),
        in_specs=[row] * 4 + [full(lng), full(lnb), full(gh)],
        out_specs=row,
        out_shape=jax.ShapeDtypeStruct((T, W), BF16),
        compiler_params=_cp(("arbitrary",)),
        name="rwkv_post",
    )(yf, yb, bonus, gate, lng, lnb, gh)


def _gmlp_kernel(p_ref, g_ref, ws_ref, b_ref, o_ref, *, tm, W):
    u = jax.nn.gelu(p_ref[:, 0:W])
    v = jax.nn.gelu(p_ref[:, W:2 * W])
    v = v * lax.rsqrt(jnp.mean(v * v, axis=-1, keepdims=True) + NORM_EPS) * g_ref[...]
    vb = v.astype(BF16)
    gw = W // GM_GROUPS
    for c in range(tm // GM_CHUNK):
        rows = slice(c * GM_CHUNK, (c + 1) * GM_CHUNK)
        parts = [_dot(ws_ref[g], vb[rows, g * gw:(g + 1) * gw]) for g in range(GM_GROUPS)]
        s = jnp.concatenate(parts, axis=1) + b_ref[...]
        o_ref[rows, :] = (u[rows, :] * s).astype(o_ref.dtype)


def _gmlp(P, col_block, g, ws, bias, tm):
    T = P.shape[0]
    W = g.shape[1]
    return pl.pallas_call(
        functools.partial(_gmlp_kernel, tm=tm, W=W),
        grid=(T // tm,),
        in_specs=[pl.BlockSpec((tm, 2 * W), lambda i: (i, col_block)),
                  pl.BlockSpec((1, W), lambda i: (0, 0)),
                  pl.BlockSpec(ws.shape, lambda i: (0, 0, 0)),
                  pl.BlockSpec(bias.shape, lambda i: (0, 0))],
        out_specs=pl.BlockSpec((tm, W), lambda i: (i, 0)),
        out_shape=jax.ShapeDtypeStruct((T, W), BF16),
        compiler_params=_cp(("arbitrary",)),
        name="gmlp",
    )(P, g, ws, bias)


def _rope(x, cos, sin, lane):
    w = x.shape[1]
    partner = jnp.where((lane % 32) < 16, pltpu.roll(x, w - 16, axis=1), pltpu.roll(x, 16, axis=1))
    return x * cos + partner * sin


def _qk_prep_kernel(q_ref, k_ref, cos_ref, sin_ref, qg_ref, kg_ref, ghq_ref, ghk_ref, qo_ref, ko_ref, *, scale):
    cos = cos_ref[...]
    sin = sin_ref[...]
    for x_ref, g_ref, gh_ref, o_ref, mul in ((q_ref, qg_ref, ghq_ref, qo_ref, scale), (k_ref, kg_ref, ghk_ref, ko_ref, 1.0)):
        x = x_ref[...]
        w = x.shape[1]
        ss = _seg_sum(x * x, gh_ref[...]) * (1.0 / ATTN_HEAD)
        xn = x * lax.rsqrt(ss + NORM_EPS) * g_ref[...]
        rep = w // cos.shape[1]
        lane = lax.broadcasted_iota(jnp.int32, x.shape, 1)
        xr = _rope(xn, jnp.tile(cos, (1, rep)), jnp.tile(sin, (1, rep)), lane)
        o_ref[...] = (xr * mul).astype(o_ref.dtype)


def _qk_prep(P, q_block, k_block, cos, sin, qg, kg, ghq, ghk, tm):
    T = P.shape[0]
    QW, KW = qg.shape[1], kg.shape[1]
    full = lambda a: pl.BlockSpec(a.shape, lambda i: (0,) * a.ndim)
    return pl.pallas_call(
        functools.partial(_qk_prep_kernel, scale=ATTN_HEAD ** -0.5),
        grid=(T // tm,),
        in_specs=[pl.BlockSpec((tm, QW), lambda i: (i, q_block)),
                  pl.BlockSpec((tm, KW), lambda i: (i, k_block)),
                  pl.BlockSpec((tm, cos.shape[1]), lambda i: (i, 0)),
                  pl.BlockSpec((tm, sin.shape[1]), lambda i: (i, 0)),
                  full(qg), full(kg), full(ghq), full(ghk)],
        out_specs=[pl.BlockSpec((tm, QW), lambda i: (i, 0)), pl.BlockSpec((tm, KW), lambda i: (i, 0))],
        out_shape=[jax.ShapeDtypeStruct((T, QW), BF16), jax.ShapeDtypeStruct((T, KW), BF16)],
        compiler_params=_cp(("arbitrary",)),
        name="qk_prep",
    )(P, P, cos, sin, qg, kg, ghq, ghk)


def _attn_block(i, nb, sink_ref, q_ref, k_refs, v_refs, o_ref, local):
    L = ATTN_BLOCK
    G = ATTN_GROUP
    hd = ATTN_HEAD
    n_kv = k_refs[-1].shape[1] // hd
    R = G * L
    srow = lax.broadcasted_iota(jnp.int32, (R, 1), 0)
    if local:
        qi = lax.broadcasted_iota(jnp.int32, (R, 3 * L), 0) % L
        kj = lax.broadcasted_iota(jnp.int32, (R, 3 * L), 1)
        rel = kj - L - qi
        valid = (rel <= ATTN_WINDOW) & (rel >= -ATTN_WINDOW)
        valid = valid & ((kj >= L) | (i > 0)) & ((kj < 2 * L) | (i < nb - 1))
    for j in range(n_kv):
        ks = [r[:, j * hd:(j + 1) * hd] for r in k_refs]
        vs = [r[:, j * hd:(j + 1) * hd].astype(BF16) for r in v_refs]
        q = jnp.concatenate([q_ref[:, (j * G + g) * hd:(j * G + g + 1) * hd] for g in range(G)], axis=0)
        sink = jnp.zeros((R, 1), F32)
        for g in range(G):
            sink = jnp.where((srow >= g * L) & (srow < (g + 1) * L), sink_ref[j * G + g], sink)
        s_ctx = _dot_nt(q, ks[-1])
        m = jnp.maximum(jnp.max(s_ctx, axis=1, keepdims=True), sink)
        if local:
            s_loc = _dot_nt(q, jnp.concatenate(ks[:3], axis=0))
            s_loc = jnp.where(valid, s_loc, -1e30)
            m = jnp.maximum(m, jnp.max(s_loc, axis=1, keepdims=True))
            p_loc = jnp.exp(s_loc - m)
        p_ctx = jnp.exp(s_ctx - m)
        den = jnp.sum(p_ctx, axis=1, keepdims=True) + jnp.exp(sink - m)
        acc = _dot(p_ctx.astype(BF16), vs[-1])
        if local:
            den = den + jnp.sum(p_loc, axis=1, keepdims=True)
            acc = acc + _dot(p_loc.astype(BF16), jnp.concatenate(vs[:3], axis=0))
        out = acc / den
        for g in range(G):
            h = j * G + g
            o_ref[:, h * hd:(h + 1) * hd] = out[g * L:(g + 1) * L, :].astype(o_ref.dtype)


def _attn_kernel(sink_ref, q_ref, kp_ref, kc_ref, kn_ref, kx_ref, vp_ref, vc_ref, vn_ref, vx_ref, o_ref, *, cb, nb):
    i = pl.program_id(0)

    @pl.when(i < cb)
    def _():
        _attn_block(i, nb, sink_ref, q_ref, (kx_ref,), (vx_ref,), o_ref, False)

    @pl.when(i >= cb)
    def _():
        _attn_block(i - cb, nb, sink_ref, q_ref, (kp_ref, kc_ref, kn_ref, kx_ref),
                    (vp_ref, vc_ref, vn_ref, vx_ref), o_ref, True)


def _attention(qr, kr, P, v_block, sink, n_ctx):
    T, QW = qr.shape
    KW = kr.shape[1]
    L = ATTN_BLOCK
    cb = n_ctx // L
    nb = (T - n_ctx) // L
    lo, hi = cb, cb + nb - 1
    shifts = (lambda i: jnp.clip(i - 1, lo, hi), lambda i: jnp.clip(i, lo, hi), lambda i: jnp.clip(i + 1, lo, hi))
    kspec = lambda f: pl.BlockSpec((L, KW), lambda i: (f(i), 0))
    vspec = lambda f: pl.BlockSpec((L, KW), lambda i: (f(i), v_block))
    in_specs = ([pl.BlockSpec(memory_space=pltpu.SMEM), pl.BlockSpec((L, QW), lambda i: (i, 0))]
                + [kspec(f) for f in shifts] + [pl.BlockSpec((n_ctx, KW), lambda i: (0, 0))]
                + [vspec(f) for f in shifts] + [pl.BlockSpec((n_ctx, KW), lambda i: (0, v_block))])
    return pl.pallas_call(
        functools.partial(_attn_kernel, cb=cb, nb=nb),
        grid=(cb + nb,),
        in_specs=in_specs,
        out_specs=pl.BlockSpec((L, QW), lambda i: (i, 0)),
        out_shape=jax.ShapeDtypeStruct((T, QW), BF16),
        compiler_params=_cp(("arbitrary",)),
        name="attn",
    )(sink, qr, kr, kr, kr, kr, P, P, P, P)


def _oproj_kernel(x_ref, gt_ref, a_ref, b_ref, c_ref, wa_ref, wb_ref, wc_ref, o_ref, *, tm, n_ctx):
    acc = _dot(a_ref[...], wa_ref[...]) + _dot(b_ref[...], wb_ref[...]) + _dot(c_ref[...], wc_ref[...])
    gate = _pick_mod(gt_ref, _ctx_rows(pl.program_id(0), tm, n_ctx))
    o_ref[...] = x_ref[...] + gate * acc


def _oproj(x, mods, l, y_rw, y_gm, y_at, w_o, n_ctx, tm):
    T, D = x.shape
    tn = 512
    W1 = y_rw.shape[1]
    W3 = y_at.shape[1]
    gate_col = 2 * (D // tn)
    return pl.pallas_call(
        functools.partial(_oproj_kernel, tm=tm, n_ctx=n_ctx),
        grid=(T // tm, D // tn),
        in_specs=[pl.BlockSpec((tm, tn), lambda i, j: (i, j)),
                  pl.BlockSpec((1, 8, tn), lambda i, j: (l, 0, gate_col + j)),
                  pl.BlockSpec((tm, W1), lambda i, j: (i, 0)),
                  pl.BlockSpec((tm, W1), lambda i, j: (i, 0)),
                  pl.BlockSpec((tm, W3), lambda i, j: (i, 0)),
                  pl.BlockSpec((W1, tn), lambda i, j: (0, j)),
                  pl.BlockSpec((W1, tn), lambda i, j: (1, j)),
                  pl.BlockSpec((W3, tn), lambda i, j: (1, j))],
        out_specs=pl.BlockSpec((tm, tn), lambda i, j: (i, j)),
        out_shape=jax.ShapeDtypeStruct((T, D), F32),
        compiler_params=_cp(("arbitrary", "arbitrary")),
        name="oproj",
    )(x, mods, y_rw, y_gm, y_at, w_o, w_o, w_o)


def _router_kernel(x_ref, sh_ref, sc_ref, g_ref, rt_ref, bias_ref, tri_ref,
                   h_ref, hp_ref, idx_ref, wt_ref, rank_ref, cnt_ref, carry, *, tm, n_ctx):
    i = pl.program_id(0)

    @pl.when(i == 0)
    def _():
        carry[...] = jnp.zeros_like(carry)

    is_ctx = _ctx_rows(i, tm, n_ctx)
    h = _norm_mod(x_ref[...], g_ref[...], _pick_mod(sh_ref, is_ctx), _pick_mod(sc_ref, is_ctx))
    h_ref[...] = h.astype(h_ref.dtype)
    _store_slabs(hp_ref, _pack_halves(h), tm)
    E = N_EXPERTS
    pg = E // N_EXPERT_GROUPS
    neg = -jnp.inf
    scores = _sigmoid(_dot_nt(rt_ref[...], h, HI))
    biased = scores + bias_ref[:, 0:1]
    b3 = biased.reshape(N_EXPERT_GROUPS, pg, tm)
    i3 = lax.broadcasted_iota(jnp.int32, b3.shape, 1)
    m1 = jnp.max(b3, axis=1, keepdims=True)
    first = jnp.min(jnp.where(b3 == m1, i3, pg), axis=1, keepdims=True)
    m2 = jnp.max(jnp.where(i3 == first, neg, b3), axis=1, keepdims=True)
    gs = (m1 + m2).reshape(N_EXPERT_GROUPS, tm)
    gi = lax.broadcasted_iota(jnp.int32, gs.shape, 0)
    gsel = jnp.zeros(gs.shape, jnp.bool_)
    for _ in range(TOPK_GROUPS):
        gm = jnp.max(gs, axis=0, keepdims=True)
        gfirst = jnp.min(jnp.where(gs == gm, gi, N_EXPERT_GROUPS), axis=0, keepdims=True)
        hit = gi == gfirst
        gsel = gsel | hit
        gs = jnp.where(hit, neg, gs)
    masked = jnp.where(gsel.reshape(N_EXPERT_GROUPS, 1, tm), b3, neg).reshape(E, tm)
    ei = lax.broadcasted_iota(jnp.int32, (E, tm), 0)
    sel = jnp.zeros((E, tm), jnp.bool_)
    picks = []
    for _ in range(TOP_K):
        mx = jnp.max(masked, axis=0, keepdims=True)
        efirst = jnp.min(jnp.where(masked == mx, ei, E), axis=0, keepdims=True)
        hit = ei == efirst
        sel = sel | hit
        masked = jnp.where(hit, neg, masked)
        picks.append((efirst, hit, jnp.sum(jnp.where(hit, scores, 0.0), axis=0, keepdims=True)))
    wsum = picks[0][2]
    for pk in picks[1:]:
        wsum = wsum + pk[2]
    self_f = jnp.where(sel, 1.0, 0.0)
    rank_dense = carry[:, 0:1] + _dot(self_f.astype(BF16), tri_ref[...])
    carry[...] = carry[...] + jnp.sum(self_f, axis=1, keepdims=True)
    cnt_ref[...] = carry[...]
    for kx, (efirst, hit, wk) in enumerate(picks):
        idx_ref[kx:kx + 1, :] = efirst
        wt_ref[kx:kx + 1, :] = wk / wsum * ROUTED_SCALE
        rank_ref[kx:kx + 1, :] = jnp.sum(jnp.where(hit, rank_dense, 0.0), axis=0, keepdims=True).astype(jnp.int32)


def _router(x, mods, l, g, router_t, bias, tri, n_ctx, tm):
    T, D = x.shape
    E = router_t.shape[0]
    kspec = pl.BlockSpec((TOP_K, tm), lambda i: (0, i))
    return pl.pallas_call(
        functools.partial(_router_kernel, tm=tm, n_ctx=n_ctx),
        grid=(T // tm,),
        in_specs=[pl.BlockSpec((tm, D), lambda i: (i, 0)),
                  pl.BlockSpec((1, 8, D), lambda i: (l, 0, 3)),
                  pl.BlockSpec((1, 8, D), lambda i: (l, 0, 4)),
                  pl.BlockSpec((1, D), lambda i: (0, 0)),
                  pl.BlockSpec((E, D), lambda i: (0, 0)),
                  pl.BlockSpec((E, 128), lambda i: (0, 0)),
                  pl.BlockSpec((tm, tm), lambda i: (0, 0))],
        out_specs=[pl.BlockSpec((tm, D), lambda i: (i, 0)), pl.BlockSpec((tm * SLAB, 128), lambda i: (i, 0)),
                   kspec, kspec, kspec, pl.BlockSpec((E, 128), lambda i: (0, 0))],
        out_shape=[jax.ShapeDtypeStruct((T, D), BF16),
                   jax.ShapeDtypeStruct((T * SLAB, 128), jnp.uint32),
                   jax.ShapeDtypeStruct((TOP_K, T), jnp.int32),
                   jax.ShapeDtypeStruct((TOP_K, T), F32),
                   jax.ShapeDtypeStruct((TOP_K, T), jnp.int32),
                   jax.ShapeDtypeStruct((E, 128), F32)],
        scratch_shapes=[pltpu.VMEM((E, 128), F32)],
        compiler_params=_cp(("arbitrary",)),
        name="router",
    )(x, mods, mods, g, router_t, bias, tri)


def _gather_slabs(idx_ref, n, src_hbm, dst, sem):
    def body(r, carry):
        src = src_hbm.at[pl.ds(pl.multiple_of(idx_ref[r], SLAB), SLAB), :]
        pltpu.make_async_copy(src, dst.at[pl.ds(r * PITCH, SLAB), :], sem).start()
        return carry
    lax.fori_loop(0, n, body, 0, unroll=8)


def _issue_slabs(idx_ref, lo, hi, src_hbm, dst, sem):
    for r in range(lo, hi):
        src = src_hbm.at[pl.ds(pl.multiple_of(idx_ref[r], SLAB), SLAB), :]
        pltpu.make_async_copy(src, dst.at[pl.ds(r * PITCH, SLAB), :], sem).start(priority=r % 2)


def _wait_slabs(n, src_hbm, dst, sem):
    pltpu.make_async_copy(src_hbm.at[pl.ds(0, n * SLAB), :], dst.at[pl.ds(0, n * SLAB), :], sem).wait()


def _experts_kernel(be_ref, nu_ref, first_ref, wslot_ref, nxe_ref, hasn_ref, tokc_ref, tokn_ref, h_hbm,
                    wg_hbm, wu_hbm, wd_hbm, y_ref, hbuf, sem, wfg, wfu, wfd, wsem, *, TB, l):
    b = pl.program_id(0)
    slot = b % 2
    n_used = nu_ref[0]
    SW = h_hbm.shape[1]

    def weight_copies(e, ws):
        return [pltpu.make_async_copy(src.at[l, e], dst.at[ws], wsem.at[ws])
                for src, dst in ((wg_hbm, wfg), (wu_hbm, wfu), (wd_hbm, wfd))]

    @pl.when(b == 0)
    def _():
        _gather_slabs(tokc_ref.at[0, 0], TB, h_hbm, hbuf.at[0], sem.at[0])
        for cp in weight_copies(be_ref[0], 0):
            cp.start(priority=1)

    @pl.when((first_ref[b] == 1) & (b < n_used))
    def _():
        ws = wslot_ref[b]

        @pl.when(hasn_ref[b] == 1)
        def _():
            for cp in weight_copies(nxe_ref[b], 1 - ws):
                cp.start(priority=1)

        for cp in weight_copies(be_ref[b], ws):
            cp.wait()

    @pl.when(b < n_used)
    def _():
        _wait_slabs(TB, h_hbm, hbuf.at[slot], sem.at[slot])
        ws = wslot_ref[b]
        nxt = (tokn_ref.at[0, 0], h_hbm, hbuf.at[1 - slot], sem.at[1 - slot])
        burst = TB // (2 * SLAB)
        issued = 0
        half = SLAB * SW
        a = u = None
        for s in range(0, SLAB, 2):
            w0 = hbuf[slot, pl.ds(s, TB, stride=PITCH), :]
            w1 = hbuf[slot, pl.ds(s + 1, TB, stride=PITCH), :]
            for unpack, base in ((_unpack_lo, 0), (_unpack_hi, half)):
                xs = jnp.concatenate([unpack(w0), unpack(w1)], axis=1).astype(BF16)
                k0 = base + s * SW
                da = _dot(xs, wfg[ws, k0:k0 + 2 * SW, :].astype(BF16))
                _issue_slabs(nxt[0], issued, issued + burst, *nxt[1:])
                du = _dot(xs, wfu[ws, k0:k0 + 2 * SW, :].astype(BF16))
                _issue_slabs(nxt[0], issued + burst, issued + 2 * burst, *nxt[1:])
                issued += 2 * burst
                a = da if a is None else a + da
                u = du if u is None else u + du
        act = (a * _sigmoid(a) * u).astype(BF16)
        _store_slabs(y_ref, _pack_halves(_dot(act, wfd[ws].astype(BF16))), TB)

        @pl.when(b + 1 >= n_used)
        def _():
            _wait_slabs(TB, h_hbm, hbuf.at[1 - slot], sem.at[1 - slot])

    @pl.when(b >= n_used)
    def _():
        y_ref[...] = jnp.zeros_like(y_ref)


def _experts(h2s, tok_rows, tables, wg, wu, wd, l):
    SW = h2s.shape[1]
    D = 2 * SW * SLAB
    nblk = tables[0].shape[0]
    TB = EXPERT_TILE
    FF = wg.shape[3]
    tok3 = tok_rows.reshape(nblk, 1, TB)
    grid_spec = pltpu.PrefetchScalarGridSpec(
        num_scalar_prefetch=len(tables),
        grid=(nblk,),
        in_specs=[pl.BlockSpec((1, 1, TB), lambda b, *_: (b, 0, 0), memory_space=pltpu.SMEM),
                  pl.BlockSpec((1, 1, TB), lambda b, *_: (jnp.minimum(b + 1, nblk - 1), 0, 0), memory_space=pltpu.SMEM),
                  pl.BlockSpec(memory_space=pl.ANY), pl.BlockSpec(memory_space=pl.ANY),
                  pl.BlockSpec(memory_space=pl.ANY), pl.BlockSpec(memory_space=pl.ANY)],
        out_specs=pl.BlockSpec((TB * SLAB, SW), lambda b, *_: (b, 0)),
        scratch_shapes=[pltpu.VMEM((2, TB * PITCH, SW), jnp.uint32), pltpu.SemaphoreType.DMA((2,)),
                        pltpu.VMEM((2, D, FF), F32), pltpu.VMEM((2, D, FF), F32), pltpu.VMEM((2, FF, D), F32),
                        pltpu.SemaphoreType.DMA((2,))],
    )
    return pl.pallas_call(
        functools.partial(_experts_kernel, TB=TB, l=l),
        grid_spec=grid_spec,
        out_shape=jax.ShapeDtypeStruct((nblk * TB * SLAB, SW), jnp.uint32),
        compiler_params=_cp(("arbitrary",)),
        name="experts",
    )(*tables, tok3, tok3, h2s, wg, wu, wd)


def _combine_kernel(slc_ref, sln_ref, y_hbm, x_ref, h_ref, wt_ref, gt_ref, swg_ref, swu_ref, swd_ref, o_ref,
                    ybuf, sem, *, tm, n_ctx, nt):
    i = pl.program_id(0)
    slot = i % 2
    SW = y_hbm.shape[1]

    @pl.when(i == 0)
    def _():
        for k in range(TOP_K):
            _gather_slabs(slc_ref.at[0, k], tm, y_hbm, ybuf.at[0, k], sem.at[0])

    hb = h_ref[...]
    a = _dot(hb, swg_ref[...])
    u = _dot(hb, swu_ref[...])
    shared = _dot((a * _sigmoid(a) * u).astype(BF16), swd_ref[...])
    gate = _pick_mod(gt_ref, _ctx_rows(i, tm, n_ctx))
    wk = [jnp.broadcast_to(wt_ref[:, k:k + 1], (tm, SW)) for k in range(TOP_K)]
    for k in range(TOP_K):
        _wait_slabs(tm, y_hbm, ybuf.at[slot, k], sem.at[slot])
    half = SLAB * SW
    burst = tm // SLAB
    for s in range(SLAB):
        lo = slice(s * SW, (s + 1) * SW)
        hi = slice(half + s * SW, half + (s + 1) * SW)
        acc_lo = shared[:, lo]
        acc_hi = shared[:, hi]
        for k in range(TOP_K):
            w = ybuf[slot, k, pl.ds(s, tm, stride=PITCH), :]
            acc_lo = acc_lo + wk[k] * _unpack_lo(w)
            acc_hi = acc_hi + wk[k] * _unpack_hi(w)
            _issue_slabs(sln_ref.at[0, k], s * burst, (s + 1) * burst, y_hbm, ybuf.at[1 - slot, k], sem.at[1 - slot])
        o_ref[:, lo] = x_ref[:, lo] + gate[:, lo] * acc_lo
        o_ref[:, hi] = x_ref[:, hi] + gate[:, hi] * acc_hi

    @pl.when(i + 1 >= nt)
    def _():
        for k in range(TOP_K):
            _wait_slabs(tm, y_hbm, ybuf.at[1 - slot, k], sem.at[1 - slot])


def _combine(y2, slot_rows, wt_t, x, h2, mods, l, swg, swu, swd, n_ctx, tm):
    T, D = x.shape
    SW = y2.shape[1]
    nt = T // tm
    FF = swg.shape[1]
    sl3 = slot_rows.reshape(TOP_K, nt, tm).transpose(1, 0, 2)
    return pl.pallas_call(
        functools.partial(_combine_kernel, tm=tm, n_ctx=n_ctx, nt=nt),
        grid=(nt,),
        in_specs=[pl.BlockSpec((1, TOP_K, tm), lambda i: (i, 0, 0), memory_space=pltpu.SMEM),
                  pl.BlockSpec((1, TOP_K, tm), lambda i: (jnp.minimum(i + 1, nt - 1), 0, 0),
                               memory_space=pltpu.SMEM),
                  pl.BlockSpec(memory_space=pl.ANY),
                  pl.BlockSpec((tm, D), lambda i: (i, 0)),
                  pl.BlockSpec((tm, D), lambda i: (i, 0)),
                  pl.BlockSpec((tm, TOP_K), lambda i: (i, 0)),
                  pl.BlockSpec((1, 8, D), lambda i: (l, 0, 5)),
                  pl.BlockSpec((D, FF), lambda i: (0, 0)),
                  pl.BlockSpec((D, FF), lambda i: (0, 0)),
                  pl.BlockSpec((FF, D), lambda i: (0, 0))],
        out_specs=pl.BlockSpec((tm, D), lambda i: (i, 0)),
        out_shape=jax.ShapeDtypeStruct((T, D), F32),
        scratch_shapes=[pltpu.VMEM((2, TOP_K, tm * PITCH, SW), jnp.uint32), pltpu.SemaphoreType.DMA((2,))],
        compiler_params=_cp(("arbitrary",)),
        name="combine",
    )(sl3, sl3, y2, x, h2, wt_t, mods, swg, swu, swd)


def _seg_indicator(width, seg):
    idx = np.arange(width) // seg
    return jnp.asarray(idx[:, None] == idx[None, :], dtype=BF16)


def _rope_tables(n_ctx, n_lat):
    rows = n_lat // GRID_W
    row = jnp.repeat(jnp.arange(rows, dtype=F32), GRID_W)
    col = jnp.tile(jnp.arange(GRID_W, dtype=F32), rows)
    n_freq = ATTN_HEAD // 4
    inv_freq = ROPE_THETA ** (-jnp.arange(n_freq, dtype=F32) / n_freq)
    ar, ac = row[:, None] * inv_freq, col[:, None] * inv_freq
    cos = jnp.concatenate([jnp.cos(ar), jnp.cos(ar), jnp.cos(ac), jnp.cos(ac)], axis=1)
    sin = jnp.concatenate([-jnp.sin(ar), jnp.sin(ar), -jnp.sin(ac), jnp.sin(ac)], axis=1)
    cos = jnp.concatenate([jnp.ones((n_ctx, ATTN_HEAD), F32), cos], axis=0)
    sin = jnp.concatenate([jnp.zeros((n_ctx, ATTN_HEAD), F32), sin], axis=0)
    return jnp.tile(cos, (1, 2)), jnp.tile(sin, (1, 2))


def kernel(x, c, ctx, c_ctx, w_ada, b_ada, norm1_g, norm2_g, w_in, w_o, rw_mu, rw_w0, rw_w2, rw_a0, rw_a2, rw_g2, rw_kk, rw_ka, rw_rk, rw_ln_g, rw_ln_b, gm_norm_g, gm_ws, gm_b, at_qn, at_kn, at_sink, moe_router, moe_bias, moe_wg, moe_wu, moe_wd, sh_wg, sh_wu, sh_wd):
    B, S, D = x.shape
    C = ctx.shape[1]
    assert B == 1
    L = w_ada.shape[0]
    T = C + S
    RW = rw_kk.shape[1]
    GW = gm_norm_g.shape[1]
    QW = at_sink.shape[1] * ATTN_HEAD
    KW = QW // ATTN_GROUP
    rw_proj = rw_mu.shape[2]
    RWP = 2048
    assert rw_proj <= RWP and T % 768 == 0 and C % 256 == 0 and S % 256 == 0 and D == 2 * SLAB * 128

    xs = jnp.concatenate([ctx[0], x[0]], axis=0)
    cond8 = jnp.zeros((8, D), F32).at[0].set(c_ctx).at[1].set(c[0])
    mods = _ada(cond8, w_ada, b_ada)

    gh_rw = _seg_indicator(RW, RWKV_HEAD)
    gh_q = _seg_indicator(QW, ATTN_HEAD)
    gh_k = _seg_indicator(KW, ATTN_HEAD)
    cos, sin = _rope_tables(C, S)
    tri = jnp.asarray(np.arange(256)[:, None] < np.arange(256)[None, :], dtype=BF16)
    TB = EXPERT_TILE
    nblk = -(-(T * TOP_K) // TB) + N_EXPERTS

    for l in range(L):
        w_in_l = w_in[l]
        w_in_p = jnp.concatenate([w_in_l[:, :rw_proj], jnp.zeros((D, RWP - rw_proj), F32), w_in_l[:, rw_proj:]],
                                 axis=1).astype(BF16)
        P = _inproj(xs, mods, l, norm1_g[l][None], w_in_p, C, 768)
        gm_block = RWP // (2 * GW)
        q_block = (RWP + 2 * GW) // QW
        k_block = (RWP + 2 * GW + QW) // KW
        v_block = k_block + 1

        pad = lambda a: jnp.pad(a, ((0, 0), (0, RWP - rw_proj)))
        prm = {"mu": pad(rw_mu[l]), "w0": rw_w0[l], "w2": rw_w2[l], "a0": rw_a0[l], "a2": rw_a2[l],
               "g2": rw_g2[l], "kk": rw_kk[l][None], "ka": rw_ka[l][None], "rk": rw_rk[l].reshape(1, RW)}
        r, v, kk, lwf, kf, bf, lwb, kb, bb, gate, bonus = _rwkv_prep(P, prm, gh_rw, C, 256)
        yf, yb = _rwkv_scan(r, v, kk, lwf, kf, bf, lwb, kb, bb, C)
        y_rw = _rwkv_post(yf, yb, bonus, gate, rw_ln_g[l][None], rw_ln_b[l][None], gh_rw, 256)

        gm_bias = jnp.repeat(gm_b[l].T, GW // GM_GROUPS, axis=1)
        y_gm = _gmlp(P, gm_block, gm_norm_g[l][None], gm_ws[l].astype(BF16), gm_bias, 256)

        qg = jnp.tile(at_qn[l], QW // ATTN_HEAD)[None]
        kg = jnp.tile(at_kn[l], KW // ATTN_HEAD)[None]
        qr, kr = _qk_prep(P, q_block, k_block, cos, sin, qg, kg, gh_q, gh_k, 256)
        y_at = _attention(qr, kr, P, v_block, at_sink[l], C)

        xs = _oproj(xs, mods, l, y_rw, y_gm, y_at, w_o[l].astype(BF16), C, 768)

        bias_col = jnp.broadcast_to(moe_bias[l][:, None], (N_EXPERTS, 128))
        h2, h2s, idx8, wt8, rank8, cnt = _router(xs, mods, l, norm2_g[l][None], moe_router[l].T, bias_col, tri, C, 256)
        counts = cnt[:, 0].astype(jnp.int32)
        padded = (counts + TB - 1) // TB * TB
        pad_end = jnp.cumsum(padded)
        pad_start = pad_end - padded
        e_ids = jnp.arange(N_EXPERTS, dtype=jnp.int32)
        slot8 = jnp.sum(jnp.where(idx8[:, :, None] == e_ids, pad_start, 0), axis=-1) + rank8
        tok = jnp.zeros((nblk * TB,), jnp.int32).at[slot8.reshape(-1)].set(
            jnp.tile(jnp.arange(T, dtype=jnp.int32), TOP_K), unique_indices=True)
        blk_pos = jnp.arange(nblk, dtype=jnp.int32) * TB
        block_e = jnp.minimum(jnp.sum((pad_end[None, :] <= blk_pos[:, None]).astype(jnp.int32), axis=1), N_EXPERTS - 1)
        n_used = (pad_end[-1] // TB).astype(jnp.int32)
        first = (blk_pos < pad_end[-1]) & jnp.concatenate([jnp.ones((1,), jnp.bool_), block_e[1:] != block_e[:-1]])
        wslot = (jnp.cumsum(first.astype(jnp.int32)) - 1) % 2
        later = jnp.where((counts[None, :] > 0) & (e_ids[None, :] > e_ids[:, None]), e_ids[None, :], N_EXPERTS)
        nxt_of = jnp.min(later, axis=1)
        nxt_blk = jnp.sum(jnp.where(block_e[:, None] == e_ids, nxt_of, 0), axis=1)
        tables = (block_e, n_used.reshape(1), first.astype(jnp.int32), wslot.astype(jnp.int32),
                  jnp.minimum(nxt_blk, N_EXPERTS - 1).astype(jnp.int32), (nxt_blk < N_EXPERTS).astype(jnp.int32))
        y2 = _experts(h2s, tok * SLAB, tables, moe_wg, moe_wu, moe_wd, l)
        xs = _combine(y2, slot8 * SLAB, wt8.T, xs, h2, mods, l, sh_wg[l].astype(BF16), sh_wu[l].astype(BF16),
                      sh_wd[l].astype(BF16), C, 128)
    return xs[C:].reshape(B, S, D)
```

```python
import functools

import jax
import jax.numpy as jnp
import numpy as np
from jax import lax
from jax.experimental import pallas as pl
from jax.experimental.pallas import tpu as pltpu

F32 = jnp.float32
BF16 = jnp.bfloat16
HI = lax.Precision.HIGHEST

NORM_EPS = 1e-6
GRID_W = 64

RWKV_HEAD = 64
DECAY_LORA = 64
ICLR_LORA = 64
GATE_LORA = 128
RWKV_GN_EPS = 64e-5

GM_GROUPS = 8
GM_CHUNK = 128

ATTN_HEAD = 64
ATTN_GROUP = 4
ATTN_WINDOW = 128
ATTN_BLOCK = 128
ROPE_THETA = 10000.0

N_EXPERTS = 64
TOP_K = 8
N_EXPERT_GROUPS = 8
TOPK_GROUPS = 4
ROUTED_SCALE = 2.5

SCAN_CHUNK = 64
EXPERT_TILE = 256
SLAB = 8
GATHER_SLOTS = 6
WEIGHT_CHUNKS = 4
PITCH = 12
VMEM_LIMIT = 56 * 1024 * 1024


def _cp(sem, vmem=VMEM_LIMIT):
    return pltpu.CompilerParams(dimension_semantics=sem, vmem_limit_bytes=vmem)


def _dot(a, b, prec=None):
    return jnp.dot(a, b, preferred_element_type=F32, precision=prec)


def _dot_nt(a, b, prec=None):
    return lax.dot_general(a, b, (((1,), (1,)), ((), ())), preferred_element_type=F32, precision=prec)


def _dot_tn(a, b, prec=None):
    return lax.dot_general(a, b, (((0,), (0,)), ((), ())), preferred_element_type=F32, precision=prec)


def _seg_sum(x, g):
    xh = x.astype(BF16)
    xl = (x - xh.astype(F32)).astype(BF16)
    return _dot(xh, g) + _dot(xl, g)


def _sigmoid(x):
    return jax.nn.sigmoid(x)


def _pack_halves(x):
    half = x.shape[1] // 2
    lo = lax.bitcast_convert_type(x[:, :half].astype(BF16).astype(F32), jnp.uint32)
    hi = lax.bitcast_convert_type(x[:, half:].astype(BF16).astype(F32), jnp.uint32)
    return hi | (lo >> 16)


def _unpack_lo(w):
    return lax.bitcast_convert_type(w << 16, F32)


def _unpack_hi(w):
    return lax.bitcast_convert_type(w & jnp.uint32(0xFFFF0000), F32)


def _store_slabs(ref, words, n):
    for c in range(SLAB):
        ref[pl.ds(c, n, stride=SLAB), :] = words[:, c * 128:(c + 1) * 128]


def _norm_mod(x, g, sh, sc):
    y = x * lax.rsqrt(jnp.mean(x * x, axis=-1, keepdims=True) + NORM_EPS)
    return (y * g) * (1.0 + sc) + sh


def _ctx_rows(i, tm, n_ctx):
    row = i * tm + lax.broadcasted_iota(jnp.int32, (tm, 1), 0)
    return row < n_ctx


def _pick_mod(mod_ref, is_ctx):
    return jnp.where(is_ctx, mod_ref[0, 0:1, :], mod_ref[0, 1:2, :])


def _ada_kernel(c_ref, w_ref, b_ref, o_ref):
    c = c_ref[...]
    s = c * _sigmoid(c)
    o_ref[0] = _dot(s, w_ref[0], HI) + b_ref[0]


def _ada(cond8, w_ada, b_ada):
    L, D, N = w_ada.shape
    tn = 1024
    return pl.pallas_call(
        _ada_kernel,
        grid=(L, N // tn),
        in_specs=[pl.BlockSpec((8, D), lambda l, j: (0, 0)),
                  pl.BlockSpec((1, D, tn), lambda l, j: (l, 0, j)),
                  pl.BlockSpec((1, 1, tn), lambda l, j: (l, 0, j))],
        out_specs=pl.BlockSpec((1, 8, tn), lambda l, j: (l, 0, j)),
        out_shape=jax.ShapeDtypeStruct((L, 8, N), F32),
        compiler_params=_cp(("arbitrary", "arbitrary")),
        name="ada",
    )(cond8, w_ada, b_ada.reshape(L, 1, N))


def _inproj_kernel(x_ref, sh_ref, sc_ref, g_ref, w_ref, o_ref, h_scr, *, tm, n_ctx):
    i = pl.program_id(0)

    @pl.when(pl.program_id(1) == 0)
    def _():
        is_ctx = _ctx_rows(i, tm, n_ctx)
        h = _norm_mod(x_ref[...], g_ref[...], _pick_mod(sh_ref, is_ctx), _pick_mod(sc_ref, is_ctx))
        h_scr[...] = h.astype(BF16)

    o_ref[...] = _dot(h_scr[...], w_ref[...])


def _inproj(x, mods, l, g, w, n_ctx, tm):
    T, D = x.shape
    N = w.shape[1]
    tn = 512
    return pl.pallas_call(
        functools.partial(_inproj_kernel, tm=tm, n_ctx=n_ctx),
        grid=(T // tm, N // tn),
        in_specs=[pl.BlockSpec((tm, D), lambda i, j: (i, 0)),
                  pl.BlockSpec((1, 8, D), lambda i, j: (l, 0, 0)),
                  pl.BlockSpec((1, 8, D), lambda i, j: (l, 0, 1)),
                  pl.BlockSpec((1, D), lambda i, j: (0, 0)),
                  pl.BlockSpec((D, tn), lambda i, j: (0, j))],
        out_specs=pl.BlockSpec((tm, tn), lambda i, j: (i, j)),
        out_shape=jax.ShapeDtypeStruct((T, N), F32),
        scratch_shapes=[pltpu.VMEM((tm, D), BF16)],
        compiler_params=_cp(("arbitrary", "arbitrary")),
        name="inproj",
    )(x, mods, mods, g, w)


def _softplus(x):
    return jnp.maximum(x, 0.0) + jnp.log(1.0 + jnp.exp(-jnp.abs(x)))


def _rwkv_prep_kernel(p_ref, pp_ref, pn_ref, mu_ref, w0_ref, w2_ref, a0_ref, a2_ref, g2_ref,
                      kkp_ref, kap_ref, rkp_ref, gh_ref,
                      r_ref, v_ref, kk_ref, lwf_ref, kf_ref, bf_ref, lwb_ref, kb_ref, bb_ref,
                      gate_ref, bonus_ref, *, tm, n_ctx, n_tot, W):
    i = pl.program_id(0)
    p = p_ref[...]
    lrow = lax.broadcasted_iota(jnp.int32, (tm, 1), 0)
    grow = i * tm + lrow
    prev = jnp.where(lrow == 0, pp_ref[7:8, :], pltpu.roll(p, 1, axis=0))
    prev = jnp.where((grow == 0) | (grow == n_ctx), 0.0, prev)
    nxt = jnp.where(lrow == tm - 1, pn_ref[0:1, :], pltpu.roll(p, tm - 1, axis=0))
    nxt = jnp.where((grow == n_ctx - 1) | (grow == n_tot - 1), 0.0, nxt)
    ps = p + mu_ref[0:1, :] * (prev - p) + mu_ref[1:2, :] * (nxt - p)

    r = ps[:, 0:W]
    k = ps[:, W:2 * W]
    v = ps[:, 2 * W:3 * W]
    o = 3 * W
    wd = (ps[:, o:o + DECAY_LORA], ps[:, o + DECAY_LORA:o + 2 * DECAY_LORA])
    o += 2 * DECAY_LORA
    ad = (ps[:, o:o + ICLR_LORA], ps[:, o + ICLR_LORA:o + 2 * ICLR_LORA])
    o += 2 * ICLR_LORA
    gd = ps[:, o:o + GATE_LORA]

    gh = gh_ref[...]
    kk = k * kkp_ref[...]
    kk = kk / jnp.maximum(jnp.sqrt(_seg_sum(kk * kk, gh)), 1e-12)
    r_ref[...] = r
    v_ref[...] = v
    kk_ref[...] = kk
    outs = ((lwf_ref, kf_ref, bf_ref), (lwb_ref, kb_ref, bb_ref))
    for d in range(2):
        z = w0_ref[d:d + 1, :] + _dot(jnp.tanh(wd[d]), w2_ref[d], HI)
        w_log = -_softplus(-z) - 0.5
        a = _sigmoid(a0_ref[d:d + 1, :] + _dot(ad[d], a2_ref[d], HI))
        lw_ref, kd_ref, bd_ref = outs[d]
        lw_ref[...] = -jnp.exp(w_log)
        kd_ref[...] = k * (1.0 + (a - 1.0) * kap_ref[...])
        bd_ref[...] = kk * a
    gate_ref[...] = _dot(_sigmoid(gd), g2_ref[...], HI)
    bonus_ref[...] = _seg_sum(r * k * rkp_ref[...], gh) * v


def _rwkv_prep(P, prm, gh, n_ctx, tm):
    T = P.shape[0]
    W = prm["kk"].shape[1]
    PW = 2048
    nb8 = T // 8
    row = pl.BlockSpec((tm, W), lambda i: (i, 0))
    full = lambda a: pl.BlockSpec(a.shape, lambda i: (0,) * a.ndim)
    args = (prm["mu"], prm["w0"], prm["w2"], prm["a0"], prm["a2"], prm["g2"], prm["kk"], prm["ka"], prm["rk"], gh)
    return pl.pallas_call(
        functools.partial(_rwkv_prep_kernel, tm=tm, n_ctx=n_ctx, n_tot=T, W=W),
        grid=(T // tm,),
        in_specs=[pl.BlockSpec((tm, PW), lambda i: (i, 0)),
                  pl.BlockSpec((8, PW), lambda i: (jnp.maximum(i * (tm // 8) - 1, 0), 0)),
                  pl.BlockSpec((8, PW), lambda i: (jnp.minimum((i + 1) * (tm // 8), nb8 - 1), 0))]
                 + [full(a) for a in args],
        out_specs=[row] * 11,
        out_shape=[jax.ShapeDtypeStruct((T, W), F32)] * 11,
        compiler_params=_cp(("arbitrary",)),
        name="rwkv_prep",
    )(P, P, P, *args)


def _rwkv_scan_kernel(rf, vf, kkf, lwf, kf, bf, rb, vb, kkb, lwb, kb, bb, yf_ref, yb_ref, s_scr, *, C, H, N):
    @pl.when(pl.program_id(0) == 0)
    def _():
        s_scr[...] = jnp.zeros_like(s_scr)

    row = lax.broadcasted_iota(jnp.int32, (C, C), 0)
    col = lax.broadcasted_iota(jnp.int32, (C, C), 1)
    eye = (row == col).astype(F32)
    n_sq = int(np.log2(C)) - 1
    dirs = ((rf, vf, kkf, lwf, kf, bf), (rb, vb, kkb, lwb, kb, bb))
    ch = []
    for d, (r_ref, v_ref, kk_ref, lw_ref, k_ref, b_ref) in enumerate(dirs):
        incl = (col <= row) if d == 0 else (col >= row)
        strict = (col < row) if d == 0 else (col > row)
        lw = lw_ref[...]
        cum = _dot(incl.astype(F32), lw, HI)
        e_pos = jnp.exp(cum)
        e_neg = jnp.exp(-cum)
        rt = (r_ref[...] * e_pos).astype(BF16)
        at = (-kk_ref[...] * jnp.exp(cum - lw)).astype(BF16)
        bt = (b_ref[...] * e_neg).astype(BF16)
        kt = (k_ref[...] * e_neg).astype(BF16)
        vv = v_ref[...].astype(BF16)
        gam = e_pos[C - 1:C, :] if d == 0 else e_pos[0:1, :]
        for h in range(H):
            sl = slice(h * N, (h + 1) * N)
            ch.append(dict(d=d, h=h, incl=incl, strict=strict, at=at[:, sl], rt=rt[:, sl], bt=bt[:, sl],
                           kt=kt[:, sl], v=vv[:, sl], gam=gam[:, sl], s0=s_scr[d, h]))
    for c in ch:
        a_all = _dot_nt(jnp.concatenate([c["at"], c["rt"]], axis=0), jnp.concatenate([c["bt"], c["kt"]], axis=0))
        c["l_pow"] = jnp.where(c["strict"], a_all[:C, :C], 0.0)
        c["l_ak"] = jnp.where(c["strict"], a_all[:C, C:], 0.0).astype(BF16)
        c["m_rb"] = jnp.where(c["incl"], a_all[C:, :C], 0.0).astype(BF16)
        c["m_rk"] = jnp.where(c["incl"], a_all[C:, C:], 0.0).astype(BF16)
        c["t_inv"] = eye + c["l_pow"]
    for _ in range(n_sq):
        for c in ch:
            lb = c["l_pow"].astype(BF16)
            c["l_pow"] = _dot(lb, lb)
        for c in ch:
            c["t_inv"] = c["t_inv"] + _dot(c["t_inv"].astype(BF16), c["l_pow"].astype(BF16))
    for c in ch:
        c["s0b"] = c["s0"].astype(BF16)
        c["w1"] = _dot_nt(c["at"], c["s0b"]) + _dot(c["l_ak"], c["v"])
    for c in ch:
        c["u"] = _dot(c["t_inv"].astype(BF16), c["w1"].astype(BF16)).astype(BF16)
    for c in ch:
        s_scr[c["d"], c["h"]] = (c["s0"] + _dot_tn(c["u"], c["bt"]) + _dot_tn(c["v"], c["kt"])) * c["gam"]
    for c in ch:
        c["y"] = _dot_nt(c["rt"], c["s0b"]) + _dot(c["m_rb"], c["u"]) + _dot(c["m_rk"], c["v"])
    yf_ref[...] = jnp.concatenate([c["y"] for c in ch[:H]], axis=1)
    yb_ref[...] = jnp.concatenate([c["y"] for c in ch[H:]], axis=1)


def _rwkv_scan(r, v, kk, lwf, kf, bf, lwb, kb, bb, n_ctx):
    T, W = r.shape
    C = SCAN_CHUNK
    H = W // RWKV_HEAD
    nch = T // C
    cch = n_ctx // C
    fwd = pl.BlockSpec((C, W), lambda n: (n, 0))
    bwd = pl.BlockSpec((C, W), lambda n: (jnp.where(n < cch, cch - 1 - n, nch - 1 + cch - n), 0))
    return pl.pallas_call(
        functools.partial(_rwkv_scan_kernel, C=C, H=H, N=RWKV_HEAD),
        grid=(nch,),
        in_specs=[fwd] * 6 + [bwd] * 6,
        out_specs=[fwd, bwd],
        out_shape=[jax.ShapeDtypeStruct((T, W), F32)] * 2,
        scratch_shapes=[pltpu.VMEM((2, H, RWKV_HEAD, RWKV_HEAD), F32)],
        compiler_params=_cp(("arbitrary",)),
        name="rwkv_scan",
    )(r, v, kk, lwf, kf, bf, r, v, kk, lwb, kb, bb)


def _rwkv_post_kernel(yf_ref, yb_ref, bonus_ref, gate_ref, lng_ref, lnb_ref, gh_ref, o_ref):
    gh = gh_ref[...]
    y = yf_ref[...] + yb_ref[...]
    mean = _seg_sum(y, gh) * (1.0 / RWKV_HEAD)
    yc = y - mean
    var = _seg_sum(yc * yc, gh) * (1.0 / RWKV_HEAD)
    yn = yc * lax.rsqrt(var + RWKV_GN_EPS) * lng_ref[...] + lnb_ref[...]
    o_ref[...] = ((yn + bonus_ref[...]) * gate_ref[...]).astype(o_ref.dtype)


def _rwkv_post(yf, yb, bonus, gate, lng, lnb, gh, tm):
    T, W = yf.shape
    row = pl.BlockSpec((tm, W), lambda i: (i, 0))
    full = lambda a: pl.BlockSpec(a.shape, lambda i: (0,) * a.ndim)
    return pl.pallas_call(
        _rwkv_post_kernel,
        grid=(T // tm,),
        in_specs=[row] * 4 + [full(lng), full(lnb), full(gh)],
        out_specs=row,
        out_shape=jax.ShapeDtypeStruct((T, W), BF16),
        compiler_params=_cp(("arbitrary",)),
        name="rwkv_post",
    )(yf, yb, bonus, gate, lng, lnb, gh)


def _gmlp_kernel(p_ref, g_ref, ws_ref, b_ref, o_ref, *, tm, W):
    u = jax.nn.gelu(p_ref[:, 0:W])
    v = jax.nn.gelu(p_ref[:, W:2 * W])
    v = v * lax.rsqrt(jnp.mean(v * v, axis=-1, keepdims=True) + NORM_EPS) * g_ref[...]
    vb = v.astype(BF16)
    gw = W // GM_GROUPS
    for c in range(tm // GM_CHUNK):
        rows = slice(c * GM_CHUNK, (c + 1) * GM_CHUNK)
        parts = [_dot(ws_ref[g], vb[rows, g * gw:(g + 1) * gw]) for g in range(GM_GROUPS)]
        s = jnp.concatenate(parts, axis=1) + b_ref[...]
        o_ref[rows, :] = (u[rows, :] * s).astype(o_ref.dtype)


def _gmlp(P, col_block, g, ws, bias, tm):
    T = P.shape[0]
    W = g.shape[1]
    return pl.pallas_call(
        functools.partial(_gmlp_kernel, tm=tm, W=W),
        grid=(T // tm,),
        in_specs=[pl.BlockSpec((tm, 2 * W), lambda i: (i, col_block)),
                  pl.BlockSpec((1, W), lambda i: (0, 0)),
                  pl.BlockSpec(ws.shape, lambda i: (0, 0, 0)),
                  pl.BlockSpec(bias.shape, lambda i: (0, 0))],
        out_specs=pl.BlockSpec((tm, W), lambda i: (i, 0)),
        out_shape=jax.ShapeDtypeStruct((T, W), BF16),
        compiler_params=_cp(("arbitrary",)),
        name="gmlp",
    )(P, g, ws, bias)


def _rope(x, cos, sin, lane):
    w = x.shape[1]
    partner = jnp.where((lane % 32) < 16, pltpu.roll(x, w - 16, axis=1), pltpu.roll(x, 16, axis=1))
    return x * cos + partner * sin


def _qk_prep_kernel(q_ref, k_ref, cos_ref, sin_ref, qg_ref, kg_ref, ghq_ref, ghk_ref, qo_ref, ko_ref, *, scale):
    cos = cos_ref[...]
    sin = sin_ref[...]
    for x_ref, g_ref, gh_ref, o_ref, mul in ((q_ref, qg_ref, ghq_ref, qo_ref, scale), (k_ref, kg_ref, ghk_ref, ko_ref, 1.0)):
        x = x_ref[...]
        w = x.shape[1]
        ss = _seg_sum(x * x, gh_ref[...]) * (1.0 / ATTN_HEAD)
        xn = x * lax.rsqrt(ss + NORM_EPS) * g_ref[...]
        rep = w // cos.shape[1]
        lane = lax.broadcasted_iota(jnp.int32, x.shape, 1)
        xr = _rope(xn, jnp.tile(cos, (1, rep)), jnp.tile(sin, (1, rep)), lane)
        o_ref[...] = (xr * mul).astype(o_ref.dtype)


def _qk_prep(P, q_block, k_block, cos, sin, qg, kg, ghq, ghk, tm):
    T = P.shape[0]
    QW, KW = qg.shape[1], kg.shape[1]
    full = lambda a: pl.BlockSpec(a.shape, lambda i: (0,) * a.ndim)
    return pl.pallas_call(
        functools.partial(_qk_prep_kernel, scale=ATTN_HEAD ** -0.5),
        grid=(T // tm,),
        in_specs=[pl.BlockSpec((tm, QW), lambda i: (i, q_block)),
                  pl.BlockSpec((tm, KW), lambda i: (i, k_block)),
                  pl.BlockSpec((tm, cos.shape[1]), lambda i: (i, 0)),
                  pl.BlockSpec((tm, sin.shape[1]), lambda i: (i, 0)),
                  full(qg), full(kg), full(ghq), full(ghk)],
        out_specs=[pl.BlockSpec((tm, QW), lambda i: (i, 0)), pl.BlockSpec((tm, KW), lambda i: (i, 0))],
        out_shape=[jax.ShapeDtypeStruct((T, QW), BF16), jax.ShapeDtypeStruct((T, KW), BF16)],
        compiler_params=_cp(("arbitrary",)),
        name="qk_prep",
    )(P, P, cos, sin, qg, kg, ghq, ghk)


def _attn_block(i, nb, sink_ref, q_ref, k_refs, v_refs, o_ref, local):
    L = ATTN_BLOCK
    G = ATTN_GROUP
    hd = ATTN_HEAD
    n_kv = k_refs[-1].shape[1] // hd
    R = G * L
    srow = lax.broadcasted_iota(jnp.int32, (R, 1), 0)
    if local:
        qi = lax.broadcasted_iota(jnp.int32, (R, 3 * L), 0) % L
        kj = lax.broadcasted_iota(jnp.int32, (R, 3 * L), 1)
        rel = kj - L - qi
        valid = (rel <= ATTN_WINDOW) & (rel >= -ATTN_WINDOW)
        valid = valid & ((kj >= L) | (i > 0)) & ((kj < 2 * L) | (i < nb - 1))
    for j in range(n_kv):
        ks = [r[:, j * hd:(j + 1) * hd] for r in k_refs]
        vs = [r[:, j * hd:(j + 1) * hd].astype(BF16) for r in v_refs]
        q = jnp.concatenate([q_ref[:, (j * G + g) * hd:(j * G + g + 1) * hd] for g in range(G)], axis=0)
        sink = jnp.zeros((R, 1), F32)
        for g in range(G):
            sink = jnp.where((srow >= g * L) & (srow < (g + 1) * L), sink_ref[j * G + g], sink)
        s_ctx = _dot_nt(q, ks[-1])
        m = jnp.maximum(jnp.max(s_ctx, axis=1, keepdims=True), sink)
        if local:
            s_loc = _dot_nt(q, jnp.concatenate(ks[:3], axis=0))
            s_loc = jnp.where(valid, s_loc, -1e30)
            m = jnp.maximum(m, jnp.max(s_loc, axis=1, keepdims=True))
            p_loc = jnp.exp(s_loc - m)
        p_ctx = jnp.exp(s_ctx - m)
        den = jnp.sum(p_ctx, axis=1, keepdims=True) + jnp.exp(sink - m)
        acc = _dot(p_ctx.astype(BF16), vs[-1])
        if local:
            den = den + jnp.sum(p_loc, axis=1, keepdims=True)
            acc = acc + _dot(p_loc.astype(BF16), jnp.concatenate(vs[:3], axis=0))
        out = acc / den
        for g in range(G):
            h = j * G + g
            o_ref[:, h * hd:(h + 1) * hd] = out[g * L:(g + 1) * L, :].astype(o_ref.dtype)


def _attn_kernel(sink_ref, q_ref, kp_ref, kc_ref, kn_ref, kx_ref, vp_ref, vc_ref, vn_ref, vx_ref, o_ref, *, cb, nb):
    i = pl.program_id(0)

    @pl.when(i < cb)
    def _():
        _attn_block(i, nb, sink_ref, q_ref, (kx_ref,), (vx_ref,), o_ref, False)

    @pl.when(i >= cb)
    def _():
        _attn_block(i - cb, nb, sink_ref, q_ref, (kp_ref, kc_ref, kn_ref, kx_ref),
                    (vp_ref, vc_ref, vn_ref, vx_ref), o_ref, True)


def _attention(qr, kr, P, v_block, sink, n_ctx):
    T, QW = qr.shape
    KW = kr.shape[1]
    L = ATTN_BLOCK
    cb = n_ctx // L
    nb = (T - n_ctx) // L
    lo, hi = cb, cb + nb - 1
    shifts = (lambda i: jnp.clip(i - 1, lo, hi), lambda i: jnp.clip(i, lo, hi), lambda i: jnp.clip(i + 1, lo, hi))
    kspec = lambda f: pl.BlockSpec((L, KW), lambda i: (f(i), 0))
    vspec = lambda f: pl.BlockSpec((L, KW), lambda i: (f(i), v_block))
    in_specs = ([pl.BlockSpec(memory_space=pltpu.SMEM), pl.BlockSpec((L, QW), lambda i: (i, 0))]
                + [kspec(f) for f in shifts] + [pl.BlockSpec((n_ctx, KW), lambda i: (0, 0))]
                + [vspec(f) for f in shifts] + [pl.BlockSpec((n_ctx, KW), lambda i: (0, v_block))])
    return pl.pallas_call(
        functools.partial(_attn_kernel, cb=cb, nb=nb),
        grid=(cb + nb,),
        in_specs=in_specs,
        out_specs=pl.BlockSpec((L, QW), lambda i: (i, 0)),
        out_shape=jax.ShapeDtypeStruct((T, QW), BF16),
        compiler_params=_cp(("arbitrary",)),
        name="attn",
    )(sink, qr, kr, kr, kr, kr, P, P, P, P)


def _oproj_kernel(x_ref, gt_ref, a_ref, b_ref, c_ref, wa_ref, wb_ref, wc_ref, o_ref, *, tm, n_ctx):
    acc = _dot(a_ref[...], wa_ref[...]) + _dot(b_ref[...], wb_ref[...]) + _dot(c_ref[...], wc_ref[...])
    gate = _pick_mod(gt_ref, _ctx_rows(pl.program_id(0), tm, n_ctx))
    o_ref[...] = x_ref[...] + gate * acc


def _oproj(x, mods, l, y_rw, y_gm, y_at, w_o, n_ctx, tm):
    T, D = x.shape
    tn = 512
    W1 = y_rw.shape[1]
    W3 = y_at.shape[1]
    gate_col = 2 * (D // tn)
    return pl.pallas_call(
        functools.partial(_oproj_kernel, tm=tm, n_ctx=n_ctx),
        grid=(T // tm, D // tn),
        in_specs=[pl.BlockSpec((tm, tn), lambda i, j: (i, j)),
                  pl.BlockSpec((1, 8, tn), lambda i, j: (l, 0, gate_col + j)),
                  pl.BlockSpec((tm, W1), lambda i, j: (i, 0)),
                  pl.BlockSpec((tm, W1), lambda i, j: (i, 0)),
                  pl.BlockSpec((tm, W3), lambda i, j: (i, 0)),
                  pl.BlockSpec((W1, tn), lambda i, j: (0, j)),
                  pl.BlockSpec((W1, tn), lambda i, j: (1, j)),
                  pl.BlockSpec((W3, tn), lambda i, j: (1, j))],
        out_specs=pl.BlockSpec((tm, tn), lambda i, j: (i, j)),
        out_shape=jax.ShapeDtypeStruct((T, D), F32),
        compiler_params=_cp(("arbitrary", "arbitrary")),
        name="oproj",
    )(x, mods, y_rw, y_gm, y_at, w_o, w_o, w_o)


def _router_kernel(x_ref, sh_ref, sc_ref, g_ref, rt_ref, bias_ref, tri_ref,
                   h_ref, hp_ref, idx_ref, wt_ref, rank_ref, cnt_ref, carry, *, tm, n_ctx):
    i = pl.program_id(0)

    @pl.when(i == 0)
    def _():
        carry[...] = jnp.zeros_like(carry)

    is_ctx = _ctx_rows(i, tm, n_ctx)
    h = _norm_mod(x_ref[...], g_ref[...], _pick_mod(sh_ref, is_ctx), _pick_mod(sc_ref, is_ctx))
    h_ref[...] = h.astype(h_ref.dtype)
    _store_slabs(hp_ref, _pack_halves(h), tm)
    E = N_EXPERTS
    pg = E // N_EXPERT_GROUPS
    neg = -jnp.inf
    scores = _sigmoid(_dot_nt(rt_ref[...], h, HI))
    biased = scores + bias_ref[:, 0:1]
    b3 = biased.reshape(N_EXPERT_GROUPS, pg, tm)
    i3 = lax.broadcasted_iota(jnp.int32, b3.shape, 1)
    m1 = jnp.max(b3, axis=1, keepdims=True)
    first = jnp.min(jnp.where(b3 == m1, i3, pg), axis=1, keepdims=True)
    m2 = jnp.max(jnp.where(i3 == first, neg, b3), axis=1, keepdims=True)
    gs = (m1 + m2).reshape(N_EXPERT_GROUPS, tm)
    gi = lax.broadcasted_iota(jnp.int32, gs.shape, 0)
    gsel = jnp.zeros(gs.shape, jnp.bool_)
    for _ in range(TOPK_GROUPS):
        gm = jnp.max(gs, axis=0, keepdims=True)
        gfirst = jnp.min(jnp.where(gs == gm, gi, N_EXPERT_GROUPS), axis=0, keepdims=True)
        hit = gi == gfirst
        gsel = gsel | hit
        gs = jnp.where(hit, neg, gs)
    masked = jnp.where(gsel.reshape(N_EXPERT_GROUPS, 1, tm), b3, neg).reshape(E, tm)
    ei = lax.broadcasted_iota(jnp.int32, (E, tm), 0)
    sel = jnp.zeros((E, tm), jnp.bool_)
    picks = []
    for _ in range(TOP_K):
        mx = jnp.max(masked, axis=0, keepdims=True)
        efirst = jnp.min(jnp.where(masked == mx, ei, E), axis=0, keepdims=True)
        hit = ei == efirst
        sel = sel | hit
        masked = jnp.where(hit, neg, masked)
        picks.append((efirst, hit, jnp.sum(jnp.where(hit, scores, 0.0), axis=0, keepdims=True)))
    wsum = picks[0][2]
    for pk in picks[1:]:
        wsum = wsum + pk[2]
    self_f = jnp.where(sel, 1.0, 0.0)
    rank_dense = carry[:, 0:1] + _dot(self_f.astype(BF16), tri_ref[...])
    carry[...] = carry[...] + jnp.sum(self_f, axis=1, keepdims=True)
    cnt_ref[...] = carry[...]
    for kx, (efirst, hit, wk) in enumerate(picks):
        idx_ref[kx:kx + 1, :] = efirst
        wt_ref[kx:kx + 1, :] = wk / wsum * ROUTED_SCALE
        rank_ref[kx:kx + 1, :] = jnp.sum(jnp.where(hit, rank_dense, 0.0), axis=0, keepdims=True).astype(jnp.int32)


def _router(x, mods, l, g, router_t, bias, tri, n_ctx, tm):
    T, D = x.shape
    E = router_t.shape[0]
    kspec = pl.BlockSpec((TOP_K, tm), lambda i: (0, i))
    return pl.pallas_call(
        functools.partial(_router_kernel, tm=tm, n_ctx=n_ctx),
        grid=(T // tm,),
        in_specs=[pl.BlockSpec((tm, D), lambda i: (i, 0)),
                  pl.BlockSpec((1, 8, D), lambda i: (l, 0, 3)),
                  pl.BlockSpec((1, 8, D), lambda i: (l, 0, 4)),
                  pl.BlockSpec((1, D), lambda i: (0, 0)),
                  pl.BlockSpec((E, D), lambda i: (0, 0)),
                  pl.BlockSpec((E, 128), lambda i: (0, 0)),
                  pl.BlockSpec((tm, tm), lambda i: (0, 0))],
        out_specs=[pl.BlockSpec((tm, D), lambda i: (i, 0)), pl.BlockSpec((tm * SLAB, 128), lambda i: (i, 0)),
                   kspec, kspec, kspec, pl.BlockSpec((E, 128), lambda i: (0, 0))],
        out_shape=[jax.ShapeDtypeStruct((T, D), BF16),
                   jax.ShapeDtypeStruct((T * SLAB, 128), jnp.uint32),
                   jax.ShapeDtypeStruct((TOP_K, T), jnp.int32),
                   jax.ShapeDtypeStruct((TOP_K, T), F32),
                   jax.ShapeDtypeStruct((TOP_K, T), jnp.int32),
                   jax.ShapeDtypeStruct((E, 128), F32)],
        scratch_shapes=[pltpu.VMEM((E, 128), F32)],
        compiler_params=_cp(("arbitrary",)),
        name="router",
    )(x, mods, mods, g, router_t, bias, tri)


def _gather_slabs(idx_ref, n, src_hbm, dst, sem):
    def body(r, carry):
        src = src_hbm.at[pl.ds(pl.multiple_of(idx_ref[r], SLAB), SLAB), :]
        pltpu.make_async_copy(src, dst.at[pl.ds(r * PITCH, SLAB), :], sem).start()
        return carry
    lax.fori_loop(0, n, body, 0, unroll=8)


def _issue_slabs(idx_ref, lo, hi, src_hbm, dst, sem):
    for r in range(lo, hi):
        src = src_hbm.at[pl.ds(pl.multiple_of(idx_ref[r], SLAB), SLAB), :]
        pltpu.make_async_copy(src, dst.at[pl.ds(r * PITCH, SLAB), :], sem).start(priority=r % 2)


def _wait_slabs(n, src_hbm, dst, sem):
    pltpu.make_async_copy(src_hbm.at[pl.ds(0, n * SLAB), :], dst.at[pl.ds(0, n * SLAB), :], sem).wait()


def _experts_kernel(be_ref, nu_ref, first_ref, wslot_ref, nxe_ref, hasn_ref, *refs, TB, l):
    look = GATHER_SLOTS - 1
    tok_refs = refs[:look + 1]
    h_hbm, wg_hbm, wu_hbm, wd_hbm, y_ref, hbuf, sem, wfg, wfu, wfd, wsem = refs[look + 1:]
    b = pl.program_id(0)
    slot = b % GATHER_SLOTS
    slot2 = (b + look) % GATHER_SLOTS
    n_used = nu_ref[0]
    SW = h_hbm.shape[1]

    def weight_copies(e, ws):
        cps = []
        for src, dst in ((wg_hbm, wfg), (wu_hbm, wfu), (wd_hbm, wfd)):
            rows = src.shape[2] // WEIGHT_CHUNKS
            for c in range(WEIGHT_CHUNKS):
                cps.append(pltpu.make_async_copy(src.at[l, e, pl.ds(c * rows, rows)], dst.at[ws, pl.ds(c * rows, rows)],
                                                 wsem.at[ws]))
        return cps

    @pl.when(b == 0)
    def _():
        for j in range(look):
            _gather_slabs(tok_refs[j].at[0, 0], TB, h_hbm, hbuf.at[j], sem.at[j])
        for n, cp in enumerate(weight_copies(be_ref[0], 0)):
            cp.start(priority=n % 2)

    @pl.when((first_ref[b] == 1) & (b < n_used))
    def _():
        ws = wslot_ref[b]

        @pl.when(hasn_ref[b] == 1)
        def _():
            for n, cp in enumerate(weight_copies(nxe_ref[b], 1 - ws)):
                cp.start(priority=n % 2)

        for cp in weight_copies(be_ref[b], ws):
            cp.wait()

    @pl.when(b < n_used)
    def _():
        _wait_slabs(TB, h_hbm, hbuf.at[slot], sem.at[slot])
        ws = wslot_ref[b]
        nxt = (tok_refs[look].at[0, 0], h_hbm, hbuf.at[slot2], sem.at[slot2])
        burst = TB // (2 * SLAB)
        issued = 0
        half = SLAB * SW
        a = u = None
        for s in range(0, SLAB, 2):
            w0 = hbuf[slot, pl.ds(s, TB, stride=PITCH), :]
            w1 = hbuf[slot, pl.ds(s + 1, TB, stride=PITCH), :]
            for unpack, base in ((_unpack_lo, 0), (_unpack_hi, half)):
                xs = jnp.concatenate([unpack(w0), unpack(w1)], axis=1).astype(BF16)
                k0 = base + s * SW
                da = _dot(xs, wfg[ws, k0:k0 + 2 * SW, :].astype(BF16))
                _issue_slabs(nxt[0], issued, issued + burst, *nxt[1:])
                du = _dot(xs, wfu[ws, k0:k0 + 2 * SW, :].astype(BF16))
                _issue_slabs(nxt[0], issued + burst, issued + 2 * burst, *nxt[1:])
                issued += 2 * burst
                a = da if a is None else a + da
                u = du if u is None else u + du
        act = (a * _sigmoid(a) * u).astype(BF16)
        _store_slabs(y_ref, _pack_halves(_dot(act, wfd[ws].astype(BF16))), TB)

        @pl.when(b + look >= n_used)
        def _():
            _wait_slabs(TB, h_hbm, hbuf.at[slot2], sem.at[slot2])

        for j in range(1, look):
            @pl.when((b == 0) & (n_used <= j))
            def _():
                _wait_slabs(TB, h_hbm, hbuf.at[j], sem.at[j])

    @pl.when(b >= n_used)
    def _():
        y_ref[...] = jnp.zeros_like(y_ref)


def _experts(h2s, tok_rows, tables, wg, wu, wd, l):
    SW = h2s.shape[1]
    D = 2 * SW * SLAB
    nblk = tables[0].shape[0]
    TB = EXPERT_TILE
    FF = wg.shape[3]
    tok3 = tok_rows.reshape(nblk, 1, TB)
    grid_spec = pltpu.PrefetchScalarGridSpec(
        num_scalar_prefetch=len(tables),
        grid=(nblk,),
        in_specs=[pl.BlockSpec((1, 1, TB), lambda b, *_, j=j: (jnp.minimum(b + j, nblk - 1), 0, 0), memory_space=pltpu.SMEM)
                  for j in range(GATHER_SLOTS)]
                 + [pl.BlockSpec(memory_space=pl.ANY), pl.BlockSpec(memory_space=pl.ANY),
                  pl.BlockSpec(memory_space=pl.ANY), pl.BlockSpec(memory_space=pl.ANY)],
        out_specs=pl.BlockSpec((TB * SLAB, SW), lambda b, *_: (b, 0)),
        scratch_shapes=[pltpu.VMEM((GATHER_SLOTS, TB * PITCH, SW), jnp.uint32), pltpu.SemaphoreType.DMA((GATHER_SLOTS,)),
                        pltpu.VMEM((2, D, FF), F32), pltpu.VMEM((2, D, FF), F32), pltpu.VMEM((2, FF, D), F32),
                        pltpu.SemaphoreType.DMA((2,))],
    )
    return pl.pallas_call(
        functools.partial(_experts_kernel, TB=TB, l=l),
        grid_spec=grid_spec,
        out_shape=jax.ShapeDtypeStruct((nblk * TB * SLAB, SW), jnp.uint32),
        compiler_params=_cp(("arbitrary",)),
        name="experts",
    )(*tables, *([tok3] * GATHER_SLOTS), h2s, wg, wu, wd)


def _combine_kernel(slc_ref, sln_ref, y_hbm, x_ref, h_ref, wt_ref, gt_ref, swg_ref, swu_ref, swd_ref, o_ref,
                    ybuf, sem, *, tm, n_ctx, nt):
    i = pl.program_id(0)
    slot = i % 2
    SW = y_hbm.shape[1]

    @pl.when(i == 0)
    def _():
        for k in range(TOP_K):
            _gather_slabs(slc_ref.at[0, k], tm, y_hbm, ybuf.at[0, k], sem.at[0])

    hb = h_ref[...]
    a = _dot(hb, swg_ref[...])
    u = _dot(hb, swu_ref[...])
    shared = _dot((a * _sigmoid(a) * u).astype(BF16), swd_ref[...])
    gate = _pick_mod(gt_ref, _ctx_rows(i, tm, n_ctx))
    wk = [jnp.broadcast_to(wt_ref[:, k:k + 1], (tm, SW)) for k in range(TOP_K)]
    for k in range(TOP_K):
        _wait_slabs(tm, y_hbm, ybuf.at[slot, k], sem.at[slot])
    half = SLAB * SW
    burst = tm // SLAB
    for s in range(SLAB):
        lo = slice(s * SW, (s + 1) * SW)
        hi = slice(half + s * SW, half + (s + 1) * SW)
        acc_lo = shared[:, lo]
        acc_hi = shared[:, hi]
        for k in range(TOP_K):
            w = ybuf[slot, k, pl.ds(s, tm, stride=PITCH), :]
            acc_lo = acc_lo + wk[k] * _unpack_lo(w)
            acc_hi = acc_hi + wk[k] * _unpack_hi(w)
            _issue_slabs(sln_ref.at[0, k], s * burst, (s + 1) * burst, y_hbm, ybuf.at[1 - slot, k], sem.at[1 - slot])
        o_ref[:, lo] = x_ref[:, lo] + gate[:, lo] * acc_lo
        o_ref[:, hi] = x_ref[:, hi] + gate[:, hi] * acc_hi

    @pl.when(i + 1 >= nt)
    def _():
        for k in range(TOP_K):
            _wait_slabs(tm, y_hbm, ybuf.at[1 - slot, k], sem.at[1 - slot])


def _combine(y2, slot_rows, wt_t, x, h2, mods, l, swg, swu, swd, n_ctx, tm):
    T, D = x.shape
    SW = y2.shape[1]
    nt = T // tm
    FF = swg.shape[1]
    sl3 = slot_rows.reshape(TOP_K, nt, tm).transpose(1, 0, 2)
    return pl.pallas_call(
        functools.partial(_combine_kernel, tm=tm, n_ctx=n_ctx, nt=nt),
        grid=(nt,),
        in_specs=[pl.BlockSpec((1, TOP_K, tm), lambda i: (i, 0, 0), memory_space=pltpu.SMEM),
                  pl.BlockSpec((1, TOP_K, tm), lambda i: (jnp.minimum(i + 1, nt - 1), 0, 0),
                               memory_space=pltpu.SMEM),
                  pl.BlockSpec(memory_space=pl.ANY),
                  pl.BlockSpec((tm, D), lambda i: (i, 0)),
                  pl.BlockSpec((tm, D), lambda i: (i, 0)),
                  pl.BlockSpec((tm, TOP_K), lambda i: (i, 0)),
                  pl.BlockSpec((1, 8, D), lambda i: (l, 0, 5)),
                  pl.BlockSpec((D, FF), lambda i: (0, 0)),
                  pl.BlockSpec((D, FF), lambda i: (0, 0)),
                  pl.BlockSpec((FF, D), lambda i: (0, 0))],
        out_specs=pl.BlockSpec((tm, D), lambda i: (i, 0)),
        out_shape=jax.ShapeDtypeStruct((T, D), F32),
        scratch_shapes=[pltpu.VMEM((2, TOP_K, tm * PITCH, SW), jnp.uint32), pltpu.SemaphoreType.DMA((2,))],
        compiler_params=_cp(("arbitrary",)),
        name="combine",
    )(sl3, sl3, y2, x, h2, wt_t, mods, swg, swu, swd)


def _seg_indicator(width, seg):
    idx = np.arange(width) // seg
    return jnp.asarray(idx[:, None] == idx[None, :], dtype=BF16)


def _rope_tables(n_ctx, n_lat):
    rows = n_lat // GRID_W
    row = jnp.repeat(jnp.arange(rows, dtype=F32), GRID_W)
    col = jnp.tile(jnp.arange(GRID_W, dtype=F32), rows)
    n_freq = ATTN_HEAD // 4
    inv_freq = ROPE_THETA ** (-jnp.arange(n_freq, dtype=F32) / n_freq)
    ar, ac = row[:, None] * inv_freq, col[:, None] * inv_freq
    cos = jnp.concatenate([jnp.cos(ar), jnp.cos(ar), jnp.cos(ac), jnp.cos(ac)], axis=1)
    sin = jnp.concatenate([-jnp.sin(ar), jnp.sin(ar), -jnp.sin(ac), jnp.sin(ac)], axis=1)
    cos = jnp.concatenate([jnp.ones((n_ctx, ATTN_HEAD), F32), cos], axis=0)
    sin = jnp.concatenate([jnp.zeros((n_ctx, ATTN_HEAD), F32), sin], axis=0)
    return jnp.tile(cos, (1, 2)), jnp.tile(sin, (1, 2))


def kernel(x, c, ctx, c_ctx, w_ada, b_ada, norm1_g, norm2_g, w_in, w_o, rw_mu, rw_w0, rw_w2, rw_a0, rw_a2, rw_g2, rw_kk, rw_ka, rw_rk, rw_ln_g, rw_ln_b, gm_norm_g, gm_ws, gm_b, at_qn, at_kn, at_sink, moe_router, moe_bias, moe_wg, moe_wu, moe_wd, sh_wg, sh_wu, sh_wd):
    B, S, D = x.shape
    C = ctx.shape[1]
    assert B == 1
    L = w_ada.shape[0]
    T = C + S
    RW = rw_kk.shape[1]
    GW = gm_norm_g.shape[1]
    QW = at_sink.shape[1] * ATTN_HEAD
    KW = QW // ATTN_GROUP
    rw_proj = rw_mu.shape[2]
    RWP = 2048
    assert rw_proj <= RWP and T % 768 == 0 and C % 256 == 0 and S % 256 == 0 and D == 2 * SLAB * 128

    xs = jnp.concatenate([ctx[0], x[0]], axis=0)
    cond8 = jnp.zeros((8, D), F32).at[0].set(c_ctx).at[1].set(c[0])
    mods = _ada(cond8, w_ada, b_ada)

    gh_rw = _seg_indicator(RW, RWKV_HEAD)
    gh_q = _seg_indicator(QW, ATTN_HEAD)
    gh_k = _seg_indicator(KW, ATTN_HEAD)
    cos, sin = _rope_tables(C, S)
    tri = jnp.asarray(np.arange(256)[:, None] < np.arange(256)[None, :], dtype=BF16)
    TB = EXPERT_TILE
    nblk = -(-(T * TOP_K) // TB) + N_EXPERTS

    for l in range(L):
        w_in_l = w_in[l]
        w_in_p = jnp.concatenate([w_in_l[:, :rw_proj], jnp.zeros((D, RWP - rw_proj), F32), w_in_l[:, rw_proj:]],
                                 axis=1).astype(BF16)
        P = _inproj(xs, mods, l, norm1_g[l][None], w_in_p, C, 768)
        gm_block = RWP // (2 * GW)
        q_block = (RWP + 2 * GW) // QW
        k_block = (RWP + 2 * GW + QW) // KW
        v_block = k_block + 1

        pad = lambda a: jnp.pad(a, ((0, 0), (0, RWP - rw_proj)))
        prm = {"mu": pad(rw_mu[l]), "w0": rw_w0[l], "w2": rw_w2[l], "a0": rw_a0[l], "a2": rw_a2[l],
               "g2": rw_g2[l], "kk": rw_kk[l][None], "ka": rw_ka[l][None], "rk": rw_rk[l].reshape(1, RW)}
        r, v, kk, lwf, kf, bf, lwb, kb, bb, gate, bonus = _rwkv_prep(P, prm, gh_rw, C, 256)
        yf, yb = _rwkv_scan(r, v, kk, lwf, kf, bf, lwb, kb, bb, C)
        y_rw = _rwkv_post(yf, yb, bonus, gate, rw_ln_g[l][None], rw_ln_b[l][None], gh_rw, 256)

        gm_bias = jnp.repeat(gm_b[l].T, GW // GM_GROUPS, axis=1)
        y_gm = _gmlp(P, gm_block, gm_norm_g[l][None], gm_ws[l].astype(BF16), gm_bias, 256)

        qg = jnp.tile(at_qn[l], QW // ATTN_HEAD)[None]
        kg = jnp.tile(at_kn[l], KW // ATTN_HEAD)[None]
        qr, kr = _qk_prep(P, q_block, k_block, cos, sin, qg, kg, gh_q, gh_k, 256)
        y_at = _attention(qr, kr, P, v_block, at_sink[l], C)

        xs = _oproj(xs, mods, l, y_rw, y_gm, y_at, w_o[l].astype(BF16), C, 768)

        bias_col = jnp.broadcast_to(moe_bias[l][:, None], (N_EXPERTS, 128))
        h2, h2s, idx8, wt8, rank8, cnt = _router(xs, mods, l, norm2_g[l][None], moe_router[l].T, bias_col, tri, C, 256)
        counts = cnt[:, 0].astype(jnp.int32)
        padded = (counts + TB - 1) // TB * TB
        pad_end = jnp.cumsum(padded)
        pad_start = pad_end - padded
        e_ids = jnp.arange(N_EXPERTS, dtype=jnp.int32)
        slot8 = jnp.sum(jnp.where(idx8[:, :, None] == e_ids, pad_start, 0), axis=-1) + rank8
        tok = jnp.zeros((nblk * TB,), jnp.int32).at[slot8.reshape(-1)].set(
            jnp.tile(jnp.arange(T, dtype=jnp.int32), TOP_K), unique_indices=True)
        blk_pos = jnp.arange(nblk, dtype=jnp.int32) * TB
        block_e = jnp.minimum(jnp.sum((pad_end[None, :] <= blk_pos[:, None]).astype(jnp.int32), axis=1), N_EXPERTS - 1)
        n_used = (pad_end[-1] // TB).astype(jnp.int32)
        first = (blk_pos < pad_end[-1]) & jnp.concatenate([jnp.ones((1,), jnp.bool_), block_e[1:] != block_e[:-1]])
        wslot = (jnp.cumsum(first.astype(jnp.int32)) - 1) % 2
        later = jnp.where((counts[None, :] > 0) & (e_ids[None, :] > e_ids[:, None]), e_ids[None, :], N_EXPERTS)
        nxt_of = jnp.min(later, axis=1)
        nxt_blk = jnp.sum(jnp.where(block_e[:, None] == e_ids, nxt_of, 0), axis=1)
        tables = (block_e, n_used.reshape(1), first.astype(jnp.int32), wslot.astype(jnp.int32),
                  jnp.minimum(nxt_blk, N_EXPERTS - 1).astype(jnp.int32), (nxt_blk < N_EXPERTS).astype(jnp.int32))
        y2 = _experts(h2s, tok * SLAB, tables, moe_wg, moe_wu, moe_wd, l)
        xs = _combine(y2, slot8 * SLAB, wt8.T, xs, h2, mods, l, sh_wg[l].astype(BF16), sh_wu[l].astype(BF16),
                      sh_wd[l].astype(BF16), C, 128)
    return xs[C:].reshape(B, S, D)
```

```python
import functools

import jax
import jax.numpy as jnp
import numpy as np
from jax import lax
from jax.experimental import pallas as pl
from jax.experimental.pallas import tpu as pltpu

F32 = jnp.float32
BF16 = jnp.bfloat16
HI = lax.Precision.HIGHEST

NORM_EPS = 1e-6
GRID_W = 64

RWKV_HEAD = 64
DECAY_LORA = 64
ICLR_LORA = 64
GATE_LORA = 128
RWKV_GN_EPS = 64e-5

GM_GROUPS = 8
GM_CHUNK = 128

ATTN_HEAD = 64
ATTN_GROUP = 4
ATTN_WINDOW = 128
ATTN_BLOCK = 128
ROPE_THETA = 10000.0

N_EXPERTS = 64
TOP_K = 8
N_EXPERT_GROUPS = 8
TOPK_GROUPS = 4
ROUTED_SCALE = 2.5

SCAN_CHUNK = 64
EXPERT_TILE = 256
SLAB = 8
GATHER_SLOTS = 6
WEIGHT_CHUNKS = 4
PITCH = 12
VMEM_LIMIT = 56 * 1024 * 1024


def _cp(sem, vmem=VMEM_LIMIT):
    return pltpu.CompilerParams(dimension_semantics=sem, vmem_limit_bytes=vmem)


def _dot(a, b, prec=None):
    return jnp.dot(a, b, preferred_element_type=F32, precision=prec)


def _dot_nt(a, b, prec=None):
    return lax.dot_general(a, b, (((1,), (1,)), ((), ())), preferred_element_type=F32, precision=prec)


def _dot_tn(a, b, prec=None):
    return lax.dot_general(a, b, (((0,), (0,)), ((), ())), preferred_element_type=F32, precision=prec)


def _seg_sum(x, g):
    xh = x.astype(BF16)
    xl = (x - xh.astype(F32)).astype(BF16)
    return _dot(xh, g) + _dot(xl, g)


def _sigmoid(x):
    return jax.nn.sigmoid(x)


def _pack_halves(x):
    half = x.shape[1] // 2
    lo = lax.bitcast_convert_type(x[:, :half].astype(BF16).astype(F32), jnp.uint32)
    hi = lax.bitcast_convert_type(x[:, half:].astype(BF16).astype(F32), jnp.uint32)
    return hi | (lo >> 16)


def _unpack_lo(w):
    return lax.bitcast_convert_type(w << 16, F32)


def _unpack_hi(w):
    return lax.bitcast_convert_type(w & jnp.uint32(0xFFFF0000), F32)


def _store_slabs(ref, words, n):
    for c in range(SLAB):
        ref[pl.ds(c, n, stride=SLAB), :] = words[:, c * 128:(c + 1) * 128]


def _norm_mod(x, g, sh, sc):
    y = x * lax.rsqrt(jnp.mean(x * x, axis=-1, keepdims=True) + NORM_EPS)
    return (y * g) * (1.0 + sc) + sh


def _ctx_rows(i, tm, n_ctx):
    row = i * tm + lax.broadcasted_iota(jnp.int32, (tm, 1), 0)
    return row < n_ctx


def _pick_mod(mod_ref, is_ctx):
    return jnp.where(is_ctx, mod_ref[0, 0:1, :], mod_ref[0, 1:2, :])


def _ada_kernel(c_ref, w_ref, b_ref, o_ref):
    c = c_ref[...]
    s = c * _sigmoid(c)
    o_ref[0] = _dot(s, w_ref[0], HI) + b_ref[0]


def _ada(cond8, w_ada, b_ada):
    L, D, N = w_ada.shape
    tn = 1024
    return pl.pallas_call(
        _ada_kernel,
        grid=(L, N // tn),
        in_specs=[pl.BlockSpec((8, D), lambda l, j: (0, 0)),
                  pl.BlockSpec((1, D, tn), lambda l, j: (l, 0, j)),
                  pl.BlockSpec((1, 1, tn), lambda l, j: (l, 0, j))],
        out_specs=pl.BlockSpec((1, 8, tn), lambda l, j: (l, 0, j)),
        out_shape=jax.ShapeDtypeStruct((L, 8, N), F32),
        compiler_params=_cp(("arbitrary", "arbitrary")),
        name="ada",
    )(cond8, w_ada, b_ada.reshape(L, 1, N))


def _inproj_kernel(x_ref, sh_ref, sc_ref, g_ref, w_ref, o_ref, h_scr, *, tm, n_ctx):
    i = pl.program_id(0)

    @pl.when(pl.program_id(1) == 0)
    def _():
        is_ctx = _ctx_rows(i, tm, n_ctx)
        h = _norm_mod(x_ref[...], g_ref[...], _pick_mod(sh_ref, is_ctx), _pick_mod(sc_ref, is_ctx))
        h_scr[...] = h.astype(BF16)

    o_ref[...] = _dot(h_scr[...], w_ref[...])


def _inproj(x, mods, l, g, w, n_ctx, tm):
    T, D = x.shape
    N = w.shape[1]
    tn = 1152
    assert N % tn == 0
    return pl.pallas_call(
        functools.partial(_inproj_kernel, tm=tm, n_ctx=n_ctx),
        grid=(T // tm, N // tn),
        in_specs=[pl.BlockSpec((tm, D), lambda i, j: (i, 0)),
                  pl.BlockSpec((1, 8, D), lambda i, j: (l, 0, 0)),
                  pl.BlockSpec((1, 8, D), lambda i, j: (l, 0, 1)),
                  pl.BlockSpec((1, D), lambda i, j: (0, 0)),
                  pl.BlockSpec((D, tn), lambda i, j: (0, j))],
        out_specs=pl.BlockSpec((tm, tn), lambda i, j: (i, j)),
        out_shape=jax.ShapeDtypeStruct((T, N), F32),
        scratch_shapes=[pltpu.VMEM((tm, D), BF16)],
        compiler_params=_cp(("arbitrary", "arbitrary")),
        name="inproj",
    )(x, mods, mods, g, w)


def _softplus(x):
    return jnp.maximum(x, 0.0) + jnp.log(1.0 + jnp.exp(-jnp.abs(x)))


def _rwkv_prep_kernel(p_ref, pp_ref, pn_ref, mu_ref, w0_ref, w2_ref, a0_ref, a2_ref, g2_ref,
                      kkp_ref, kap_ref, rkp_ref, gh_ref,
                      r_ref, v_ref, kk_ref, lwf_ref, kf_ref, bf_ref, lwb_ref, kb_ref, bb_ref,
                      gate_ref, bonus_ref, *, tm, n_ctx, n_tot, W):
    i = pl.program_id(0)
    p = p_ref[...]
    lrow = lax.broadcasted_iota(jnp.int32, (tm, 1), 0)
    grow = i * tm + lrow
    prev = jnp.where(lrow == 0, pp_ref[7:8, :], pltpu.roll(p, 1, axis=0))
    prev = jnp.where((grow == 0) | (grow == n_ctx), 0.0, prev)
    nxt = jnp.where(lrow == tm - 1, pn_ref[0:1, :], pltpu.roll(p, tm - 1, axis=0))
    nxt = jnp.where((grow == n_ctx - 1) | (grow == n_tot - 1), 0.0, nxt)
    ps = p + mu_ref[0:1, :] * (prev - p) + mu_ref[1:2, :] * (nxt - p)

    r = ps[:, 0:W]
    k = ps[:, W:2 * W]
    v = ps[:, 2 * W:3 * W]
    o = 3 * W
    wd = (ps[:, o:o + DECAY_LORA], ps[:, o + DECAY_LORA:o + 2 * DECAY_LORA])
    o += 2 * DECAY_LORA
    ad = (ps[:, o:o + ICLR_LORA], ps[:, o + ICLR_LORA:o + 2 * ICLR_LORA])
    o += 2 * ICLR_LORA
    gd = ps[:, o:o + GATE_LORA]

    gh = gh_ref[...]
    kk = k * kkp_ref[...]
    kk = kk / jnp.maximum(jnp.sqrt(_seg_sum(kk * kk, gh)), 1e-12)
    r_ref[...] = r.astype(r_ref.dtype)
    v_ref[...] = v.astype(v_ref.dtype)
    kk_ref[...] = kk.astype(kk_ref.dtype)
    outs = ((lwf_ref, kf_ref, bf_ref), (lwb_ref, kb_ref, bb_ref))
    for d in range(2):
        z = w0_ref[d:d + 1, :] + _dot(jnp.tanh(wd[d]), w2_ref[d], HI)
        w_log = -_softplus(-z) - 0.5
        a = _sigmoid(a0_ref[d:d + 1, :] + _dot(ad[d], a2_ref[d], HI))
        lw_ref, kd_ref, bd_ref = outs[d]
        lw_ref[...] = -jnp.exp(w_log)
        kd_ref[...] = (k * (1.0 + (a - 1.0) * kap_ref[...])).astype(kd_ref.dtype)
        bd_ref[...] = (kk * a).astype(bd_ref.dtype)
    gate_ref[...] = _dot(_sigmoid(gd), g2_ref[...], HI).astype(gate_ref.dtype)
    bonus_ref[...] = (_seg_sum(r * k * rkp_ref[...], gh) * v).astype(bonus_ref.dtype)


def _rwkv_prep(P, prm, gh, n_ctx, tm):
    T = P.shape[0]
    W = prm["kk"].shape[1]
    PW = 2048
    nb8 = T // 8
    row = pl.BlockSpec((tm, W), lambda i: (i, 0))
    full = lambda a: pl.BlockSpec(a.shape, lambda i: (0,) * a.ndim)
    args = (prm["mu"], prm["w0"], prm["w2"], prm["a0"], prm["a2"], prm["g2"], prm["kk"], prm["ka"], prm["rk"], gh)
    return pl.pallas_call(
        functools.partial(_rwkv_prep_kernel, tm=tm, n_ctx=n_ctx, n_tot=T, W=W),
        grid=(T // tm,),
        in_specs=[pl.BlockSpec((tm, PW), lambda i: (i, 0)),
                  pl.BlockSpec((8, PW), lambda i: (jnp.maximum(i * (tm // 8) - 1, 0), 0)),
                  pl.BlockSpec((8, PW), lambda i: (jnp.minimum((i + 1) * (tm // 8), nb8 - 1), 0))]
                 + [full(a) for a in args],
        out_specs=[row] * 11,
        out_shape=[jax.ShapeDtypeStruct((T, W), F32 if i in (3, 6) else BF16) for i in range(11)],
        compiler_params=_cp(("arbitrary",)),
        name="rwkv_prep",
    )(P, P, P, *args)


def _rwkv_scan_kernel(rf, vf, kkf, lwf, kf, bf, rb, vb, kkb, lwb, kb, bb, yf_ref, yb_ref, s_scr, *, C, H, N):
    @pl.when(pl.program_id(0) == 0)
    def _():
        s_scr[...] = jnp.zeros_like(s_scr)

    row = lax.broadcasted_iota(jnp.int32, (C, C), 0)
    col = lax.broadcasted_iota(jnp.int32, (C, C), 1)
    eye = (row == col).astype(F32)
    n_sq = int(np.log2(C)) - 1
    dirs = ((rf, vf, kkf, lwf, kf, bf), (rb, vb, kkb, lwb, kb, bb))
    ch = []
    for d, (r_ref, v_ref, kk_ref, lw_ref, k_ref, b_ref) in enumerate(dirs):
        incl = (col <= row) if d == 0 else (col >= row)
        strict = (col < row) if d == 0 else (col > row)
        lw = lw_ref[...]
        cum = _dot(incl.astype(F32), lw, HI)
        e_pos = jnp.exp(cum)
        e_neg = jnp.exp(-cum)
        rt = (r_ref[...] * e_pos).astype(BF16)
        at = (-kk_ref[...] * jnp.exp(cum - lw)).astype(BF16)
        bt = (b_ref[...] * e_neg).astype(BF16)
        kt = (k_ref[...] * e_neg).astype(BF16)
        vv = v_ref[...].astype(BF16)
        gam = e_pos[C - 1:C, :] if d == 0 else e_pos[0:1, :]
        for h in range(H):
            sl = slice(h * N, (h + 1) * N)
            ch.append(dict(d=d, h=h, incl=incl, strict=strict, at=at[:, sl], rt=rt[:, sl], bt=bt[:, sl],
                           kt=kt[:, sl], v=vv[:, sl], gam=gam[:, sl], s0=s_scr[d, h]))
    for c in ch:
        a_all = _dot_nt(jnp.concatenate([c["at"], c["rt"]], axis=0), jnp.concatenate([c["bt"], c["kt"]], axis=0))
        c["l_pow"] = jnp.where(c["strict"], a_all[:C, :C], 0.0)
        c["l_ak"] = jnp.where(c["strict"], a_all[:C, C:], 0.0).astype(BF16)
        c["m_rb"] = jnp.where(c["incl"], a_all[C:, :C], 0.0).astype(BF16)
        c["m_rk"] = jnp.where(c["incl"], a_all[C:, C:], 0.0).astype(BF16)
        c["t_inv"] = eye + c["l_pow"]
    for _ in range(n_sq):
        for c in ch:
            lb = c["l_pow"].astype(BF16)
            c["l_pow"] = _dot(lb, lb)
        for c in ch:
            c["t_inv"] = c["t_inv"] + _dot(c["t_inv"].astype(BF16), c["l_pow"].astype(BF16))
    for c in ch:
        c["s0b"] = c["s0"].astype(BF16)
        c["w1"] = _dot_nt(c["at"], c["s0b"]) + _dot(c["l_ak"], c["v"])
    for c in ch:
        c["u"] = _dot(c["t_inv"].astype(BF16), c["w1"].astype(BF16)).astype(BF16)
    for c in ch:
        s_scr[c["d"], c["h"]] = (c["s0"] + _dot_tn(c["u"], c["bt"]) + _dot_tn(c["v"], c["kt"])) * c["gam"]
    for c in ch:
        c["y"] = _dot_nt(c["rt"], c["s0b"]) + _dot(c["m_rb"], c["u"]) + _dot(c["m_rk"], c["v"])
    yf_ref[...] = jnp.concatenate([c["y"] for c in ch[:H]], axis=1)
    yb_ref[...] = jnp.concatenate([c["y"] for c in ch[H:]], axis=1)


def _rwkv_scan(r, v, kk, lwf, kf, bf, lwb, kb, bb, n_ctx):
    T, W = r.shape
    C = SCAN_CHUNK
    H = W // RWKV_HEAD
    nch = T // C
    cch = n_ctx // C
    fwd = pl.BlockSpec((C, W), lambda n: (n, 0))
    bwd = pl.BlockSpec((C, W), lambda n: (jnp.where(n < cch, cch - 1 - n, nch - 1 + cch - n), 0))
    return pl.pallas_call(
        functools.partial(_rwkv_scan_kernel, C=C, H=H, N=RWKV_HEAD),
        grid=(nch,),
        in_specs=[fwd] * 6 + [bwd] * 6,
        out_specs=[fwd, bwd],
        out_shape=[jax.ShapeDtypeStruct((T, W), F32)] * 2,
        scratch_shapes=[pltpu.VMEM((2, H, RWKV_HEAD, RWKV_HEAD), F32)],
        compiler_params=_cp(("arbitrary",)),
        name="rwkv_scan",
    )(r, v, kk, lwf, kf, bf, r, v, kk, lwb, kb, bb)


def _rwkv_post_kernel(yf_ref, yb_ref, bonus_ref, gate_ref, lng_ref, lnb_ref, gh_ref, o_ref):
    gh = gh_ref[...]
    y = yf_ref[...] + yb_ref[...]
    mean = _seg_sum(y, gh) * (1.0 / RWKV_HEAD)
    yc = y - mean
    var = _seg_sum(yc * yc, gh) * (1.0 / RWKV_HEAD)
    yn = yc * lax.rsqrt(var + RWKV_GN_EPS) * lng_ref[...] + lnb_ref[...]
    o_ref[...] = ((yn + bonus_ref[...]) * gate_ref[...]).astype(o_ref.dtype)


def _rwkv_post(yf, yb, bonus, gate, lng, lnb, gh, tm):
    T, W = yf.shape
    row = pl.BlockSpec((tm, W), lambda i: (i, 0))
    full = lambda a: pl.BlockSpec(a.shape, lambda i: (0,) * a.ndim)
    return pl.pallas_call(
        _rwkv_post_kernel,
        grid=(T // tm,),
        in_specs=[row] * 4 + [full(lng), full(lnb), full(gh)],
        out_specs=row,
        out_shape=jax.ShapeDtypeStruct((T, W), BF16),
        compiler_params=_cp(("arbitrary",)),
        name="rwkv_post",
    )(yf, yb, bonus, gate, lng, lnb, gh)


def _gmlp_kernel(p_ref, g_ref, ws_ref, b_ref, o_ref, *, tm, W):
    u = jax.nn.gelu(p_ref[:, 0:W])
    v = jax.nn.gelu(p_ref[:, W:2 * W])
    v = v * lax.rsqrt(jnp.mean(v * v, axis=-1, keepdims=True) + NORM_EPS) * g_ref[...]
    vb = v.astype(BF16)
    gw = W // GM_GROUPS
    for c in range(tm // GM_CHUNK):
        rows = slice(c * GM_CHUNK, (c + 1) * GM_CHUNK)
        parts = [_dot(ws_ref[g], vb[rows, g * gw:(g + 1) * gw]) for g in range(GM_GROUPS)]
        s = jnp.concatenate(parts, axis=1) + b_ref[...]
        o_ref[rows, :] = (u[rows, :] * s).astype(o_ref.dtype)


def _gmlp(P, col_block, g, ws, bias, tm):
    T = P.shape[0]
    W = g.shape[1]
    return pl.pallas_call(
        functools.partial(_gmlp_kernel, tm=tm, W=W),
        grid=(T // tm,),
        in_specs=[pl.BlockSpec((tm, 2 * W), lambda i: (i, col_block)),
                  pl.BlockSpec((1, W), lambda i: (0, 0)),
                  pl.BlockSpec(ws.shape, lambda i: (0, 0, 0)),
                  pl.BlockSpec(bias.shape, lambda i: (0, 0))],
        out_specs=pl.BlockSpec((tm, W), lambda i: (i, 0)),
        out_shape=jax.ShapeDtypeStruct((T, W), BF16),
        compiler_params=_cp(("arbitrary",)),
        name="gmlp",
    )(P, g, ws, bias)


def _rope(x, cos, sin, lane):
    w = x.shape[1]
    partner = jnp.where((lane % 32) < 16, pltpu.roll(x, w - 16, axis=1), pltpu.roll(x, 16, axis=1))
    return x * cos + partner * sin


def _qk_prep_kernel(q_ref, k_ref, cos_ref, sin_ref, qg_ref, kg_ref, ghq_ref, ghk_ref, qo_ref, ko_ref, *, scale):
    cos = cos_ref[...]
    sin = sin_ref[...]
    for x_ref, g_ref, gh_ref, o_ref, mul in ((q_ref, qg_ref, ghq_ref, qo_ref, scale), (k_ref, kg_ref, ghk_ref, ko_ref, 1.0)):
        x = x_ref[...]
        w = x.shape[1]
        ss = _seg_sum(x * x, gh_ref[...]) * (1.0 / ATTN_HEAD)
        xn = x * lax.rsqrt(ss + NORM_EPS) * g_ref[...]
        rep = w // cos.shape[1]
        lane = lax.broadcasted_iota(jnp.int32, x.shape, 1)
        xr = _rope(xn, jnp.tile(cos, (1, rep)), jnp.tile(sin, (1, rep)), lane)
        o_ref[...] = (xr * mul).astype(o_ref.dtype)


def _qk_prep(P, q_block, k_block, cos, sin, qg, kg, ghq, ghk, tm):
    T = P.shape[0]
    QW, KW = qg.shape[1], kg.shape[1]
    full = lambda a: pl.BlockSpec(a.shape, lambda i: (0,) * a.ndim)
    return pl.pallas_call(
        functools.partial(_qk_prep_kernel, scale=ATTN_HEAD ** -0.5),
        grid=(T // tm,),
        in_specs=[pl.BlockSpec((tm, QW), lambda i: (i, q_block)),
                  pl.BlockSpec((tm, KW), lambda i: (i, k_block)),
                  pl.BlockSpec((tm, cos.shape[1]), lambda i: (i, 0)),
                  pl.BlockSpec((tm, sin.shape[1]), lambda i: (i, 0)),
                  full(qg), full(kg), full(ghq), full(ghk)],
        out_specs=[pl.BlockSpec((tm, QW), lambda i: (i, 0)), pl.BlockSpec((tm, KW), lambda i: (i, 0))],
        out_shape=[jax.ShapeDtypeStruct((T, QW), BF16), jax.ShapeDtypeStruct((T, KW), BF16)],
        compiler_params=_cp(("arbitrary",)),
        name="qk_prep",
    )(P, P, cos, sin, qg, kg, ghq, ghk)


def _attn_block(i, nb, sink_ref, q_ref, k_refs, v_refs, o_ref, local):
    L = ATTN_BLOCK
    G = ATTN_GROUP
    hd = ATTN_HEAD
    n_kv = k_refs[-1].shape[1] // hd
    R = G * L
    srow = lax.broadcasted_iota(jnp.int32, (R, 1), 0)
    if local:
        qi = lax.broadcasted_iota(jnp.int32, (R, 3 * L), 0) % L
        kj = lax.broadcasted_iota(jnp.int32, (R, 3 * L), 1)
        rel = kj - L - qi
        valid = (rel <= ATTN_WINDOW) & (rel >= -ATTN_WINDOW)
        valid = valid & ((kj >= L) | (i > 0)) & ((kj < 2 * L) | (i < nb - 1))
    for j in range(n_kv):
        ks = [r[:, j * hd:(j + 1) * hd] for r in k_refs]
        vs = [r[:, j * hd:(j + 1) * hd].astype(BF16) for r in v_refs]
        q = jnp.concatenate([q_ref[:, (j * G + g) * hd:(j * G + g + 1) * hd] for g in range(G)], axis=0)
        sink = jnp.zeros((R, 1), F32)
        for g in range(G):
            sink = jnp.where((srow >= g * L) & (srow < (g + 1) * L), sink_ref[j * G + g], sink)
        s_ctx = _dot_nt(q, ks[-1])
        m = jnp.maximum(jnp.max(s_ctx, axis=1, keepdims=True), sink)
        if local:
            s_loc = _dot_nt(q, jnp.concatenate(ks[:3], axis=0))
            s_loc = jnp.where(valid, s_loc, -1e30)
            m = jnp.maximum(m, jnp.max(s_loc, axis=1, keepdims=True))
            p_loc = jnp.exp(s_loc - m)
        p_ctx = jnp.exp(s_ctx - m)
        den = jnp.sum(p_ctx, axis=1, keepdims=True) + jnp.exp(sink - m)
        acc = _dot(p_ctx.astype(BF16), vs[-1])
        if local:
            den = den + jnp.sum(p_loc, axis=1, keepdims=True)
            acc = acc + _dot(p_loc.astype(BF16), jnp.concatenate(vs[:3], axis=0))
        out = acc / den
        for g in range(G):
            h = j * G + g
            o_ref[:, h * hd:(h + 1) * hd] = out[g * L:(g + 1) * L, :].astype(o_ref.dtype)


def _attn_kernel(sink_ref, q_ref, kp_ref, kc_ref, kn_ref, kx_ref, vp_ref, vc_ref, vn_ref, vx_ref, o_ref, *, cb, nb):
    i = pl.program_id(0)

    @pl.when(i < cb)
    def _():
        _attn_block(i, nb, sink_ref, q_ref, (kx_ref,), (vx_ref,), o_ref, False)

    @pl.when(i >= cb)
    def _():
        _attn_block(i - cb, nb, sink_ref, q_ref, (kp_ref, kc_ref, kn_ref, kx_ref),
                    (vp_ref, vc_ref, vn_ref, vx_ref), o_ref, True)


def _attention(qr, kr, P, v_block, sink, n_ctx):
    T, QW = qr.shape
    KW = kr.shape[1]
    L = ATTN_BLOCK
    cb = n_ctx // L
    nb = (T - n_ctx) // L
    lo, hi = cb, cb + nb - 1
    shifts = (lambda i: jnp.clip(i - 1, lo, hi), lambda i: jnp.clip(i, lo, hi), lambda i: jnp.clip(i + 1, lo, hi))
    kspec = lambda f: pl.BlockSpec((L, KW), lambda i: (f(i), 0))
    vspec = lambda f: pl.BlockSpec((L, KW), lambda i: (f(i), v_block))
    in_specs = ([pl.BlockSpec(memory_space=pltpu.SMEM), pl.BlockSpec((L, QW), lambda i: (i, 0))]
                + [kspec(f) for f in shifts] + [pl.BlockSpec((n_ctx, KW), lambda i: (0, 0))]
                + [vspec(f) for f in shifts] + [pl.BlockSpec((n_ctx, KW), lambda i: (0, v_block))])
    return pl.pallas_call(
        functools.partial(_attn_kernel, cb=cb, nb=nb),
        grid=(cb + nb,),
        in_specs=in_specs,
        out_specs=pl.BlockSpec((L, QW), lambda i: (i, 0)),
        out_shape=jax.ShapeDtypeStruct((T, QW), BF16),
        compiler_params=_cp(("arbitrary",)),
        name="attn",
    )(sink, qr, kr, kr, kr, kr, P, P, P, P)


def _oproj_kernel(x_ref, gt_ref, a_ref, b_ref, c_ref, wa_ref, wb_ref, wc_ref, o_ref, *, tm, n_ctx):
    acc = _dot(a_ref[...], wa_ref[...]) + _dot(b_ref[...], wb_ref[...]) + _dot(c_ref[...], wc_ref[...])
    gate = _pick_mod(gt_ref, _ctx_rows(pl.program_id(0), tm, n_ctx))
    o_ref[...] = x_ref[...] + gate * acc


def _oproj(x, mods, l, y_rw, y_gm, y_at, w_o, n_ctx, tm):
    T, D = x.shape
    tn = 1024
    W1 = y_rw.shape[1]
    W3 = y_at.shape[1]
    gate_col = 2 * (D // tn)
    return pl.pallas_call(
        functools.partial(_oproj_kernel, tm=tm, n_ctx=n_ctx),
        grid=(T // tm, D // tn),
        in_specs=[pl.BlockSpec((tm, tn), lambda i, j: (i, j)),
                  pl.BlockSpec((1, 8, tn), lambda i, j: (l, 0, gate_col + j)),
                  pl.BlockSpec((tm, W1), lambda i, j: (i, 0)),
                  pl.BlockSpec((tm, W1), lambda i, j: (i, 0)),
                  pl.BlockSpec((tm, W3), lambda i, j: (i, 0)),
                  pl.BlockSpec((W1, tn), lambda i, j: (0, j)),
                  pl.BlockSpec((W1, tn), lambda i, j: (1, j)),
                  pl.BlockSpec((W3, tn), lambda i, j: (1, j))],
        out_specs=pl.BlockSpec((tm, tn), lambda i, j: (i, j)),
        out_shape=jax.ShapeDtypeStruct((T, D), F32),
        compiler_params=_cp(("arbitrary", "arbitrary")),
        name="oproj",
    )(x, mods, y_rw, y_gm, y_at, w_o, w_o, w_o)


def _router_kernel(x_ref, sh_ref, sc_ref, g_ref, rt_ref, bias_ref, tri_ref,
                   h_ref, hp_ref, idx_ref, wt_ref, rank_ref, cnt_ref, carry, *, tm, n_ctx):
    i = pl.program_id(0)

    @pl.when(i == 0)
    def _():
        carry[...] = jnp.zeros_like(carry)

    is_ctx = _ctx_rows(i, tm, n_ctx)
    h = _norm_mod(x_ref[...], g_ref[...], _pick_mod(sh_ref, is_ctx), _pick_mod(sc_ref, is_ctx))
    h_ref[...] = h.astype(h_ref.dtype)
    _store_slabs(hp_ref, _pack_halves(h), tm)
    E = N_EXPERTS
    pg = E // N_EXPERT_GROUPS
    neg = -jnp.inf
    scores = _sigmoid(_dot_nt(rt_ref[...], h, HI))
    biased = scores + bias_ref[:, 0:1]
    b3 = biased.reshape(N_EXPERT_GROUPS, pg, tm)
    i3 = lax.broadcasted_iota(jnp.int32, b3.shape, 1)
    m1 = jnp.max(b3, axis=1, keepdims=True)
    first = jnp.min(jnp.where(b3 == m1, i3, pg), axis=1, keepdims=True)
    m2 = jnp.max(jnp.where(i3 == first, neg, b3), axis=1, keepdims=True)
    gs = (m1 + m2).reshape(N_EXPERT_GROUPS, tm)
    gi = lax.broadcasted_iota(jnp.int32, gs.shape, 0)
    gsel = jnp.zeros(gs.shape, jnp.bool_)
    for _ in range(TOPK_GROUPS):
        gm = jnp.max(gs, axis=0, keepdims=True)
        gfirst = jnp.min(jnp.where(gs == gm, gi, N_EXPERT_GROUPS), axis=0, keepdims=True)
        hit = gi == gfirst
        gsel = gsel | hit
        gs = jnp.where(hit, neg, gs)
    masked = jnp.where(gsel.reshape(N_EXPERT_GROUPS, 1, tm), b3, neg).reshape(E, tm)
    ei = lax.broadcasted_iota(jnp.int32, (E, tm), 0)
    sel = jnp.zeros((E, tm), jnp.bool_)
    picks = []
    for _ in range(TOP_K):
        mx = jnp.max(masked, axis=0, keepdims=True)
        efirst = jnp.min(jnp.where(masked == mx, ei, E), axis=0, keepdims=True)
        hit = ei == efirst
        sel = sel | hit
        masked = jnp.where(hit, neg, masked)
        picks.append((efirst, hit, jnp.sum(jnp.where(hit, scores, 0.0), axis=0, keepdims=True)))
    wsum = picks[0][2]
    for pk in picks[1:]:
        wsum = wsum + pk[2]
    self_f = jnp.where(sel, 1.0, 0.0)
    rank_dense = carry[:, 0:1] + _dot(self_f.astype(BF16), tri_ref[...])
    carry[...] = carry[...] + jnp.sum(self_f, axis=1, keepdims=True)
    cnt_ref[...] = carry[...]
    for kx, (efirst, hit, wk) in enumerate(picks):
        idx_ref[kx:kx + 1, :] = efirst
        wt_ref[kx:kx + 1, :] = wk / wsum * ROUTED_SCALE
        rank_ref[kx:kx + 1, :] = jnp.sum(jnp.where(hit, rank_dense, 0.0), axis=0, keepdims=True).astype(jnp.int32)


def _router(x, mods, l, g, router_t, bias, tri, n_ctx, tm):
    T, D = x.shape
    E = router_t.shape[0]
    kspec = pl.BlockSpec((TOP_K, tm), lambda i: (0, i))
    return pl.pallas_call(
        functools.partial(_router_kernel, tm=tm, n_ctx=n_ctx),
        grid=(T // tm,),
        in_specs=[pl.BlockSpec((tm, D), lambda i: (i, 0)),
                  pl.BlockSpec((1, 8, D), lambda i: (l, 0, 3)),
                  pl.BlockSpec((1, 8, D), lambda i: (l, 0, 4)),
                  pl.BlockSpec((1, D), lambda i: (0, 0)),
                  pl.BlockSpec((E, D), lambda i: (0, 0)),
                  pl.BlockSpec((E, 128), lambda i: (0, 0)),
                  pl.BlockSpec((tm, tm), lambda i: (0, 0))],
        out_specs=[pl.BlockSpec((tm, D), lambda i: (i, 0)), pl.BlockSpec((tm * SLAB, 128), lambda i: (i, 0)),
                   kspec, kspec, kspec, pl.BlockSpec((E, 128), lambda i: (0, 0))],
        out_shape=[jax.ShapeDtypeStruct((T, D), BF16),
                   jax.ShapeDtypeStruct((T * SLAB, 128), jnp.uint32),
                   jax.ShapeDtypeStruct((TOP_K, T), jnp.int32),
                   jax.ShapeDtypeStruct((TOP_K, T), F32),
                   jax.ShapeDtypeStruct((TOP_K, T), jnp.int32),
                   jax.ShapeDtypeStruct((E, 128), F32)],
        scratch_shapes=[pltpu.VMEM((E, 128), F32)],
        compiler_params=_cp(("arbitrary",)),
        name="router",
    )(x, mods, mods, g, router_t, bias, tri)


def _gather_slabs(idx_ref, n, src_hbm, dst, sem):
    def body(r, carry):
        src = src_hbm.at[pl.ds(pl.multiple_of(idx_ref[r], SLAB), SLAB), :]
        pltpu.make_async_copy(src, dst.at[pl.ds(r * PITCH, SLAB), :], sem).start()
        return carry
    lax.fori_loop(0, n, body, 0, unroll=8)


def _issue_slabs(idx_ref, lo, hi, src_hbm, dst, sem):
    for r in range(lo, hi):
        src = src_hbm.at[pl.ds(pl.multiple_of(idx_ref[r], SLAB), SLAB), :]
        pltpu.make_async_copy(src, dst.at[pl.ds(r * PITCH, SLAB), :], sem).start(priority=r % 2)


def _wait_slabs(n, src_hbm, dst, sem):
    pltpu.make_async_copy(src_hbm.at[pl.ds(0, n * SLAB), :], dst.at[pl.ds(0, n * SLAB), :], sem).wait()


def _experts_kernel(be_ref, nu_ref, first_ref, wslot_ref, nxe_ref, hasn_ref, *refs, TB, l):
    look = GATHER_SLOTS - 1
    tok_refs = refs[:look + 1]
    h_hbm, wg_hbm, wu_hbm, wd_hbm, y_ref, hbuf, sem, wfg, wfu, wfd, wsem = refs[look + 1:]
    b = pl.program_id(0)
    slot = b % GATHER_SLOTS
    slot2 = (b + look) % GATHER_SLOTS
    n_used = nu_ref[0]
    SW = h_hbm.shape[1]

    def weight_copies(e, ws):
        cps = []
        for src, dst in ((wg_hbm, wfg), (wu_hbm, wfu), (wd_hbm, wfd)):
            rows = src.shape[2] // WEIGHT_CHUNKS
            for c in range(WEIGHT_CHUNKS):
                cps.append(pltpu.make_async_copy(src.at[l, e, pl.ds(c * rows, rows)], dst.at[ws, pl.ds(c * rows, rows)],
                                                 wsem.at[ws]))
        return cps

    @pl.when(b == 0)
    def _():
        for j in range(look):
            _gather_slabs(tok_refs[j].at[0, 0], TB, h_hbm, hbuf.at[j], sem.at[j])
        for n, cp in enumerate(weight_copies(be_ref[0], 0)):
            cp.start(priority=n % 2)

    @pl.when((first_ref[b] == 1) & (b < n_used))
    def _():
        ws = wslot_ref[b]

        @pl.when(hasn_ref[b] == 1)
        def _():
            for n, cp in enumerate(weight_copies(nxe_ref[b], 1 - ws)):
                cp.start(priority=n % 2)

        for cp in weight_copies(be_ref[b], ws):
            cp.wait()

    @pl.when(b < n_used)
    def _():
        _wait_slabs(TB, h_hbm, hbuf.at[slot], sem.at[slot])
        ws = wslot_ref[b]
        nxt = (tok_refs[look].at[0, 0], h_hbm, hbuf.at[slot2], sem.at[slot2])
        burst = TB // (2 * SLAB)
        issued = 0
        half = SLAB * SW
        a = u = None
        for s in range(0, SLAB, 2):
            w0 = hbuf[slot, pl.ds(s, TB, stride=PITCH), :]
            w1 = hbuf[slot, pl.ds(s + 1, TB, stride=PITCH), :]
            for unpack, base in ((_unpack_lo, 0), (_unpack_hi, half)):
                xs = jnp.concatenate([unpack(w0), unpack(w1)], axis=1).astype(BF16)
                k0 = base + s * SW
                da = _dot(xs, wfg[ws, k0:k0 + 2 * SW, :].astype(BF16))
                _issue_slabs(nxt[0], issued, issued + burst, *nxt[1:])
                du = _dot(xs, wfu[ws, k0:k0 + 2 * SW, :].astype(BF16))
                _issue_slabs(nxt[0], issued + burst, issued + 2 * burst, *nxt[1:])
                issued += 2 * burst
                a = da if a is None else a + da
                u = du if u is None else u + du
        act = (a * _sigmoid(a) * u).astype(BF16)
        _store_slabs(y_ref, _pack_halves(_dot(act, wfd[ws].astype(BF16))), TB)

        @pl.when(b + look >= n_used)
        def _():
            _wait_slabs(TB, h_hbm, hbuf.at[slot2], sem.at[slot2])

        for j in range(1, look):
            @pl.when((b == 0) & (n_used <= j))
            def _():
                _wait_slabs(TB, h_hbm, hbuf.at[j], sem.at[j])

    @pl.when(b >= n_used)
    def _():
        y_ref[...] = jnp.zeros_like(y_ref)


def _experts(h2s, tok_rows, tables, wg, wu, wd, l):
    SW = h2s.shape[1]
    D = 2 * SW * SLAB
    nblk = tables[0].shape[0]
    TB = EXPERT_TILE
    FF = wg.shape[3]
    tok3 = tok_rows.reshape(nblk, 1, TB)
    grid_spec = pltpu.PrefetchScalarGridSpec(
        num_scalar_prefetch=len(tables),
        grid=(nblk,),
        in_specs=[pl.BlockSpec((1, 1, TB), lambda b, *_, j=j: (jnp.minimum(b + j, nblk - 1), 0, 0), memory_space=pltpu.SMEM)
                  for j in range(GATHER_SLOTS)]
                 + [pl.BlockSpec(memory_space=pl.ANY), pl.BlockSpec(memory_space=pl.ANY),
                  pl.BlockSpec(memory_space=pl.ANY), pl.BlockSpec(memory_space=pl.ANY)],
        out_specs=pl.BlockSpec((TB * SLAB, SW), lambda b, *_: (b, 0)),
        scratch_shapes=[pltpu.VMEM((GATHER_SLOTS, TB * PITCH, SW), jnp.uint32), pltpu.SemaphoreType.DMA((GATHER_SLOTS,)),
                        pltpu.VMEM((2, D, FF), F32), pltpu.VMEM((2, D, FF), F32), pltpu.VMEM((2, FF, D), F32),
                        pltpu.SemaphoreType.DMA((2,))],
    )
    return pl.pallas_call(
        functools.partial(_experts_kernel, TB=TB, l=l),
        grid_spec=grid_spec,
        out_shape=jax.ShapeDtypeStruct((nblk * TB * SLAB, SW), jnp.uint32),
        compiler_params=_cp(("arbitrary",)),
        name="experts",
    )(*tables, *([tok3] * GATHER_SLOTS), h2s, wg, wu, wd)


def _combine_kernel(slc_ref, sln_ref, y_hbm, x_ref, h_ref, wt_ref, gt_ref, swg_ref, swu_ref, swd_ref, o_ref,
                    ybuf, sem, *, tm, n_ctx, nt):
    i = pl.program_id(0)
    slot = i % 2
    SW = y_hbm.shape[1]

    @pl.when(i == 0)
    def _():
        for k in range(TOP_K):
            _gather_slabs(slc_ref.at[0, k], tm, y_hbm, ybuf.at[0, k], sem.at[0])

    hb = h_ref[...]
    a = _dot(hb, swg_ref[...])
    u = _dot(hb, swu_ref[...])
    shared = _dot((a * _sigmoid(a) * u).astype(BF16), swd_ref[...])
    gate = _pick_mod(gt_ref, _ctx_rows(i, tm, n_ctx))
    wk = [jnp.broadcast_to(wt_ref[:, k:k + 1], (tm, SW)) for k in range(TOP_K)]
    for k in range(TOP_K):
        _wait_slabs(tm, y_hbm, ybuf.at[slot, k], sem.at[slot])
    half = SLAB * SW
    burst = tm // SLAB
    for s in range(SLAB):
        lo = slice(s * SW, (s + 1) * SW)
        hi = slice(half + s * SW, half + (s + 1) * SW)
        acc_lo = shared[:, lo]
        acc_hi = shared[:, hi]
        for k in range(TOP_K):
            w = ybuf[slot, k, pl.ds(s, tm, stride=PITCH), :]
            acc_lo = acc_lo + wk[k] * _unpack_lo(w)
            acc_hi = acc_hi + wk[k] * _unpack_hi(w)
            _issue_slabs(sln_ref.at[0, k], s * burst, (s + 1) * burst, y_hbm, ybuf.at[1 - slot, k], sem.at[1 - slot])
        o_ref[:, lo] = x_ref[:, lo] + gate[:, lo] * acc_lo
        o_ref[:, hi] = x_ref[:, hi] + gate[:, hi] * acc_hi

    @pl.when(i + 1 >= nt)
    def _():
        for k in range(TOP_K):
            _wait_slabs(tm, y_hbm, ybuf.at[1 - slot, k], sem.at[1 - slot])


def _combine(y2, slot_rows, wt_t, x, h2, mods, l, swg, swu, swd, n_ctx, tm, drop_ctx):
    T, D = x.shape
    SW = y2.shape[1]
    nt = T // tm
    FF = swg.shape[1]
    sl3 = slot_rows.reshape(TOP_K, nt, tm).transpose(1, 0, 2)
    skip = n_ctx // tm if drop_ctx else 0
    return pl.pallas_call(
        functools.partial(_combine_kernel, tm=tm, n_ctx=n_ctx, nt=nt),
        grid=(nt,),
        in_specs=[pl.BlockSpec((1, TOP_K, tm), lambda i: (i, 0, 0), memory_space=pltpu.SMEM),
                  pl.BlockSpec((1, TOP_K, tm), lambda i: (jnp.minimum(i + 1, nt - 1), 0, 0),
                               memory_space=pltpu.SMEM),
                  pl.BlockSpec(memory_space=pl.ANY),
                  pl.BlockSpec((tm, D), lambda i: (i, 0)),
                  pl.BlockSpec((tm, D), lambda i: (i, 0)),
                  pl.BlockSpec((tm, TOP_K), lambda i: (i, 0)),
                  pl.BlockSpec((1, 8, D), lambda i: (l, 0, 5)),
                  pl.BlockSpec((D, FF), lambda i: (0, 0)),
                  pl.BlockSpec((D, FF), lambda i: (0, 0)),
                  pl.BlockSpec((FF, D), lambda i: (0, 0))],
        out_specs=pl.BlockSpec((tm, D), lambda i: (jnp.maximum(i - skip, 0), 0)),
        out_shape=jax.ShapeDtypeStruct((T - skip * tm, D), F32),
        scratch_shapes=[pltpu.VMEM((2, TOP_K, tm * PITCH, SW), jnp.uint32), pltpu.SemaphoreType.DMA((2,))],
        compiler_params=_cp(("arbitrary",)),
        name="combine",
    )(sl3, sl3, y2, x, h2, wt_t, mods, swg, swu, swd)


def _seg_indicator(width, seg):
    idx = np.arange(width) // seg
    return jnp.asarray(idx[:, None] == idx[None, :], dtype=BF16)


def _rope_tables(n_ctx, n_lat):
    rows = n_lat // GRID_W
    row = jnp.repeat(jnp.arange(rows, dtype=F32), GRID_W)
    col = jnp.tile(jnp.arange(GRID_W, dtype=F32), rows)
    n_freq = ATTN_HEAD // 4
    inv_freq = ROPE_THETA ** (-jnp.arange(n_freq, dtype=F32) / n_freq)
    ar, ac = row[:, None] * inv_freq, col[:, None] * inv_freq
    cos = jnp.concatenate([jnp.cos(ar), jnp.cos(ar), jnp.cos(ac), jnp.cos(ac)], axis=1)
    sin = jnp.concatenate([-jnp.sin(ar), jnp.sin(ar), -jnp.sin(ac), jnp.sin(ac)], axis=1)
    cos = jnp.concatenate([jnp.ones((n_ctx, ATTN_HEAD), F32), cos], axis=0)
    sin = jnp.concatenate([jnp.zeros((n_ctx, ATTN_HEAD), F32), sin], axis=0)
    return jnp.tile(cos, (1, 2)), jnp.tile(sin, (1, 2))


def kernel(x, c, ctx, c_ctx, w_ada, b_ada, norm1_g, norm2_g, w_in, w_o, rw_mu, rw_w0, rw_w2, rw_a0, rw_a2, rw_g2, rw_kk, rw_ka, rw_rk, rw_ln_g, rw_ln_b, gm_norm_g, gm_ws, gm_b, at_qn, at_kn, at_sink, moe_router, moe_bias, moe_wg, moe_wu, moe_wd, sh_wg, sh_wu, sh_wd):
    B, S, D = x.shape
    C = ctx.shape[1]
    assert B == 1
    L = w_ada.shape[0]
    T = C + S
    RW = rw_kk.shape[1]
    GW = gm_norm_g.shape[1]
    QW = at_sink.shape[1] * ATTN_HEAD
    KW = QW // ATTN_GROUP
    rw_proj = rw_mu.shape[2]
    RWP = 2048
    assert rw_proj <= RWP and T % 768 == 0 and C % 256 == 0 and S % 256 == 0 and D == 2 * SLAB * 128

    xs = jnp.concatenate([ctx[0], x[0]], axis=0)
    cond8 = jnp.zeros((8, D), F32).at[0].set(c_ctx).at[1].set(c[0])
    mods = _ada(cond8, w_ada, b_ada)

    gh_rw = _seg_indicator(RW, RWKV_HEAD)
    gh_q = _seg_indicator(QW, ATTN_HEAD)
    gh_k = _seg_indicator(KW, ATTN_HEAD)
    cos, sin = _rope_tables(C, S)
    tri = jnp.asarray(np.arange(256)[:, None] < np.arange(256)[None, :], dtype=BF16)
    TB = EXPERT_TILE
    nblk = -(-(T * TOP_K) // TB) + N_EXPERTS

    for l in range(L):
        w_in_l = w_in[l]
        w_in_p = jnp.concatenate([w_in_l[:, :rw_proj], jnp.zeros((D, RWP - rw_proj), F32), w_in_l[:, rw_proj:]],
                                 axis=1).astype(BF16)
        P = _inproj(xs, mods, l, norm1_g[l][None], w_in_p, C, 768)
        gm_block = RWP // (2 * GW)
        q_block = (RWP + 2 * GW) // QW
        k_block = (RWP + 2 * GW + QW) // KW
        v_block = k_block + 1

        pad = lambda a: jnp.pad(a, ((0, 0), (0, RWP - rw_proj)))
        prm = {"mu": pad(rw_mu[l]), "w0": rw_w0[l], "w2": rw_w2[l], "a0": rw_a0[l], "a2": rw_a2[l],
               "g2": rw_g2[l], "kk": rw_kk[l][None], "ka": rw_ka[l][None], "rk": rw_rk[l].reshape(1, RW)}
        r, v, kk, lwf, kf, bf, lwb, kb, bb, gate, bonus = _rwkv_prep(P, prm, gh_rw, C, 256)
        yf, yb = _rwkv_scan(r, v, kk, lwf, kf, bf, lwb, kb, bb, C)
        y_rw = _rwkv_post(yf, yb, bonus, gate, rw_ln_g[l][None], rw_ln_b[l][None], gh_rw, 256)

        gm_bias = jnp.repeat(gm_b[l].T, GW // GM_GROUPS, axis=1)
        y_gm = _gmlp(P, gm_block, gm_norm_g[l][None], gm_ws[l].astype(BF16), gm_bias, 256)

        qg = jnp.tile(at_qn[l], QW // ATTN_HEAD)[None]
        kg = jnp.tile(at_kn[l], KW // ATTN_HEAD)[None]
        qr, kr = _qk_prep(P, q_block, k_block, cos, sin, qg, kg, gh_q, gh_k, 256)
        y_at = _attention(qr, kr, P, v_block, at_sink[l], C)

        xs = _oproj(xs, mods, l, y_rw, y_gm, y_at, w_o[l].astype(BF16), C, 768)

        bias_col = jnp.broadcast_to(moe_bias[l][:, None], (N_EXPERTS, 128))
        h2, h2s, idx8, wt8, rank8, cnt = _router(xs, mods, l, norm2_g[l][None], moe_router[l].T, bias_col, tri, C, 256)
        counts = cnt[:, 0].astype(jnp.int32)
        padded = (counts + TB - 1) // TB * TB
        pad_end = jnp.cumsum(padded)
        pad_start = pad_end - padded
        e_ids = jnp.arange(N_EXPERTS, dtype=jnp.int32)
        slot8 = jnp.sum(jnp.where(idx8[:, :, None] == e_ids, pad_start, 0), axis=-1) + rank8
        tok = jnp.zeros((nblk * TB,), jnp.int32).at[slot8.reshape(-1)].set(
            jnp.tile(jnp.arange(T, dtype=jnp.int32), TOP_K), unique_indices=True)
        blk_pos = jnp.arange(nblk, dtype=jnp.int32) * TB
        block_e = jnp.minimum(jnp.sum((pad_end[None, :] <= blk_pos[:, None]).astype(jnp.int32), axis=1), N_EXPERTS - 1)
        n_used = (pad_end[-1] // TB).astype(jnp.int32)
        first = (blk_pos < pad_end[-1]) & jnp.concatenate([jnp.ones((1,), jnp.bool_), block_e[1:] != block_e[:-1]])
        wslot = (jnp.cumsum(first.astype(jnp.int32)) - 1) % 2
        later = jnp.where((counts[None, :] > 0) & (e_ids[None, :] > e_ids[:, None]), e_ids[None, :], N_EXPERTS)
        nxt_of = jnp.min(later, axis=1)
        nxt_blk = jnp.sum(jnp.where(block_e[:, None] == e_ids, nxt_of, 0), axis=1)
        tables = (block_e, n_used.reshape(1), first.astype(jnp.int32), wslot.astype(jnp.int32),
                  jnp.minimum(nxt_blk, N_EXPERTS - 1).astype(jnp.int32), (nxt_blk < N_EXPERTS).astype(jnp.int32))
        y2 = _experts(h2s, tok * SLAB, tables, moe_wg, moe_wu, moe_wd, l)
        xs = _combine(y2, slot8 * SLAB, wt8.T, xs, h2, mods, l, sh_wg[l].astype(BF16), sh_wu[l].astype(BF16),
                      sh_wd[l].astype(BF16), C, 128, drop_ctx=(l == L - 1))
    return xs.reshape(B, S, D)
```

```python
import functools

import jax
import jax.numpy as jnp
import numpy as np
from jax import lax
from jax.experimental import pallas as pl
from jax.experimental.pallas import tpu as pltpu

F32 = jnp.float32
BF16 = jnp.bfloat16
HI = lax.Precision.HIGHEST

NORM_EPS = 1e-6
GRID_W = 64

RWKV_HEAD = 64
DECAY_LORA = 64
ICLR_LORA = 64
GATE_LORA = 128
RWKV_GN_EPS = 64e-5

GM_GROUPS = 8
GM_CHUNK = 128

ATTN_HEAD = 64
ATTN_GROUP = 4
ATTN_WINDOW = 128
ATTN_BLOCK = 128
ROPE_THETA = 10000.0

N_EXPERTS = 64
TOP_K = 8
N_EXPERT_GROUPS = 8
TOPK_GROUPS = 4
ROUTED_SCALE = 2.5

SCAN_CHUNK = 64
EXPERT_TILE = 256
SLAB = 8
GATHER_SLOTS = 8
WEIGHT_CHUNKS = 4
PITCH = 12
VMEM_LIMIT = 56 * 1024 * 1024


def _cp(sem, vmem=VMEM_LIMIT):
    return pltpu.CompilerParams(dimension_semantics=sem, vmem_limit_bytes=vmem)


def _dot(a, b, prec=None):
    return jnp.dot(a, b, preferred_element_type=F32, precision=prec)


def _dot_nt(a, b, prec=None):
    return lax.dot_general(a, b, (((1,), (1,)), ((), ())), preferred_element_type=F32, precision=prec)


def _dot_tn(a, b, prec=None):
    return lax.dot_general(a, b, (((0,), (0,)), ((), ())), preferred_element_type=F32, precision=prec)


def _dot3(x, w):
    xh = x.astype(BF16)
    xl = (x - xh.astype(F32)).astype(BF16)
    wh = w.astype(BF16)
    wl = (w - wh.astype(F32)).astype(BF16)
    return _dot(xh, wh) + _dot(xl, wh) + _dot(xh, wl)


def _seg_sum(x, g):
    xh = x.astype(BF16)
    xl = (x - xh.astype(F32)).astype(BF16)
    return _dot(xh, g) + _dot(xl, g)


def _sigmoid(x):
    return jax.nn.sigmoid(x)


def _pack_halves(x):
    half = x.shape[1] // 2
    lo = lax.bitcast_convert_type(x[:, :half].astype(BF16).astype(F32), jnp.uint32)
    hi = lax.bitcast_convert_type(x[:, half:].astype(BF16).astype(F32), jnp.uint32)
    return hi | (lo >> 16)


def _unpack_lo(w):
    return lax.bitcast_convert_type(w << 16, F32)


def _unpack_hi(w):
    return lax.bitcast_convert_type(w & jnp.uint32(0xFFFF0000), F32)


def _store_slabs(ref, words, n):
    for c in range(SLAB):
        ref[pl.ds(c, n, stride=SLAB), :] = words[:, c * 128:(c + 1) * 128]


def _norm_mod(x, g, sh, sc):
    y = x * lax.rsqrt(jnp.mean(x * x, axis=-1, keepdims=True) + NORM_EPS)
    return (y * g) * (1.0 + sc) + sh


def _ctx_rows(i, tm, n_ctx):
    row = i * tm + lax.broadcasted_iota(jnp.int32, (tm, 1), 0)
    return row < n_ctx


def _pick_mod(mod_ref, is_ctx):
    return jnp.where(is_ctx, mod_ref[0, 0:1, :], mod_ref[0, 1:2, :])


def _ada_kernel(c_ref, w_ref, b_ref, o_ref):
    c = c_ref[...]
    s = c * _sigmoid(c)
    o_ref[0] = _dot(s, w_ref[0], HI) + b_ref[0]


def _ada(cond8, w_ada, b_ada):
    L, D, N = w_ada.shape
    tn = 1024
    return pl.pallas_call(
        _ada_kernel,
        grid=(L, N // tn),
        in_specs=[pl.BlockSpec((8, D), lambda l, j: (0, 0)),
                  pl.BlockSpec((1, D, tn), lambda l, j: (l, 0, j)),
                  pl.BlockSpec((1, 1, tn), lambda l, j: (l, 0, j))],
        out_specs=pl.BlockSpec((1, 8, tn), lambda l, j: (l, 0, j)),
        out_shape=jax.ShapeDtypeStruct((L, 8, N), F32),
        compiler_params=_cp(("arbitrary", "arbitrary")),
        name="ada",
    )(cond8, w_ada, b_ada.reshape(L, 1, N))


def _inproj_kernel(x_ref, sh_ref, sc_ref, g_ref, w_ref, o_ref, h_scr, *, tm, n_ctx):
    i = pl.program_id(0)

    @pl.when(pl.program_id(1) == 0)
    def _():
        rows = 16

        def chunk(c, carry):
            r0 = pl.multiple_of(c * rows, rows)
            is_ctx = i * tm + r0 < n_ctx
            h = _norm_mod(x_ref[pl.ds(r0, rows), :], g_ref[...], _pick_mod(sh_ref, is_ctx), _pick_mod(sc_ref, is_ctx))
            h_scr[pl.ds(r0, rows), :] = h.astype(BF16)
            return carry

        lax.fori_loop(0, tm // rows, chunk, 0, unroll=4)

    o_ref[...] = _dot(h_scr[...], w_ref[...])


def _inproj(x, mods, l, g, w, n_ctx, tm):
    T, D = x.shape
    N = w.shape[1]
    tn = 1152
    assert N % tn == 0
    return pl.pallas_call(
        functools.partial(_inproj_kernel, tm=tm, n_ctx=n_ctx),
        grid=(T // tm, N // tn),
        in_specs=[pl.BlockSpec((tm, D), lambda i, j: (i, 0)),
                  pl.BlockSpec((1, 8, D), lambda i, j: (l, 0, 0)),
                  pl.BlockSpec((1, 8, D), lambda i, j: (l, 0, 1)),
                  pl.BlockSpec((1, D), lambda i, j: (0, 0)),
                  pl.BlockSpec((D, tn), lambda i, j: (0, j))],
        out_specs=pl.BlockSpec((tm, tn), lambda i, j: (i, j)),
        out_shape=jax.ShapeDtypeStruct((T, N), F32),
        scratch_shapes=[pltpu.VMEM((tm, D), BF16)],
        compiler_params=_cp(("arbitrary", "arbitrary")),
        name="inproj",
    )(x, mods, mods, g, w)


def _softplus(x):
    return jnp.maximum(x, 0.0) + jnp.log(1.0 + jnp.exp(-jnp.abs(x)))


def _rwkv_prep_kernel(p_ref, pp_ref, pn_ref, mu_ref, w0_ref, w2_ref, a0_ref, a2_ref, g2_ref,
                      kkp_ref, kap_ref, rkp_ref, gh_ref,
                      r_ref, v_ref, kk_ref, lwf_ref, kf_ref, bf_ref, lwb_ref, kb_ref, bb_ref,
                      gate_ref, bonus_ref, *, tm, n_ctx, n_tot, W):
    i = pl.program_id(0)
    p = p_ref[...]
    lrow = lax.broadcasted_iota(jnp.int32, (tm, 1), 0)
    grow = i * tm + lrow
    prev = jnp.where(lrow == 0, pp_ref[7:8, :], pltpu.roll(p, 1, axis=0))
    prev = jnp.where((grow == 0) | (grow == n_ctx), 0.0, prev)
    nxt = jnp.where(lrow == tm - 1, pn_ref[0:1, :], pltpu.roll(p, tm - 1, axis=0))
    nxt = jnp.where((grow == n_ctx - 1) | (grow == n_tot - 1), 0.0, nxt)
    ps = p + mu_ref[0:1, :] * (prev - p) + mu_ref[1:2, :] * (nxt - p)

    r = ps[:, 0:W]
    k = ps[:, W:2 * W]
    v = ps[:, 2 * W:3 * W]
    o = 3 * W
    wd = (ps[:, o:o + DECAY_LORA], ps[:, o + DECAY_LORA:o + 2 * DECAY_LORA])
    o += 2 * DECAY_LORA
    ad = (ps[:, o:o + ICLR_LORA], ps[:, o + ICLR_LORA:o + 2 * ICLR_LORA])
    o += 2 * ICLR_LORA
    gd = ps[:, o:o + GATE_LORA]

    gh = gh_ref[...]
    kk = k * kkp_ref[...]
    kk = kk / jnp.maximum(jnp.sqrt(_seg_sum(kk * kk, gh)), 1e-12)
    r_ref[...] = r.astype(r_ref.dtype)
    v_ref[...] = v.astype(v_ref.dtype)
    kk_ref[...] = kk.astype(kk_ref.dtype)
    outs = ((lwf_ref, kf_ref, bf_ref), (lwb_ref, kb_ref, bb_ref))
    for d in range(2):
        z = w0_ref[d:d + 1, :] + _dot3(jnp.tanh(wd[d]), w2_ref[d])
        w_log = -_softplus(-z) - 0.5
        a = _sigmoid(a0_ref[d:d + 1, :] + _dot3(ad[d], a2_ref[d]))
        lw_ref, kd_ref, bd_ref = outs[d]
        lw_ref[...] = -jnp.exp(w_log)
        kd_ref[...] = (k * (1.0 + (a - 1.0) * kap_ref[...])).astype(kd_ref.dtype)
        bd_ref[...] = (kk * a).astype(bd_ref.dtype)
    gate_ref[...] = _dot3(_sigmoid(gd), g2_ref[...]).astype(gate_ref.dtype)
    bonus_ref[...] = (_seg_sum(r * k * rkp_ref[...], gh) * v).astype(bonus_ref.dtype)


def _rwkv_prep(P, prm, gh, n_ctx, tm):
    T = P.shape[0]
    W = prm["kk"].shape[1]
    PW = 2048
    nb8 = T // 8
    row = pl.BlockSpec((tm, W), lambda i: (i, 0))
    full = lambda a: pl.BlockSpec(a.shape, lambda i: (0,) * a.ndim)
    args = (prm["mu"], prm["w0"], prm["w2"], prm["a0"], prm["a2"], prm["g2"], prm["kk"], prm["ka"], prm["rk"], gh)
    return pl.pallas_call(
        functools.partial(_rwkv_prep_kernel, tm=tm, n_ctx=n_ctx, n_tot=T, W=W),
        grid=(T // tm,),
        in_specs=[pl.BlockSpec((tm, PW), lambda i: (i, 0)),
                  pl.BlockSpec((8, PW), lambda i: (jnp.maximum(i * (tm // 8) - 1, 0), 0)),
                  pl.BlockSpec((8, PW), lambda i: (jnp.minimum((i + 1) * (tm // 8), nb8 - 1), 0))]
                 + [full(a) for a in args],
        out_specs=[row] * 11,
        out_shape=[jax.ShapeDtypeStruct((T, W), F32 if i in (3, 6) else BF16) for i in range(11)],
        compiler_params=_cp(("arbitrary",)),
        name="rwkv_prep",
    )(P, P, P, *args)


def _rwkv_scan_kernel(rf, vf, kkf, lwf, kf, bf, rb, vb, kkb, lwb, kb, bb, yf_ref, yb_ref, s_scr, *, C, H, N):
    @pl.when(pl.program_id(0) == 0)
    def _():
        s_scr[...] = jnp.zeros_like(s_scr)

    row = lax.broadcasted_iota(jnp.int32, (C, C), 0)
    col = lax.broadcasted_iota(jnp.int32, (C, C), 1)
    eye = (row == col).astype(F32)
    n_sq = int(np.log2(C)) - 1
    dirs = ((rf, vf, kkf, lwf, kf, bf), (rb, vb, kkb, lwb, kb, bb))
    ch = []
    for d, (r_ref, v_ref, kk_ref, lw_ref, k_ref, b_ref) in enumerate(dirs):
        incl = (col <= row) if d == 0 else (col >= row)
        strict = (col < row) if d == 0 else (col > row)
        lw = lw_ref[...]
        lw_hi = lw.astype(BF16)
        lw_lo = (lw - lw_hi.astype(F32)).astype(BF16)
        tri = incl.astype(BF16)
        cum = _dot(tri, lw_hi) + _dot(tri, lw_lo)
        e_pos = jnp.exp(cum)
        e_neg = jnp.exp(-cum)
        rt = (r_ref[...] * e_pos).astype(BF16)
        at = (-kk_ref[...] * jnp.exp(cum - lw)).astype(BF16)
        bt = (b_ref[...] * e_neg).astype(BF16)
        kt = (k_ref[...] * e_neg).astype(BF16)
        vv = v_ref[...].astype(BF16)
        gam = e_pos[C - 1:C, :] if d == 0 else e_pos[0:1, :]
        for h in range(H):
            sl = slice(h * N, (h + 1) * N)
            ch.append(dict(d=d, h=h, incl=incl, strict=strict, at=at[:, sl], rt=rt[:, sl], bt=bt[:, sl],
                           kt=kt[:, sl], v=vv[:, sl], gam=gam[:, sl], s0=s_scr[d, h]))
    for c in ch:
        a_all = _dot_nt(jnp.concatenate([c["at"], c["rt"]], axis=0), jnp.concatenate([c["bt"], c["kt"]], axis=0))
        c["l_pow"] = jnp.where(c["strict"], a_all[:C, :C], 0.0)
        c["l_ak"] = jnp.where(c["strict"], a_all[:C, C:], 0.0).astype(BF16)
        c["m_rb"] = jnp.where(c["incl"], a_all[C:, :C], 0.0).astype(BF16)
        c["m_rk"] = jnp.where(c["incl"], a_all[C:, C:], 0.0).astype(BF16)
        c["t_inv"] = eye + c["l_pow"]
    for _ in range(n_sq):
        for c in ch:
            lb = c["l_pow"].astype(BF16)
            c["l_pow"] = _dot(lb, lb)
        for c in ch:
            c["t_inv"] = c["t_inv"] + _dot(c["t_inv"].astype(BF16), c["l_pow"].astype(BF16))
    for c in ch:
        c["s0b"] = c["s0"].astype(BF16)
        c["w1"] = _dot_nt(c["at"], c["s0b"]) + _dot(c["l_ak"], c["v"])
    for c in ch:
        c["u"] = _dot(c["t_inv"].astype(BF16), c["w1"].astype(BF16)).astype(BF16)
    for c in ch:
        s_scr[c["d"], c["h"]] = (c["s0"] + _dot_tn(c["u"], c["bt"]) + _dot_tn(c["v"], c["kt"])) * c["gam"]
    for c in ch:
        c["y"] = _dot_nt(c["rt"], c["s0b"]) + _dot(c["m_rb"], c["u"]) + _dot(c["m_rk"], c["v"])
    yf_ref[...] = jnp.concatenate([c["y"] for c in ch[:H]], axis=1)
    yb_ref[...] = jnp.concatenate([c["y"] for c in ch[H:]], axis=1)


def _rwkv_scan(r, v, kk, lwf, kf, bf, lwb, kb, bb, n_ctx):
    T, W = r.shape
    C = SCAN_CHUNK
    H = W // RWKV_HEAD
    nch = T // C
    cch = n_ctx // C
    fwd = pl.BlockSpec((C, W), lambda n: (n, 0))
    bwd = pl.BlockSpec((C, W), lambda n: (jnp.where(n < cch, cch - 1 - n, nch - 1 + cch - n), 0))
    return pl.pallas_call(
        functools.partial(_rwkv_scan_kernel, C=C, H=H, N=RWKV_HEAD),
        grid=(nch,),
        in_specs=[fwd] * 6 + [bwd] * 6,
        out_specs=[fwd, bwd],
        out_shape=[jax.ShapeDtypeStruct((T, W), F32)] * 2,
        scratch_shapes=[pltpu.VMEM((2, H, RWKV_HEAD, RWKV_HEAD), F32)],
        compiler_params=_cp(("arbitrary",)),
        name="rwkv_scan",
    )(r, v, kk, lwf, kf, bf, r, v, kk, lwb, kb, bb)


def _rwkv_post_kernel(yf_ref, yb_ref, bonus_ref, gate_ref, lng_ref, lnb_ref, gh_ref, o_ref):
    gh = gh_ref[...]
    y = yf_ref[...] + yb_ref[...]
    mean = _seg_sum(y, gh) * (1.0 / RWKV_HEAD)
    yc = y - mean
    var = _seg_sum(yc * yc, gh) * (1.0 / RWKV_HEAD)
    yn = yc * lax.rsqrt(var + RWKV_GN_EPS) * lng_ref[...] + lnb_ref[...]
    o_ref[...] = ((yn + bonus_ref[...]) * gate_ref[...]).astype(o_ref.dtype)


def _rwkv_post(yf, yb, bonus, gate, lng, lnb, gh, tm):
    T, W = yf.shape
    row = pl.BlockSpec((tm, W), lambda i: (i, 0))
    full = lambda a: pl.BlockSpec(a.shape, lambda i: (0,) * a.ndim)
    return pl.pallas_call(
        _rwkv_post_kernel,
        grid=(T // tm,),
        in_specs=[row] * 4 + [full(lng), full(lnb), full(gh)],
        out_specs=row,
        out_shape=jax.ShapeDtypeStruct((T, W), BF16),
        compiler_params=_cp(("arbitrary",)),
        name="rwkv_post",
    )(yf, yb, bonus, gate, lng, lnb, gh)


def _gmlp_kernel(p_ref, g_ref, ws_ref, b_ref, o_ref, *, tm, W):
    u = jax.nn.gelu(p_ref[:, 0:W])
    v = jax.nn.gelu(p_ref[:, W:2 * W])
    v = v * lax.rsqrt(jnp.mean(v * v, axis=-1, keepdims=True) + NORM_EPS) * g_ref[...]
    vb = v.astype(BF16)
    gw = W // GM_GROUPS
    for c in range(tm // GM_CHUNK):
        rows = slice(c * GM_CHUNK, (c + 1) * GM_CHUNK)
        parts = [_dot(ws_ref[g], vb[rows, g * gw:(g + 1) * gw]) for g in range(GM_GROUPS)]
        s = jnp.concatenate(parts, axis=1) + b_ref[...]
        o_ref[rows, :] = (u[rows, :] * s).astype(o_ref.dtype)


def _gmlp(P, col_block, g, ws, bias, tm):
    T = P.shape[0]
    W = g.shape[1]
    return pl.pallas_call(
        functools.partial(_gmlp_kernel, tm=tm, W=W),
        grid=(T // tm,),
        in_specs=[pl.BlockSpec((tm, 2 * W), lambda i: (i, col_block)),
                  pl.BlockSpec((1, W), lambda i: (0, 0)),
                  pl.BlockSpec(ws.shape, lambda i: (0, 0, 0)),
                  pl.BlockSpec(bias.shape, lambda i: (0, 0))],
        out_specs=pl.BlockSpec((tm, W), lambda i: (i, 0)),
        out_shape=jax.ShapeDtypeStruct((T, W), BF16),
        compiler_params=_cp(("arbitrary",)),
        name="gmlp",
    )(P, g, ws, bias)


def _rope(x, cos, sin, lane):
    w = x.shape[1]
    partner = jnp.where((lane % 32) < 16, pltpu.roll(x, w - 16, axis=1), pltpu.roll(x, 16, axis=1))
    return x * cos + partner * sin


def _qk_prep_kernel(q_ref, k_ref, cos_ref, sin_ref, qg_ref, kg_ref, ghq_ref, ghk_ref, qo_ref, ko_ref, *, scale):
    cos = cos_ref[...]
    sin = sin_ref[...]
    for x_ref, g_ref, gh_ref, o_ref, mul in ((q_ref, qg_ref, ghq_ref, qo_ref, scale), (k_ref, kg_ref, ghk_ref, ko_ref, 1.0)):
        x = x_ref[...]
        w = x.shape[1]
        ss = _seg_sum(x * x, gh_ref[...]) * (1.0 / ATTN_HEAD)
        xn = x * lax.rsqrt(ss + NORM_EPS) * g_ref[...]
        rep = w // cos.shape[1]
        lane = lax.broadcasted_iota(jnp.int32, x.shape, 1)
        xr = _rope(xn, jnp.tile(cos, (1, rep)), jnp.tile(sin, (1, rep)), lane)
        o_ref[...] = (xr * mul).astype(o_ref.dtype)


def _qk_prep(P, q_block, k_block, cos, sin, qg, kg, ghq, ghk, tm):
    T = P.shape[0]
    QW, KW = qg.shape[1], kg.shape[1]
    full = lambda a: pl.BlockSpec(a.shape, lambda i: (0,) * a.ndim)
    return pl.pallas_call(
        functools.partial(_qk_prep_kernel, scale=ATTN_HEAD ** -0.5),
        grid=(T // tm,),
        in_specs=[pl.BlockSpec((tm, QW), lambda i: (i, q_block)),
                  pl.BlockSpec((tm, KW), lambda i: (i, k_block)),
                  pl.BlockSpec((tm, cos.shape[1]), lambda i: (i, 0)),
                  pl.BlockSpec((tm, sin.shape[1]), lambda i: (i, 0)),
                  full(qg), full(kg), full(ghq), full(ghk)],
        out_specs=[pl.BlockSpec((tm, QW), lambda i: (i, 0)), pl.BlockSpec((tm, KW), lambda i: (i, 0))],
        out_shape=[jax.ShapeDtypeStruct((T, QW), BF16), jax.ShapeDtypeStruct((T, KW), BF16)],
        compiler_params=_cp(("arbitrary",)),
        name="qk_prep",
    )(P, P, cos, sin, qg, kg, ghq, ghk)


def _attn_block(i, nb, sink_ref, q_ref, k_refs, v_refs, o_ref, local):
    L = ATTN_BLOCK
    G = ATTN_GROUP
    hd = ATTN_HEAD
    n_kv = k_refs[-1].shape[1] // hd
    R = G * L
    srow = lax.broadcasted_iota(jnp.int32, (R, 1), 0)
    if local:
        qi = lax.broadcasted_iota(jnp.int32, (R, 3 * L), 0) % L
        kj = lax.broadcasted_iota(jnp.int32, (R, 3 * L), 1)
        rel = kj - L - qi
        valid = (rel <= ATTN_WINDOW) & (rel >= -ATTN_WINDOW)
        valid = valid & ((kj >= L) | (i > 0)) & ((kj < 2 * L) | (i < nb - 1))
    for j in range(n_kv):
        ks = [r[:, j * hd:(j + 1) * hd] for r in k_refs]
        vs = [r[:, j * hd:(j + 1) * hd].astype(BF16) for r in v_refs]
        q = jnp.concatenate([q_ref[:, (j * G + g) * hd:(j * G + g + 1) * hd] for g in range(G)], axis=0)
        sink = jnp.zeros((R, 1), F32)
        for g in range(G):
            sink = jnp.where((srow >= g * L) & (srow < (g + 1) * L), sink_ref[j * G + g], sink)
        s_ctx = _dot_nt(q, ks[-1])
        m = jnp.maximum(jnp.max(s_ctx, axis=1, keepdims=True), sink)
        if local:
            s_loc = _dot_nt(q, jnp.concatenate(ks[:3], axis=0))
            s_loc = jnp.where(valid, s_loc, -1e30)
            m = jnp.maximum(m, jnp.max(s_loc, axis=1, keepdims=True))
            p_loc = jnp.exp(s_loc - m)
        p_ctx = jnp.exp(s_ctx - m)
        den = jnp.sum(p_ctx, axis=1, keepdims=True) + jnp.exp(sink - m)
        acc = _dot(p_ctx.astype(BF16), vs[-1])
        if local:
            den = den + jnp.sum(p_loc, axis=1, keepdims=True)
            acc = acc + _dot(p_loc.astype(BF16), jnp.concatenate(vs[:3], axis=0))
        out = acc / den
        for g in range(G):
            h = j * G + g
            o_ref[:, h * hd:(h + 1) * hd] = out[g * L:(g + 1) * L, :].astype(o_ref.dtype)


def _attn_kernel(sink_ref, q_ref, kp_ref, kc_ref, kn_ref, kx_ref, vp_ref, vc_ref, vn_ref, vx_ref, o_ref, *, cb, nb):
    i = pl.program_id(0)

    @pl.when(i < cb)
    def _():
        _attn_block(i, nb, sink_ref, q_ref, (kx_ref,), (vx_ref,), o_ref, False)

    @pl.when(i >= cb)
    def _():
        _attn_block(i - cb, nb, sink_ref, q_ref, (kp_ref, kc_ref, kn_ref, kx_ref),
                    (vp_ref, vc_ref, vn_ref, vx_ref), o_ref, True)


def _attention(qr, kr, P, v_block, sink, n_ctx):
    T, QW = qr.shape
    KW = kr.shape[1]
    L = ATTN_BLOCK
    cb = n_ctx // L
    nb = (T - n_ctx) // L
    lo, hi = cb, cb + nb - 1
    shifts = (lambda i: jnp.clip(i - 1, lo, hi), lambda i: jnp.clip(i, lo, hi), lambda i: jnp.clip(i + 1, lo, hi))
    kspec = lambda f: pl.BlockSpec((L, KW), lambda i: (f(i), 0))
    vspec = lambda f: pl.BlockSpec((L, KW), lambda i: (f(i), v_block))
    in_specs = ([pl.BlockSpec(memory_space=pltpu.SMEM), pl.BlockSpec((L, QW), lambda i: (i, 0))]
                + [kspec(f) for f in shifts] + [pl.BlockSpec((n_ctx, KW), lambda i: (0, 0))]
                + [vspec(f) for f in shifts] + [pl.BlockSpec((n_ctx, KW), lambda i: (0, v_block))])
    return pl.pallas_call(
        functools.partial(_attn_kernel, cb=cb, nb=nb),
        grid=(cb + nb,),
        in_specs=in_specs,
        out_specs=pl.BlockSpec((L, QW), lambda i: (i, 0)),
        out_shape=jax.ShapeDtypeStruct((T, QW), BF16),
        compiler_params=_cp(("arbitrary",)),
        name="attn",
    )(sink, qr, kr, kr, kr, kr, P, P, P, P)


def _oproj_kernel(x_ref, gt_ref, a_ref, b_ref, c_ref, wa_ref, wb_ref, wc_ref, o_ref, *, tm, n_ctx):
    acc = _dot(a_ref[...], wa_ref[...]) + _dot(b_ref[...], wb_ref[...]) + _dot(c_ref[...], wc_ref[...])
    gate = _pick_mod(gt_ref, _ctx_rows(pl.program_id(0), tm, n_ctx))
    o_ref[...] = x_ref[...] + gate * acc


def _oproj(x, mods, l, y_rw, y_gm, y_at, w_o, n_ctx, tm):
    T, D = x.shape
    tn = 1024
    W1 = y_rw.shape[1]
    W3 = y_at.shape[1]
    gate_col = 2 * (D // tn)
    return pl.pallas_call(
        functools.partial(_oproj_kernel, tm=tm, n_ctx=n_ctx),
        grid=(T // tm, D // tn),
        in_specs=[pl.BlockSpec((tm, tn), lambda i, j: (i, j)),
                  pl.BlockSpec((1, 8, tn), lambda i, j: (l, 0, gate_col + j)),
                  pl.BlockSpec((tm, W1), lambda i, j: (i, 0)),
                  pl.BlockSpec((tm, W1), lambda i, j: (i, 0)),
                  pl.BlockSpec((tm, W3), lambda i, j: (i, 0)),
                  pl.BlockSpec((W1, tn), lambda i, j: (0, j)),
                  pl.BlockSpec((W1, tn), lambda i, j: (1, j)),
                  pl.BlockSpec((W3, tn), lambda i, j: (1, j))],
        out_specs=pl.BlockSpec((tm, tn), lambda i, j: (i, j)),
        out_shape=jax.ShapeDtypeStruct((T, D), F32),
        compiler_params=_cp(("arbitrary", "arbitrary")),
        name="oproj",
    )(x, mods, y_rw, y_gm, y_at, w_o, w_o, w_o)


def _router_kernel(x_ref, sh_ref, sc_ref, g_ref, rt_ref, bias_ref, tri_ref,
                   h_ref, hp_ref, idx_ref, wt_ref, rank_ref, cnt_ref, carry, *, tm, n_ctx):
    i = pl.program_id(0)

    @pl.when(i == 0)
    def _():
        carry[...] = jnp.zeros_like(carry)

    is_ctx = _ctx_rows(i, tm, n_ctx)
    h = _norm_mod(x_ref[...], g_ref[...], _pick_mod(sh_ref, is_ctx), _pick_mod(sc_ref, is_ctx))
    h_ref[...] = h.astype(h_ref.dtype)
    _store_slabs(hp_ref, _pack_halves(h), tm)
    E = N_EXPERTS
    pg = E // N_EXPERT_GROUPS
    neg = -jnp.inf
    scores = _sigmoid(_dot_nt(rt_ref[...], h, HI))
    biased = scores + bias_ref[:, 0:1]
    b3 = biased.reshape(N_EXPERT_GROUPS, pg, tm)
    i3 = lax.broadcasted_iota(jnp.int32, b3.shape, 1)
    m1 = jnp.max(b3, axis=1, keepdims=True)
    first = jnp.min(jnp.where(b3 == m1, i3, pg), axis=1, keepdims=True)
    m2 = jnp.max(jnp.where(i3 == first, neg, b3), axis=1, keepdims=True)
    gs = (m1 + m2).reshape(N_EXPERT_GROUPS, tm)
    gi = lax.broadcasted_iota(jnp.int32, gs.shape, 0)
    gsel = jnp.zeros(gs.shape, jnp.bool_)
    for _ in range(TOPK_GROUPS):
        gm = jnp.max(gs, axis=0, keepdims=True)
        gfirst = jnp.min(jnp.where(gs == gm, gi, N_EXPERT_GROUPS), axis=0, keepdims=True)
        hit = gi == gfirst
        gsel = gsel | hit
        gs = jnp.where(hit, neg, gs)
    masked = jnp.where(gsel.reshape(N_EXPERT_GROUPS, 1, tm), b3, neg).reshape(E, tm)
    ei = lax.broadcasted_iota(jnp.int32, (E, tm), 0)
    sel = jnp.zeros((E, tm), jnp.bool_)
    picks = []
    for _ in range(TOP_K):
        mx = jnp.max(masked, axis=0, keepdims=True)
        efirst = jnp.min(jnp.where(masked == mx, ei, E), axis=0, keepdims=True)
        hit = ei == efirst
        sel = sel | hit
        masked = jnp.where(hit, neg, masked)
        picks.append((efirst, hit, jnp.sum(jnp.where(hit, scores, 0.0), axis=0, keepdims=True)))
    wsum = picks[0][2]
    for pk in picks[1:]:
        wsum = wsum + pk[2]
    self_f = jnp.where(sel, 1.0, 0.0)
    rank_dense = carry[:, 0:1] + _dot(self_f.astype(BF16), tri_ref[...])
    carry[...] = carry[...] + jnp.sum(self_f, axis=1, keepdims=True)
    cnt_ref[...] = carry[...]
    for kx, (efirst, hit, wk) in enumerate(picks):
        idx_ref[kx:kx + 1, :] = efirst
        wt_ref[kx:kx + 1, :] = wk / wsum * ROUTED_SCALE
        rank_ref[kx:kx + 1, :] = jnp.sum(jnp.where(hit, rank_dense, 0.0), axis=0, keepdims=True).astype(jnp.int32)


def _router(x, mods, l, g, router_t, bias, tri, n_ctx, tm):
    T, D = x.shape
    E = router_t.shape[0]
    kspec = pl.BlockSpec((TOP_K, tm), lambda i: (0, i))
    return pl.pallas_call(
        functools.partial(_router_kernel, tm=tm, n_ctx=n_ctx),
        grid=(T // tm,),
        in_specs=[pl.BlockSpec((tm, D), lambda i: (i, 0)),
                  pl.BlockSpec((1, 8, D), lambda i: (l, 0, 3)),
                  pl.BlockSpec((1, 8, D), lambda i: (l, 0, 4)),
                  pl.BlockSpec((1, D), lambda i: (0, 0)),
                  pl.BlockSpec((E, D), lambda i: (0, 0)),
                  pl.BlockSpec((E, 128), lambda i: (0, 0)),
                  pl.BlockSpec((tm, tm), lambda i: (0, 0))],
        out_specs=[pl.BlockSpec((tm, D), lambda i: (i, 0)), pl.BlockSpec((tm * SLAB, 128), lambda i: (i, 0)),
                   kspec, kspec, kspec, pl.BlockSpec((E, 128), lambda i: (0, 0))],
        out_shape=[jax.ShapeDtypeStruct((T, D), BF16),
                   jax.ShapeDtypeStruct((T * SLAB, 128), jnp.uint32),
                   jax.ShapeDtypeStruct((TOP_K, T), jnp.int32),
                   jax.ShapeDtypeStruct((TOP_K, T), F32),
                   jax.ShapeDtypeStruct((TOP_K, T), jnp.int32),
                   jax.ShapeDtypeStruct((E, 128), F32)],
        scratch_shapes=[pltpu.VMEM((E, 128), F32)],
        compiler_params=_cp(("arbitrary",)),
        name="router",
    )(x, mods, mods, g, router_t, bias, tri)


def _gather_slabs(idx_ref, n, src_hbm, dst, sem):
    def body(r, carry):
        src = src_hbm.at[pl.ds(pl.multiple_of(idx_ref[r], SLAB), SLAB), :]
        pltpu.make_async_copy(src, dst.at[pl.ds(r * PITCH, SLAB), :], sem).start()
        return carry
    lax.fori_loop(0, n, body, 0, unroll=8)


def _issue_slabs(idx_ref, lo, hi, src_hbm, dst, sem):
    for r in range(lo, hi):
        src = src_hbm.at[pl.ds(pl.multiple_of(idx_ref[r], SLAB), SLAB), :]
        pltpu.make_async_copy(src, dst.at[pl.ds(r * PITCH, SLAB), :], sem).start(priority=r % 2)


def _wait_slabs(n, src_hbm, dst, sem):
    pltpu.make_async_copy(src_hbm.at[pl.ds(0, n * SLAB), :], dst.at[pl.ds(0, n * SLAB), :], sem).wait()


def _experts_kernel(be_ref, nu_ref, first_ref, wslot_ref, nxe_ref, hasn_ref, *refs, TB, l):
    look = GATHER_SLOTS - 1
    tok0_ref, tokl_ref = refs[:2]
    h_hbm, wg_hbm, wu_hbm, wd_hbm, y_ref, hbuf, sem, wfg, wfu, wfd, wsem = refs[2:]
    b = pl.program_id(0)
    slot = b % GATHER_SLOTS
    slot2 = (b + look) % GATHER_SLOTS
    n_used = nu_ref[0]
    SW = h_hbm.shape[1]

    def weight_copies(e, ws):
        cps = []
        for src, dst in ((wg_hbm, wfg), (wu_hbm, wfu), (wd_hbm, wfd)):
            rows = src.shape[2] // WEIGHT_CHUNKS
            for c in range(WEIGHT_CHUNKS):
                cps.append(pltpu.make_async_copy(src.at[l, e, pl.ds(c * rows, rows)], dst.at[ws, pl.ds(c * rows, rows)],
                                                 wsem.at[ws]))
        return cps

    @pl.when(b == 0)
    def _():
        for j in range(look):
            _gather_slabs(tok0_ref.at[j, 0], TB, h_hbm, hbuf.at[j], sem.at[j])
        for n, cp in enumerate(weight_copies(be_ref[0], 0)):
            cp.start(priority=n % 2)

    @pl.when((first_ref[b] == 1) & (b < n_used))
    def _():
        ws = wslot_ref[b]

        @pl.when(hasn_ref[b] == 1)
        def _():
            for n, cp in enumerate(weight_copies(nxe_ref[b], 1 - ws)):
                cp.start(priority=n % 2)

        for cp in weight_copies(be_ref[b], ws):
            cp.wait()

    @pl.when(b < n_used)
    def _():
        _wait_slabs(TB, h_hbm, hbuf.at[slot], sem.at[slot])
        ws = wslot_ref[b]
        nxt = (tokl_ref.at[0, 0], h_hbm, hbuf.at[slot2], sem.at[slot2])
        burst = TB // (2 * SLAB)
        issued = 0
        half = SLAB * SW
        a = u = None
        for s in range(0, SLAB, 2):
            w0 = hbuf[slot, pl.ds(s, TB, stride=PITCH), :]
            w1 = hbuf[slot, pl.ds(s + 1, TB, stride=PITCH), :]
            for unpack, base in ((_unpack_lo, 0), (_unpack_hi, half)):
                xs = jnp.concatenate([unpack(w0), unpack(w1)], axis=1).astype(BF16)
                k0 = base + s * SW
                da = _dot(xs, wfg[ws, k0:k0 + 2 * SW, :].astype(BF16))
                _issue_slabs(nxt[0], issued, issued + burst, *nxt[1:])
                du = _dot(xs, wfu[ws, k0:k0 + 2 * SW, :].astype(BF16))
                _issue_slabs(nxt[0], issued + burst, issued + 2 * burst, *nxt[1:])
                issued += 2 * burst
                a = da if a is None else a + da
                u = du if u is None else u + du
        act = (a * _sigmoid(a) * u).astype(BF16)
        _store_slabs(y_ref, _pack_halves(_dot(act, wfd[ws].astype(BF16))), TB)

        @pl.when(b + look >= n_used)
        def _():
            _wait_slabs(TB, h_hbm, hbuf.at[slot2], sem.at[slot2])

        for j in range(1, look):
            @pl.when((b == 0) & (n_used <= j))
            def _():
                _wait_slabs(TB, h_hbm, hbuf.at[j], sem.at[j])

    @pl.when(b >= n_used)
    def _():
        y_ref[...] = jnp.zeros_like(y_ref)


def _experts(h2s, tok_rows, tables, wg, wu, wd, l):
    SW = h2s.shape[1]
    D = 2 * SW * SLAB
    nblk = tables[0].shape[0]
    TB = EXPERT_TILE
    FF = wg.shape[3]
    tok3 = tok_rows.reshape(nblk, 1, TB)
    look = GATHER_SLOTS - 1
    grid_spec = pltpu.PrefetchScalarGridSpec(
        num_scalar_prefetch=len(tables),
        grid=(nblk,),
        in_specs=[pl.BlockSpec((look, 1, TB), lambda b, *_: (0, 0, 0), memory_space=pltpu.SMEM),
                  pl.BlockSpec((1, 1, TB), lambda b, *_: (jnp.minimum(b + look, nblk - 1), 0, 0), memory_space=pltpu.SMEM),
                  pl.BlockSpec(memory_space=pl.ANY), pl.BlockSpec(memory_space=pl.ANY),
                  pl.BlockSpec(memory_space=pl.ANY), pl.BlockSpec(memory_space=pl.ANY)],
        out_specs=pl.BlockSpec((TB * SLAB, SW), lambda b, *_: (b, 0)),
        scratch_shapes=[pltpu.VMEM((GATHER_SLOTS, TB * PITCH, SW), jnp.uint32), pltpu.SemaphoreType.DMA((GATHER_SLOTS,)),
                        pltpu.VMEM((2, D, FF), F32), pltpu.VMEM((2, D, FF), F32), pltpu.VMEM((2, FF, D), F32),
                        pltpu.SemaphoreType.DMA((2,))],
    )
    return pl.pallas_call(
        functools.partial(_experts_kernel, TB=TB, l=l),
        grid_spec=grid_spec,
        out_shape=jax.ShapeDtypeStruct((nblk * TB * SLAB, SW), jnp.uint32),
        compiler_params=_cp(("arbitrary",)),
        name="experts",
    )(*tables, tok3, tok3, h2s, wg, wu, wd)


def _combine_kernel(slc_ref, sln_ref, y_hbm, x_ref, h_ref, wt_ref, gt_ref, swg_ref, swu_ref, swd_ref, o_ref,
                    ybuf, sem, *, tm, n_ctx, nt):
    i = pl.program_id(0)
    slot = i % 2
    SW = y_hbm.shape[1]

    @pl.when(i == 0)
    def _():
        for k in range(TOP_K):
            _gather_slabs(slc_ref.at[0, k], tm, y_hbm, ybuf.at[0, k], sem.at[0])

    hb = h_ref[...]
    a = _dot(hb, swg_ref[...])
    u = _dot(hb, swu_ref[...])
    shared = _dot((a * _sigmoid(a) * u).astype(BF16), swd_ref[...])
    gate = _pick_mod(gt_ref, _ctx_rows(i, tm, n_ctx))
    wk = [jnp.broadcast_to(wt_ref[:, k:k + 1], (tm, SW)) for k in range(TOP_K)]
    for k in range(TOP_K):
        _wait_slabs(tm, y_hbm, ybuf.at[slot, k], sem.at[slot])
    half = SLAB * SW
    burst = tm // SLAB
    for s in range(SLAB):
        lo = slice(s * SW, (s + 1) * SW)
        hi = slice(half + s * SW, half + (s + 1) * SW)
        acc_lo = shared[:, lo]
        acc_hi = shared[:, hi]
        for k in range(TOP_K):
            w = ybuf[slot, k, pl.ds(s, tm, stride=PITCH), :]
            acc_lo = acc_lo + wk[k] * _unpack_lo(w)
            acc_hi = acc_hi + wk[k] * _unpack_hi(w)
            _issue_slabs(sln_ref.at[0, k], s * burst, (s + 1) * burst, y_hbm, ybuf.at[1 - slot, k], sem.at[1 - slot])
        o_ref[:, lo] = x_ref[:, lo] + gate[:, lo] * acc_lo
        o_ref[:, hi] = x_ref[:, hi] + gate[:, hi] * acc_hi

    @pl.when(i + 1 >= nt)
    def _():
        for k in range(TOP_K):
            _wait_slabs(tm, y_hbm, ybuf.at[1 - slot, k], sem.at[1 - slot])


def _combine(y2, slot_rows, wt_t, x, h2, mods, l, swg, swu, swd, n_ctx, tm, drop_ctx):
    T, D = x.shape
    SW = y2.shape[1]
    nt = T // tm
    FF = swg.shape[1]
    sl3 = slot_rows.reshape(TOP_K, nt, tm).transpose(1, 0, 2)
    skip = n_ctx // tm if drop_ctx else 0
    return pl.pallas_call(
        functools.partial(_combine_kernel, tm=tm, n_ctx=n_ctx, nt=nt),
        grid=(nt,),
        in_specs=[pl.BlockSpec((1, TOP_K, tm), lambda i: (i, 0, 0), memory_space=pltpu.SMEM),
                  pl.BlockSpec((1, TOP_K, tm), lambda i: (jnp.minimum(i + 1, nt - 1), 0, 0),
                               memory_space=pltpu.SMEM),
                  pl.BlockSpec(memory_space=pl.ANY),
                  pl.BlockSpec((tm, D), lambda i: (i, 0)),
                  pl.BlockSpec((tm, D), lambda i: (i, 0)),
                  pl.BlockSpec((tm, TOP_K), lambda i: (i, 0)),
                  pl.BlockSpec((1, 8, D), lambda i: (l, 0, 5)),
                  pl.BlockSpec((D, FF), lambda i: (0, 0)),
                  pl.BlockSpec((D, FF), lambda i: (0, 0)),
                  pl.BlockSpec((FF, D), lambda i: (0, 0))],
        out_specs=pl.BlockSpec((tm, D), lambda i: (jnp.maximum(i - skip, 0), 0)),
        out_shape=jax.ShapeDtypeStruct((T - skip * tm, D), F32),
        scratch_shapes=[pltpu.VMEM((2, TOP_K, tm * PITCH, SW), jnp.uint32), pltpu.SemaphoreType.DMA((2,))],
        compiler_params=_cp(("arbitrary",)),
        name="combine",
    )(sl3, sl3, y2, x, h2, wt_t, mods, swg, swu, swd)


def _seg_indicator(width, seg):
    idx = np.arange(width) // seg
    return jnp.asarray(idx[:, None] == idx[None, :], dtype=BF16)


def _rope_tables(n_ctx, n_lat):
    rows = n_lat // GRID_W
    row = jnp.repeat(jnp.arange(rows, dtype=F32), GRID_W)
    col = jnp.tile(jnp.arange(GRID_W, dtype=F32), rows)
    n_freq = ATTN_HEAD // 4
    inv_freq = ROPE_THETA ** (-jnp.arange(n_freq, dtype=F32) / n_freq)
    ar, ac = row[:, None] * inv_freq, col[:, None] * inv_freq
    cos = jnp.concatenate([jnp.cos(ar), jnp.cos(ar), jnp.cos(ac), jnp.cos(ac)], axis=1)
    sin = jnp.concatenate([-jnp.sin(ar), jnp.sin(ar), -jnp.sin(ac), jnp.sin(ac)], axis=1)
    cos = jnp.concatenate([jnp.ones((n_ctx, ATTN_HEAD), F32), cos], axis=0)
    sin = jnp.concatenate([jnp.zeros((n_ctx, ATTN_HEAD), F32), sin], axis=0)
    return jnp.tile(cos, (1, 2)), jnp.tile(sin, (1, 2))


def kernel(x, c, ctx, c_ctx, w_ada, b_ada, norm1_g, norm2_g, w_in, w_o, rw_mu, rw_w0, rw_w2, rw_a0, rw_a2, rw_g2, rw_kk, rw_ka, rw_rk, rw_ln_g, rw_ln_b, gm_norm_g, gm_ws, gm_b, at_qn, at_kn, at_sink, moe_router, moe_bias, moe_wg, moe_wu, moe_wd, sh_wg, sh_wu, sh_wd):
    B, S, D = x.shape
    C = ctx.shape[1]
    assert B == 1
    L = w_ada.shape[0]
    T = C + S
    RW = rw_kk.shape[1]
    GW = gm_norm_g.shape[1]
    QW = at_sink.shape[1] * ATTN_HEAD
    KW = QW // ATTN_GROUP
    rw_proj = rw_mu.shape[2]
    RWP = 2048
    assert rw_proj <= RWP and T % 768 == 0 and C % 256 == 0 and S % 256 == 0 and D == 2 * SLAB * 128

    xs = jnp.concatenate([ctx[0], x[0]], axis=0)
    cond8 = jnp.zeros((8, D), F32).at[0].set(c_ctx).at[1].set(c[0])
    mods = _ada(cond8, w_ada, b_ada)

    gh_rw = _seg_indicator(RW, RWKV_HEAD)
    gh_q = _seg_indicator(QW, ATTN_HEAD)
    gh_k = _seg_indicator(KW, ATTN_HEAD)
    cos, sin = _rope_tables(C, S)
    tri = jnp.asarray(np.arange(256)[:, None] < np.arange(256)[None, :], dtype=BF16)
    TB = EXPERT_TILE
    nblk = -(-(T * TOP_K) // TB) + N_EXPERTS

    for l in range(L):
        w_in_l = w_in[l]
        w_in_p = jnp.concatenate([w_in_l[:, :rw_proj], jnp.zeros((D, RWP - rw_proj), F32), w_in_l[:, rw_proj:]],
                                 axis=1).astype(BF16)
        P = _inproj(xs, mods, l, norm1_g[l][None], w_in_p, C, 768)
        gm_block = RWP // (2 * GW)
        q_block = (RWP + 2 * GW) // QW
        k_block = (RWP + 2 * GW + QW) // KW
        v_block = k_block + 1

        pad = lambda a: jnp.pad(a, ((0, 0), (0, RWP - rw_proj)))
        prm = {"mu": pad(rw_mu[l]), "w0": rw_w0[l], "w2": rw_w2[l], "a0": rw_a0[l], "a2": rw_a2[l],
               "g2": rw_g2[l], "kk": rw_kk[l][None], "ka": rw_ka[l][None], "rk": rw_rk[l].reshape(1, RW)}
        r, v, kk, lwf, kf, bf, lwb, kb, bb, gate, bonus = _rwkv_prep(P, prm, gh_rw, C, 256)
        yf, yb = _rwkv_scan(r, v, kk, lwf, kf, bf, lwb, kb, bb, C)
        y_rw = _rwkv_post(yf, yb, bonus, gate, rw_ln_g[l][None], rw_ln_b[l][None], gh_rw, 256)

        gm_bias = jnp.repeat(gm_b[l].T, GW // GM_GROUPS, axis=1)
        y_gm = _gmlp(P, gm_block, gm_norm_g[l][None], gm_ws[l].astype(BF16), gm_bias, 256)

        qg = jnp.tile(at_qn[l], QW // ATTN_HEAD)[None]
        kg = jnp.tile(at_kn[l], KW // ATTN_HEAD)[None]
        qr, kr = _qk_prep(P, q_block, k_block, cos, sin, qg, kg, gh_q, gh_k, 256)
        y_at = _attention(qr, kr, P, v_block, at_sink[l], C)

        xs = _oproj(xs, mods, l, y_rw, y_gm, y_at, w_o[l].astype(BF16), C, 768)

        bias_col = jnp.broadcast_to(moe_bias[l][:, None], (N_EXPERTS, 128))
        h2, h2s, idx8, wt8, rank8, cnt = _router(xs, mods, l, norm2_g[l][None], moe_router[l].T, bias_col, tri, C, 256)
        counts = cnt[:, 0].astype(jnp.int32)
        padded = (counts + TB - 1) // TB * TB
        pad_end = jnp.cumsum(padded)
        pad_start = pad_end - padded
        e_ids = jnp.arange(N_EXPERTS, dtype=jnp.int32)
        slot8 = jnp.sum(jnp.where(idx8[:, :, None] == e_ids, pad_start, 0), axis=-1) + rank8
        tok = jnp.zeros((nblk * TB,), jnp.int32).at[slot8.reshape(-1)].set(
            jnp.tile(jnp.arange(T, dtype=jnp.int32), TOP_K), unique_indices=True)
        blk_pos = jnp.arange(nblk, dtype=jnp.int32) * TB
        block_e = jnp.minimum(jnp.sum((pad_end[None, :] <= blk_pos[:, None]).astype(jnp.int32), axis=1), N_EXPERTS - 1)
        n_used = (pad_end[-1] // TB).astype(jnp.int32)
        first = (blk_pos < pad_end[-1]) & jnp.concatenate([jnp.ones((1,), jnp.bool_), block_e[1:] != block_e[:-1]])
        wslot = (jnp.cumsum(first.astype(jnp.int32)) - 1) % 2
        later = jnp.where((counts[None, :] > 0) & (e_ids[None, :] > e_ids[:, None]), e_ids[None, :], N_EXPERTS)
        nxt_of = jnp.min(later, axis=1)
        nxt_blk = jnp.sum(jnp.where(block_e[:, None] == e_ids, nxt_of, 0), axis=1)
        tables = (block_e, n_used.reshape(1), first.astype(jnp.int32), wslot.astype(jnp.int32),
                  jnp.minimum(nxt_blk, N_EXPERTS - 1).astype(jnp.int32), (nxt_blk < N_EXPERTS).astype(jnp.int32))
        y2 = _experts(h2s, tok * SLAB, tables, moe_wg, moe_wu, moe_wd, l)
        xs = _combine(y2, slot8 * SLAB, wt8.T, xs, h2, mods, l, sh_wg[l].astype(BF16), sh_wu[l].astype(BF16),
                      sh_wd[l].astype(BF16), C, 128, drop_ctx=(l == L - 1))
    return xs.reshape(B, S, D)
```

```python
import functools

import jax
import jax.numpy as jnp
import numpy as np
from jax import lax
from jax.experimental import pallas as pl
from jax.experimental.pallas import tpu as pltpu

F32 = jnp.float32
BF16 = jnp.bfloat16
HI = lax.Precision.HIGHEST

NORM_EPS = 1e-6
GRID_W = 64

RWKV_HEAD = 64
DECAY_LORA = 64
ICLR_LORA = 64
GATE_LORA = 128
RWKV_GN_EPS = 64e-5

GM_GROUPS = 8
GM_CHUNK = 128

ATTN_HEAD = 64
ATTN_GROUP = 4
ATTN_WINDOW = 128
ATTN_BLOCK = 128
ROPE_THETA = 10000.0

N_EXPERTS = 64
TOP_K = 8
N_EXPERT_GROUPS = 8
TOPK_GROUPS = 4
ROUTED_SCALE = 2.5

SCAN_CHUNK = 64
EXPERT_TILE = 256
SLAB = 8
GATHER_SLOTS = 8
WEIGHT_CHUNKS = 4
PITCH = 12
VMEM_LIMIT = 56 * 1024 * 1024


def _cp(sem, vmem=VMEM_LIMIT):
    return pltpu.CompilerParams(dimension_semantics=sem, vmem_limit_bytes=vmem)


def _dot(a, b, prec=None):
    return jnp.dot(a, b, preferred_element_type=F32, precision=prec)


def _dot_nt(a, b, prec=None):
    return lax.dot_general(a, b, (((1,), (1,)), ((), ())), preferred_element_type=F32, precision=prec)


def _dot_tn(a, b, prec=None):
    return lax.dot_general(a, b, (((0,), (0,)), ((), ())), preferred_element_type=F32, precision=prec)


def _dot3(x, w):
    xh = x.astype(BF16)
    xl = (x - xh.astype(F32)).astype(BF16)
    wh = w.astype(BF16)
    wl = (w - wh.astype(F32)).astype(BF16)
    return _dot(xh, wh) + _dot(xl, wh) + _dot(xh, wl)


def _seg_sum(x, g):
    xh = x.astype(BF16)
    xl = (x - xh.astype(F32)).astype(BF16)
    return _dot(xh, g) + _dot(xl, g)


def _seg_sum_cols(x, gc):
    xh = x.astype(BF16)
    xl = (x - xh.astype(F32)).astype(BF16)
    s = _dot(xh, gc) + _dot(xl, gc)
    sh = s.astype(BF16)
    sl = (s - sh.astype(F32)).astype(BF16)
    return _dot_nt(sh, gc) + _dot_nt(sl, gc)


def _sigmoid(x):
    return jax.nn.sigmoid(x)


def _pack_halves(x):
    half = x.shape[1] // 2
    lo = lax.bitcast_convert_type(x[:, :half].astype(BF16).astype(F32), jnp.uint32)
    hi = lax.bitcast_convert_type(x[:, half:].astype(BF16).astype(F32), jnp.uint32)
    return hi | (lo >> 16)


def _unpack_lo(w):
    return lax.bitcast_convert_type(w << 16, F32)


def _unpack_hi(w):
    return lax.bitcast_convert_type(w & jnp.uint32(0xFFFF0000), F32)


def _store_slabs(ref, words, n):
    for c in range(SLAB):
        ref[pl.ds(c, n, stride=SLAB), :] = words[:, c * 128:(c + 1) * 128]


def _norm_mod(x, g, sh, sc):
    y = x * lax.rsqrt(jnp.mean(x * x, axis=-1, keepdims=True) + NORM_EPS)
    return (y * g) * (1.0 + sc) + sh


def _ctx_rows(i, tm, n_ctx):
    row = i * tm + lax.broadcasted_iota(jnp.int32, (tm, 1), 0)
    return row < n_ctx


def _pick_mod(mod_ref, is_ctx):
    return jnp.where(is_ctx, mod_ref[0, 0:1, :], mod_ref[0, 1:2, :])


def _ada_kernel(c_ref, w_ref, b_ref, o_ref):
    c = c_ref[...]
    s = c * _sigmoid(c)
    o_ref[0] = _dot3(s, w_ref[0]) + b_ref[0]


def _ada(cond8, w_ada, b_ada):
    L, D, N = w_ada.shape
    tn = 1024
    return pl.pallas_call(
        _ada_kernel,
        grid=(L, N // tn),
        in_specs=[pl.BlockSpec((8, D), lambda l, j: (0, 0)),
                  pl.BlockSpec((1, D, tn), lambda l, j: (l, 0, j)),
                  pl.BlockSpec((1, 1, tn), lambda l, j: (l, 0, j))],
        out_specs=pl.BlockSpec((1, 8, tn), lambda l, j: (l, 0, j)),
        out_shape=jax.ShapeDtypeStruct((L, 8, N), F32),
        compiler_params=_cp(("arbitrary", "arbitrary")),
        name="ada",
    )(cond8, w_ada, b_ada.reshape(L, 1, N))


def _inproj_kernel(x_ref, sh_ref, sc_ref, g_ref, w_ref, o_ref, h_scr, *, tm, n_ctx):
    i = pl.program_id(0)

    @pl.when(pl.program_id(1) == 0)
    def _():
        rows = 16

        def chunk(c, carry):
            r0 = pl.multiple_of(c * rows, rows)
            is_ctx = i * tm + r0 < n_ctx
            h = _norm_mod(x_ref[pl.ds(r0, rows), :], g_ref[...], _pick_mod(sh_ref, is_ctx), _pick_mod(sc_ref, is_ctx))
            h_scr[pl.ds(r0, rows), :] = h.astype(BF16)
            return carry

        lax.fori_loop(0, tm // rows, chunk, 0, unroll=4)

    o_ref[...] = _dot(h_scr[...], w_ref[...])


def _inproj(x, mods, l, g, w, n_ctx, tm):
    T, D = x.shape
    N = w.shape[1]
    tn = 1152
    assert N % tn == 0
    return pl.pallas_call(
        functools.partial(_inproj_kernel, tm=tm, n_ctx=n_ctx),
        grid=(T // tm, N // tn),
        in_specs=[pl.BlockSpec((tm, D), lambda i, j: (i, 0)),
                  pl.BlockSpec((1, 8, D), lambda i, j: (l, 0, 0)),
                  pl.BlockSpec((1, 8, D), lambda i, j: (l, 0, 1)),
                  pl.BlockSpec((1, D), lambda i, j: (0, 0)),
                  pl.BlockSpec((D, tn), lambda i, j: (0, j))],
        out_specs=pl.BlockSpec((tm, tn), lambda i, j: (i, j)),
        out_shape=jax.ShapeDtypeStruct((T, N), F32),
        scratch_shapes=[pltpu.VMEM((tm, D), BF16)],
        compiler_params=_cp(("arbitrary", "arbitrary")),
        name="inproj",
    )(x, mods, mods, g, w)


def _softplus(x):
    return jnp.maximum(x, 0.0) + jnp.log(1.0 + jnp.exp(-jnp.abs(x)))


def _rwkv_prep_kernel(p_ref, pp_ref, pn_ref, mu_ref, w0_ref, w2_ref, a0_ref, a2_ref, g2_ref,
                      kkp_ref, kap_ref, rkp_ref, gh_ref,
                      r_ref, v_ref, kk_ref, lwf_ref, kf_ref, bf_ref, lwb_ref, kb_ref, bb_ref,
                      gate_ref, bonus_ref, *, tm, n_ctx, n_tot, W):
    i = pl.program_id(0)
    p = p_ref[...]
    lrow = lax.broadcasted_iota(jnp.int32, (tm, 1), 0)
    grow = i * tm + lrow
    prev = jnp.where(lrow == 0, pp_ref[7:8, :], pltpu.roll(p, 1, axis=0))
    prev = jnp.where((grow == 0) | (grow == n_ctx), 0.0, prev)
    nxt = jnp.where(lrow == tm - 1, pn_ref[0:1, :], pltpu.roll(p, tm - 1, axis=0))
    nxt = jnp.where((grow == n_ctx - 1) | (grow == n_tot - 1), 0.0, nxt)
    ps = p + mu_ref[0:1, :] * (prev - p) + mu_ref[1:2, :] * (nxt - p)

    r = ps[:, 0:W]
    k = ps[:, W:2 * W]
    v = ps[:, 2 * W:3 * W]
    o = 3 * W
    wd = (ps[:, o:o + DECAY_LORA], ps[:, o + DECAY_LORA:o + 2 * DECAY_LORA])
    o += 2 * DECAY_LORA
    ad = (ps[:, o:o + ICLR_LORA], ps[:, o + ICLR_LORA:o + 2 * ICLR_LORA])
    o += 2 * ICLR_LORA
    gd = ps[:, o:o + GATE_LORA]

    gh = gh_ref[...]
    kk = k * kkp_ref[...]
    kk = kk / jnp.maximum(jnp.sqrt(_seg_sum(kk * kk, gh)), 1e-12)
    r_ref[...] = r.astype(r_ref.dtype)
    v_ref[...] = v.astype(v_ref.dtype)
    kk_ref[...] = kk.astype(kk_ref.dtype)
    outs = ((lwf_ref, kf_ref, bf_ref), (lwb_ref, kb_ref, bb_ref))
    for d in range(2):
        z = w0_ref[d:d + 1, :] + _dot3(jnp.tanh(wd[d]), w2_ref[d])
        w_log = -_softplus(-z) - 0.5
        a = _sigmoid(a0_ref[d:d + 1, :] + _dot3(ad[d], a2_ref[d]))
        lw_ref, kd_ref, bd_ref = outs[d]
        lw_ref[...] = -jnp.exp(w_log)
        kd_ref[...] = (k * (1.0 + (a - 1.0) * kap_ref[...])).astype(kd_ref.dtype)
        bd_ref[...] = (kk * a).astype(bd_ref.dtype)
    gate_ref[...] = _dot3(_sigmoid(gd), g2_ref[...]).astype(gate_ref.dtype)
    bonus_ref[...] = (_seg_sum(r * k * rkp_ref[...], gh) * v).astype(bonus_ref.dtype)


def _rwkv_prep(P, prm, gh, n_ctx, tm):
    T = P.shape[0]
    W = prm["kk"].shape[1]
    PW = 2048
    nb8 = T // 8
    row = pl.BlockSpec((tm, W), lambda i: (i, 0))
    full = lambda a: pl.BlockSpec(a.shape, lambda i: (0,) * a.ndim)
    args = (prm["mu"], prm["w0"], prm["w2"], prm["a0"], prm["a2"], prm["g2"], prm["kk"], prm["ka"], prm["rk"], gh)
    return pl.pallas_call(
        functools.partial(_rwkv_prep_kernel, tm=tm, n_ctx=n_ctx, n_tot=T, W=W),
        grid=(T // tm,),
        in_specs=[pl.BlockSpec((tm, PW), lambda i: (i, 0)),
                  pl.BlockSpec((8, PW), lambda i: (jnp.maximum(i * (tm // 8) - 1, 0), 0)),
                  pl.BlockSpec((8, PW), lambda i: (jnp.minimum((i + 1) * (tm // 8), nb8 - 1), 0))]
                 + [full(a) for a in args],
        out_specs=[row] * 11,
        out_shape=[jax.ShapeDtypeStruct((T, W), F32 if i in (3, 6) else BF16) for i in range(11)],
        compiler_params=_cp(("arbitrary",)),
        name="rwkv_prep",
    )(P, P, P, *args)


def _rwkv_scan_kernel(rf, vf, kkf, lwf, kf, bf, rb, vb, kkb, lwb, kb, bb, yf_ref, yb_ref, s_scr, *, C, H, N):
    @pl.when(pl.program_id(0) == 0)
    def _():
        s_scr[...] = jnp.zeros_like(s_scr)

    row = lax.broadcasted_iota(jnp.int32, (C, C), 0)
    col = lax.broadcasted_iota(jnp.int32, (C, C), 1)
    eye = (row == col).astype(F32)
    n_sq = int(np.log2(C)) - 1
    dirs = ((rf, vf, kkf, lwf, kf, bf), (rb, vb, kkb, lwb, kb, bb))
    ch = []
    for d, (r_ref, v_ref, kk_ref, lw_ref, k_ref, b_ref) in enumerate(dirs):
        incl = (col <= row) if d == 0 else (col >= row)
        strict = (col < row) if d == 0 else (col > row)
        lw = lw_ref[...]
        lw_hi = lw.astype(BF16)
        lw_lo = (lw - lw_hi.astype(F32)).astype(BF16)
        tri = incl.astype(BF16)
        cum = _dot(tri, lw_hi) + _dot(tri, lw_lo)
        e_pos = jnp.exp(cum)
        e_neg = jnp.exp(-cum)
        rt = (r_ref[...] * e_pos).astype(BF16)
        at = (-kk_ref[...] * jnp.exp(cum - lw)).astype(BF16)
        bt = (b_ref[...] * e_neg).astype(BF16)
        kt = (k_ref[...] * e_neg).astype(BF16)
        vv = v_ref[...].astype(BF16)
        gam = e_pos[C - 1:C, :] if d == 0 else e_pos[0:1, :]
        for h in range(H):
            sl = slice(h * N, (h + 1) * N)
            ch.append(dict(d=d, h=h, incl=incl, strict=strict, at=at[:, sl], rt=rt[:, sl], bt=bt[:, sl],
                           kt=kt[:, sl], v=vv[:, sl], gam=gam[:, sl], s0=s_scr[d, h]))
    for c in ch:
        a_all = _dot_nt(jnp.concatenate([c["at"], c["rt"]], axis=0), jnp.concatenate([c["bt"], c["kt"]], axis=0))
        c["l_pow"] = jnp.where(c["strict"], a_all[:C, :C], 0.0)
        c["l_ak"] = jnp.where(c["strict"], a_all[:C, C:], 0.0).astype(BF16)
        c["m_rb"] = jnp.where(c["incl"], a_all[C:, :C], 0.0).astype(BF16)
        c["m_rk"] = jnp.where(c["incl"], a_all[C:, C:], 0.0).astype(BF16)
        c["t_inv"] = eye + c["l_pow"]
    for _ in range(n_sq):
        for c in ch:
            lb = c["l_pow"].astype(BF16)
            c["l_pow"] = _dot(lb, lb)
        for c in ch:
            c["t_inv"] = c["t_inv"] + _dot(c["t_inv"].astype(BF16), c["l_pow"].astype(BF16))
    for c in ch:
        c["s0b"] = c["s0"].astype(BF16)
        c["w1"] = _dot_nt(c["at"], c["s0b"]) + _dot(c["l_ak"], c["v"])
    for c in ch:
        c["u"] = _dot(c["t_inv"].astype(BF16), c["w1"].astype(BF16)).astype(BF16)
    for c in ch:
        s_scr[c["d"], c["h"]] = (c["s0"] + _dot_tn(c["u"], c["bt"]) + _dot_tn(c["v"], c["kt"])) * c["gam"]
    for c in ch:
        c["y"] = _dot_nt(c["rt"], c["s0b"]) + _dot(c["m_rb"], c["u"]) + _dot(c["m_rk"], c["v"])
    yf_ref[...] = jnp.concatenate([c["y"] for c in ch[:H]], axis=1)
    yb_ref[...] = jnp.concatenate([c["y"] for c in ch[H:]], axis=1)


def _rwkv_scan(r, v, kk, lwf, kf, bf, lwb, kb, bb, n_ctx):
    T, W = r.shape
    C = SCAN_CHUNK
    H = W // RWKV_HEAD
    nch = T // C
    cch = n_ctx // C
    fwd = pl.BlockSpec((C, W), lambda n: (n, 0))
    bwd = pl.BlockSpec((C, W), lambda n: (jnp.where(n < cch, cch - 1 - n, nch - 1 + cch - n), 0))
    return pl.pallas_call(
        functools.partial(_rwkv_scan_kernel, C=C, H=H, N=RWKV_HEAD),
        grid=(nch,),
        in_specs=[fwd] * 6 + [bwd] * 6,
        out_specs=[fwd, bwd],
        out_shape=[jax.ShapeDtypeStruct((T, W), F32)] * 2,
        scratch_shapes=[pltpu.VMEM((2, H, RWKV_HEAD, RWKV_HEAD), F32)],
        compiler_params=_cp(("arbitrary",)),
        name="rwkv_scan",
    )(r, v, kk, lwf, kf, bf, r, v, kk, lwb, kb, bb)


def _rwkv_post_kernel(yf_ref, yb_ref, bonus_ref, gate_ref, lng_ref, lnb_ref, gh_ref, o_ref):
    gh = gh_ref[...]
    y = yf_ref[...] + yb_ref[...]
    mean = _seg_sum(y, gh) * (1.0 / RWKV_HEAD)
    yc = y - mean
    var = _seg_sum(yc * yc, gh) * (1.0 / RWKV_HEAD)
    yn = yc * lax.rsqrt(var + RWKV_GN_EPS) * lng_ref[...] + lnb_ref[...]
    o_ref[...] = ((yn + bonus_ref[...]) * gate_ref[...]).astype(o_ref.dtype)


def _rwkv_post(yf, yb, bonus, gate, lng, lnb, gh, tm):
    T, W = yf.shape
    row = pl.BlockSpec((tm, W), lambda i: (i, 0))
    full = lambda a: pl.BlockSpec(a.shape, lambda i: (0,) * a.ndim)
    return pl.pallas_call(
        _rwkv_post_kernel,
        grid=(T // tm,),
        in_specs=[row] * 4 + [full(lng), full(lnb), full(gh)],
        out_specs=row,
        out_shape=jax.ShapeDtypeStruct((T, W), BF16),
        compiler_params=_cp(("arbitrary",)),
        name="rwkv_post",
    )(yf, yb, bonus, gate, lng, lnb, gh)


def _gmlp_kernel(p_ref, g_ref, ws_ref, b_ref, o_ref, *, tm, W):
    u = jax.nn.gelu(p_ref[:, 0:W])
    v = jax.nn.gelu(p_ref[:, W:2 * W])
    v = v * lax.rsqrt(jnp.mean(v * v, axis=-1, keepdims=True) + NORM_EPS) * g_ref[...]
    vb = v.astype(BF16)
    gw = W // GM_GROUPS
    for c in range(tm // GM_CHUNK):
        rows = slice(c * GM_CHUNK, (c + 1) * GM_CHUNK)
        parts = [_dot(ws_ref[g], vb[rows, g * gw:(g + 1) * gw]) for g in range(GM_GROUPS)]
        s = jnp.concatenate(parts, axis=1) + b_ref[...]
        o_ref[rows, :] = (u[rows, :] * s).astype(o_ref.dtype)


def _gmlp(P, col_block, g, ws, bias, tm):
    T = P.shape[0]
    W = g.shape[1]
    return pl.pallas_call(
        functools.partial(_gmlp_kernel, tm=tm, W=W),
        grid=(T // tm,),
        in_specs=[pl.BlockSpec((tm, 2 * W), lambda i: (i, col_block)),
                  pl.BlockSpec((1, W), lambda i: (0, 0)),
                  pl.BlockSpec(ws.shape, lambda i: (0, 0, 0)),
                  pl.BlockSpec(bias.shape, lambda i: (0, 0))],
        out_specs=pl.BlockSpec((tm, W), lambda i: (i, 0)),
        out_shape=jax.ShapeDtypeStruct((T, W), BF16),
        compiler_params=_cp(("arbitrary",)),
        name="gmlp",
    )(P, g, ws, bias)


def _rope(x, cos, sin, lane):
    w = x.shape[1]
    partner = jnp.where((lane % 32) < 16, pltpu.roll(x, w - 16, axis=1), pltpu.roll(x, 16, axis=1))
    return x * cos + partner * sin


def _qk_prep_kernel(q_ref, k_ref, cos_ref, sin_ref, qg_ref, kg_ref, ghq_ref, ghk_ref, qo_ref, ko_ref, *, scale):
    cos = cos_ref[...]
    sin = sin_ref[...]
    for x_ref, g_ref, gh_ref, o_ref, mul in ((q_ref, qg_ref, ghq_ref, qo_ref, scale), (k_ref, kg_ref, ghk_ref, ko_ref, 1.0)):
        x = x_ref[...]
        w = x.shape[1]
        ss = _seg_sum_cols(x * x, gh_ref[...]) * (1.0 / ATTN_HEAD)
        xn = x * lax.rsqrt(ss + NORM_EPS) * g_ref[...]
        rep = w // cos.shape[1]
        lane = lax.broadcasted_iota(jnp.int32, x.shape, 1)
        xr = _rope(xn, jnp.tile(cos, (1, rep)), jnp.tile(sin, (1, rep)), lane)
        o_ref[...] = (xr * mul).astype(o_ref.dtype)


def _qk_prep(P, q_block, k_block, cos, sin, qg, kg, ghq, ghk, tm):
    T = P.shape[0]
    QW, KW = qg.shape[1], kg.shape[1]
    full = lambda a: pl.BlockSpec(a.shape, lambda i: (0,) * a.ndim)
    return pl.pallas_call(
        functools.partial(_qk_prep_kernel, scale=ATTN_HEAD ** -0.5),
        grid=(T // tm,),
        in_specs=[pl.BlockSpec((tm, QW), lambda i: (i, q_block)),
                  pl.BlockSpec((tm, KW), lambda i: (i, k_block)),
                  pl.BlockSpec((tm, cos.shape[1]), lambda i: (i, 0)),
                  pl.BlockSpec((tm, sin.shape[1]), lambda i: (i, 0)),
                  full(qg), full(kg), full(ghq), full(ghk)],
        out_specs=[pl.BlockSpec((tm, QW), lambda i: (i, 0)), pl.BlockSpec((tm, KW), lambda i: (i, 0))],
        out_shape=[jax.ShapeDtypeStruct((T, QW), BF16), jax.ShapeDtypeStruct((T, KW), BF16)],
        compiler_params=_cp(("arbitrary",)),
        name="qk_prep",
    )(P, P, cos, sin, qg, kg, ghq, ghk)


def _attn_block(i, nb, sink_ref, q_ref, k_refs, v_refs, o_ref, local):
    L = ATTN_BLOCK
    G = ATTN_GROUP
    hd = ATTN_HEAD
    n_kv = k_refs[-1].shape[1] // hd
    R = G * L
    srow = lax.broadcasted_iota(jnp.int32, (R, 1), 0)
    if local:
        qi = lax.broadcasted_iota(jnp.int32, (R, 3 * L), 0) % L
        kj = lax.broadcasted_iota(jnp.int32, (R, 3 * L), 1)
        rel = kj - L - qi
        valid = (rel <= ATTN_WINDOW) & (rel >= -ATTN_WINDOW)
        valid = valid & ((kj >= L) | (i > 0)) & ((kj < 2 * L) | (i < nb - 1))
    for j in range(n_kv):
        ks = [r[:, j * hd:(j + 1) * hd] for r in k_refs]
        vs = [r[:, j * hd:(j + 1) * hd].astype(BF16) for r in v_refs]
        q = jnp.concatenate([q_ref[:, (j * G + g) * hd:(j * G + g + 1) * hd] for g in range(G)], axis=0)
        sink = jnp.zeros((R, 1), F32)
        for g in range(G):
            sink = jnp.where((srow >= g * L) & (srow < (g + 1) * L), sink_ref[j * G + g], sink)
        s_ctx = _dot_nt(q, ks[-1])
        m = jnp.maximum(jnp.max(s_ctx, axis=1, keepdims=True), sink)
        if local:
            s_loc = _dot_nt(q, jnp.concatenate(ks[:3], axis=0))
            s_loc = jnp.where(valid, s_loc, -1e30)
            m = jnp.maximum(m, jnp.max(s_loc, axis=1, keepdims=True))
            p_loc = jnp.exp(s_loc - m)
        p_ctx = jnp.exp(s_ctx - m)
        den = jnp.sum(p_ctx, axis=1, keepdims=True) + jnp.exp(sink - m)
        acc = _dot(p_ctx.astype(BF16), vs[-1])
        if local:
            den = den + jnp.sum(p_loc, axis=1, keepdims=True)
            acc = acc + _dot(p_loc.astype(BF16), jnp.concatenate(vs[:3], axis=0))
        out = acc / den
        for g in range(G):
            h = j * G + g
            o_ref[:, h * hd:(h + 1) * hd] = out[g * L:(g + 1) * L, :].astype(o_ref.dtype)


def _attn_kernel(sink_ref, q_ref, kp_ref, kc_ref, kn_ref, kx_ref, vp_ref, vc_ref, vn_ref, vx_ref, o_ref, *, cb, nb):
    i = pl.program_id(0)

    @pl.when(i < cb)
    def _():
        _attn_block(i, nb, sink_ref, q_ref, (kx_ref,), (vx_ref,), o_ref, False)

    @pl.when(i >= cb)
    def _():
        _attn_block(i - cb, nb, sink_ref, q_ref, (kp_ref, kc_ref, kn_ref, kx_ref),
                    (vp_ref, vc_ref, vn_ref, vx_ref), o_ref, True)


def _attention(qr, kr, P, v_block, sink, n_ctx):
    T, QW = qr.shape
    KW = kr.shape[1]
    L = ATTN_BLOCK
    cb = n_ctx // L
    nb = (T - n_ctx) // L
    lo, hi = cb, cb + nb - 1
    shifts = (lambda i: jnp.clip(i - 1, lo, hi), lambda i: jnp.clip(i, lo, hi), lambda i: jnp.clip(i + 1, lo, hi))
    kspec = lambda f: pl.BlockSpec((L, KW), lambda i: (f(i), 0))
    vspec = lambda f: pl.BlockSpec((L, KW), lambda i: (f(i), v_block))
    in_specs = ([pl.BlockSpec(memory_space=pltpu.SMEM), pl.BlockSpec((L, QW), lambda i: (i, 0))]
                + [kspec(f) for f in shifts] + [pl.BlockSpec((n_ctx, KW), lambda i: (0, 0))]
                + [vspec(f) for f in shifts] + [pl.BlockSpec((n_ctx, KW), lambda i: (0, v_block))])
    return pl.pallas_call(
        functools.partial(_attn_kernel, cb=cb, nb=nb),
        grid=(cb + nb,),
        in_specs=in_specs,
        out_specs=pl.BlockSpec((L, QW), lambda i: (i, 0)),
        out_shape=jax.ShapeDtypeStruct((T, QW), BF16),
        compiler_params=_cp(("arbitrary",)),
        name="attn",
    )(sink, qr, kr, kr, kr, kr, P, P, P, P)


def _oproj_kernel(x_ref, gt_ref, a_ref, b_ref, c_ref, wa_ref, wb_ref, wc_ref, o_ref, *, tm, n_ctx):
    acc = _dot(a_ref[...], wa_ref[...]) + _dot(b_ref[...], wb_ref[...]) + _dot(c_ref[...], wc_ref[...])
    gate = _pick_mod(gt_ref, _ctx_rows(pl.program_id(0), tm, n_ctx))
    o_ref[...] = x_ref[...] + gate * acc


def _oproj(x, mods, l, y_rw, y_gm, y_at, w_o, n_ctx, tm):
    T, D = x.shape
    tn = 1024
    W1 = y_rw.shape[1]
    W3 = y_at.shape[1]
    gate_col = 2 * (D // tn)
    return pl.pallas_call(
        functools.partial(_oproj_kernel, tm=tm, n_ctx=n_ctx),
        grid=(T // tm, D // tn),
        in_specs=[pl.BlockSpec((tm, tn), lambda i, j: (i, j)),
                  pl.BlockSpec((1, 8, tn), lambda i, j: (l, 0, gate_col + j)),
                  pl.BlockSpec((tm, W1), lambda i, j: (i, 0)),
                  pl.BlockSpec((tm, W1), lambda i, j: (i, 0)),
                  pl.BlockSpec((tm, W3), lambda i, j: (i, 0)),
                  pl.BlockSpec((W1, tn), lambda i, j: (0, j)),
                  pl.BlockSpec((W1, tn), lambda i, j: (1, j)),
                  pl.BlockSpec((W3, tn), lambda i, j: (1, j))],
        out_specs=pl.BlockSpec((tm, tn), lambda i, j: (i, j)),
        out_shape=jax.ShapeDtypeStruct((T, D), F32),
        compiler_params=_cp(("arbitrary", "arbitrary")),
        name="oproj",
    )(x, mods, y_rw, y_gm, y_at, w_o, w_o, w_o)


def _router_kernel(x_ref, sh_ref, sc_ref, g_ref, rt_ref, bias_ref, tri_ref,
                   h_ref, hp_ref, idx_ref, wt_ref, rank_ref, cnt_ref, carry, *, tm, n_ctx):
    i = pl.program_id(0)

    @pl.when(i == 0)
    def _():
        carry[...] = jnp.zeros_like(carry)

    is_ctx = _ctx_rows(i, tm, n_ctx)
    h = _norm_mod(x_ref[...], g_ref[...], _pick_mod(sh_ref, is_ctx), _pick_mod(sc_ref, is_ctx))
    h_ref[...] = h.astype(h_ref.dtype)
    _store_slabs(hp_ref, _pack_halves(h), tm)
    E = N_EXPERTS
    pg = E // N_EXPERT_GROUPS
    neg = -jnp.inf
    scores = _sigmoid(_dot_nt(rt_ref[...], h, HI))
    biased = scores + bias_ref[:, 0:1]
    b3 = biased.reshape(N_EXPERT_GROUPS, pg, tm)
    i3 = lax.broadcasted_iota(jnp.int32, b3.shape, 1)
    m1 = jnp.max(b3, axis=1, keepdims=True)
    first = jnp.min(jnp.where(b3 == m1, i3, pg), axis=1, keepdims=True)
    m2 = jnp.max(jnp.where(i3 == first, neg, b3), axis=1, keepdims=True)
    gs = (m1 + m2).reshape(N_EXPERT_GROUPS, tm)
    gi = lax.broadcasted_iota(jnp.int32, gs.shape, 0)
    gsel = jnp.zeros(gs.shape, jnp.bool_)
    for _ in range(TOPK_GROUPS):
        gm = jnp.max(gs, axis=0, keepdims=True)
        gfirst = jnp.min(jnp.where(gs == gm, gi, N_EXPERT_GROUPS), axis=0, keepdims=True)
        hit = gi == gfirst
        gsel = gsel | hit
        gs = jnp.where(hit, neg, gs)
    masked = jnp.where(gsel.reshape(N_EXPERT_GROUPS, 1, tm), b3, neg).reshape(E, tm)
    ei = lax.broadcasted_iota(jnp.int32, (E, tm), 0)
    sel = jnp.zeros((E, tm), jnp.bool_)
    picks = []
    for _ in range(TOP_K):
        mx = jnp.max(masked, axis=0, keepdims=True)
        efirst = jnp.min(jnp.where(masked == mx, ei, E), axis=0, keepdims=True)
        hit = ei == efirst
        sel = sel | hit
        masked = jnp.where(hit, neg, masked)
        picks.append((efirst, hit, jnp.sum(jnp.where(hit, scores, 0.0), axis=0, keepdims=True)))
    wsum = picks[0][2]
    for pk in picks[1:]:
        wsum = wsum + pk[2]
    self_f = jnp.where(sel, 1.0, 0.0)
    rank_dense = carry[:, 0:1] + _dot(self_f.astype(BF16), tri_ref[...])
    carry[...] = carry[...] + jnp.sum(self_f, axis=1, keepdims=True)
    cnt_ref[...] = carry[...]
    for kx, (efirst, hit, wk) in enumerate(picks):
        idx_ref[kx:kx + 1, :] = efirst
        wt_ref[kx:kx + 1, :] = wk / wsum * ROUTED_SCALE
        rank_ref[kx:kx + 1, :] = jnp.sum(jnp.where(hit, rank_dense, 0.0), axis=0, keepdims=True).astype(jnp.int32)


def _router(x, mods, l, g, router_t, bias, tri, n_ctx, tm):
    T, D = x.shape
    E = router_t.shape[0]
    kspec = pl.BlockSpec((TOP_K, tm), lambda i: (0, i))
    return pl.pallas_call(
        functools.partial(_router_kernel, tm=tm, n_ctx=n_ctx),
        grid=(T // tm,),
        in_specs=[pl.BlockSpec((tm, D), lambda i: (i, 0)),
                  pl.BlockSpec((1, 8, D), lambda i: (l, 0, 3)),
                  pl.BlockSpec((1, 8, D), lambda i: (l, 0, 4)),
                  pl.BlockSpec((1, D), lambda i: (0, 0)),
                  pl.BlockSpec((E, D), lambda i: (0, 0)),
                  pl.BlockSpec((E, 128), lambda i: (0, 0)),
                  pl.BlockSpec((tm, tm), lambda i: (0, 0))],
        out_specs=[pl.BlockSpec((tm, D), lambda i: (i, 0)), pl.BlockSpec((tm * SLAB, 128), lambda i: (i, 0)),
                   kspec, kspec, kspec, pl.BlockSpec((E, 128), lambda i: (0, 0))],
        out_shape=[jax.ShapeDtypeStruct((T, D), BF16),
                   jax.ShapeDtypeStruct((T * SLAB, 128), jnp.uint32),
                   jax.ShapeDtypeStruct((TOP_K, T), jnp.int32),
                   jax.ShapeDtypeStruct((TOP_K, T), F32),
                   jax.ShapeDtypeStruct((TOP_K, T), jnp.int32),
                   jax.ShapeDtypeStruct((E, 128), F32)],
        scratch_shapes=[pltpu.VMEM((E, 128), F32)],
        compiler_params=_cp(("arbitrary",)),
        name="router",
    )(x, mods, mods, g, router_t, bias, tri)


def _gather_slabs(idx_ref, n, src_hbm, dst, sem):
    def body(r, carry):
        src = src_hbm.at[pl.ds(pl.multiple_of(idx_ref[r], SLAB), SLAB), :]
        pltpu.make_async_copy(src, dst.at[pl.ds(r * PITCH, SLAB), :], sem).start()
        return carry
    lax.fori_loop(0, n, body, 0, unroll=8)


def _issue_slabs(idx_ref, lo, hi, src_hbm, dst, sem):
    for r in range(lo, hi):
        src = src_hbm.at[pl.ds(pl.multiple_of(idx_ref[r], SLAB), SLAB), :]
        pltpu.make_async_copy(src, dst.at[pl.ds(r * PITCH, SLAB), :], sem).start(priority=r % 2)


def _wait_slabs(n, src_hbm, dst, sem):
    pltpu.make_async_copy(src_hbm.at[pl.ds(0, n * SLAB), :], dst.at[pl.ds(0, n * SLAB), :], sem).wait()


def _experts_kernel(be_ref, nu_ref, first_ref, wslot_ref, nxe_ref, hasn_ref, *refs, TB, l):
    look = GATHER_SLOTS - 1
    tok0_ref, tokl_ref = refs[:2]
    h_hbm, wg_hbm, wu_hbm, wd_hbm, y_ref, hbuf, sem, wfg, wfu, wfd, wsem = refs[2:]
    b = pl.program_id(0)
    slot = b % GATHER_SLOTS
    slot2 = (b + look) % GATHER_SLOTS
    n_used = nu_ref[0]
    SW = h_hbm.shape[1]

    def weight_copies(e, ws):
        cps = []
        for src, dst in ((wg_hbm, wfg), (wu_hbm, wfu), (wd_hbm, wfd)):
            rows = src.shape[2] // WEIGHT_CHUNKS
            for c in range(WEIGHT_CHUNKS):
                cps.append(pltpu.make_async_copy(src.at[l, e, pl.ds(c * rows, rows)], dst.at[ws, pl.ds(c * rows, rows)],
                                                 wsem.at[ws]))
        return cps

    @pl.when(b == 0)
    def _():
        for j in range(look):
            _gather_slabs(tok0_ref.at[j, 0], TB, h_hbm, hbuf.at[j], sem.at[j])
        for n, cp in enumerate(weight_copies(be_ref[0], 0)):
            cp.start(priority=n % 2)

    @pl.when((first_ref[b] == 1) & (b < n_used))
    def _():
        ws = wslot_ref[b]

        @pl.when(hasn_ref[b] == 1)
        def _():
            for n, cp in enumerate(weight_copies(nxe_ref[b], 1 - ws)):
                cp.start(priority=n % 2)

        for cp in weight_copies(be_ref[b], ws):
            cp.wait()

    @pl.when(b < n_used)
    def _():
        _wait_slabs(TB, h_hbm, hbuf.at[slot], sem.at[slot])
        ws = wslot_ref[b]
        nxt = (tokl_ref.at[0, 0], h_hbm, hbuf.at[slot2], sem.at[slot2])
        burst = TB // (2 * SLAB)
        issued = 0
        half = SLAB * SW
        a = u = None
        for s in range(0, SLAB, 2):
            w0 = hbuf[slot, pl.ds(s, TB, stride=PITCH), :]
            w1 = hbuf[slot, pl.ds(s + 1, TB, stride=PITCH), :]
            for unpack, base in ((_unpack_lo, 0), (_unpack_hi, half)):
                xs = jnp.concatenate([unpack(w0), unpack(w1)], axis=1).astype(BF16)
                k0 = base + s * SW
                da = _dot(xs, wfg[ws, k0:k0 + 2 * SW, :].astype(BF16))
                _issue_slabs(nxt[0], issued, issued + burst, *nxt[1:])
                du = _dot(xs, wfu[ws, k0:k0 + 2 * SW, :].astype(BF16))
                _issue_slabs(nxt[0], issued + burst, issued + 2 * burst, *nxt[1:])
                issued += 2 * burst
                a = da if a is None else a + da
                u = du if u is None else u + du
        act = (a * _sigmoid(a) * u).astype(BF16)
        _store_slabs(y_ref, _pack_halves(_dot(act, wfd[ws].astype(BF16))), TB)

        @pl.when(b + look >= n_used)
        def _():
            _wait_slabs(TB, h_hbm, hbuf.at[slot2], sem.at[slot2])

        for j in range(1, look):
            @pl.when((b == 0) & (n_used <= j))
            def _():
                _wait_slabs(TB, h_hbm, hbuf.at[j], sem.at[j])

    @pl.when(b >= n_used)
    def _():
        y_ref[...] = jnp.zeros_like(y_ref)


def _experts(h2s, tok_rows, tables, wg, wu, wd, l):
    SW = h2s.shape[1]
    D = 2 * SW * SLAB
    nblk = tables[0].shape[0]
    TB = EXPERT_TILE
    FF = wg.shape[3]
    tok3 = tok_rows.reshape(nblk, 1, TB)
    look = GATHER_SLOTS - 1
    grid_spec = pltpu.PrefetchScalarGridSpec(
        num_scalar_prefetch=len(tables),
        grid=(nblk,),
        in_specs=[pl.BlockSpec((look, 1, TB), lambda b, *_: (0, 0, 0), memory_space=pltpu.SMEM),
                  pl.BlockSpec((1, 1, TB), lambda b, *_: (jnp.minimum(b + look, nblk - 1), 0, 0), memory_space=pltpu.SMEM),
                  pl.BlockSpec(memory_space=pl.ANY), pl.BlockSpec(memory_space=pl.ANY),
                  pl.BlockSpec(memory_space=pl.ANY), pl.BlockSpec(memory_space=pl.ANY)],
        out_specs=pl.BlockSpec((TB * SLAB, SW), lambda b, *_: (b, 0)),
        scratch_shapes=[pltpu.VMEM((GATHER_SLOTS, TB * PITCH, SW), jnp.uint32), pltpu.SemaphoreType.DMA((GATHER_SLOTS,)),
                        pltpu.VMEM((2, D, FF), F32), pltpu.VMEM((2, D, FF), F32), pltpu.VMEM((2, FF, D), F32),
                        pltpu.SemaphoreType.DMA((2,))],
    )
    return pl.pallas_call(
        functools.partial(_experts_kernel, TB=TB, l=l),
        grid_spec=grid_spec,
        out_shape=jax.ShapeDtypeStruct((nblk * TB * SLAB, SW), jnp.uint32),
        compiler_params=_cp(("arbitrary",)),
        name="experts",
    )(*tables, tok3, tok3, h2s, wg, wu, wd)


def _combine_kernel(slc_ref, sln_ref, y_hbm, x_ref, h_ref, wt_ref, gt_ref, swg_ref, swu_ref, swd_ref, o_ref,
                    ybuf, sem, *, tm, n_ctx, nt):
    i = pl.program_id(0)
    slot = i % 2
    SW = y_hbm.shape[1]

    @pl.when(i == 0)
    def _():
        for k in range(TOP_K):
            _gather_slabs(slc_ref.at[0, k], tm, y_hbm, ybuf.at[0, k], sem.at[0])

    hb = h_ref[...]
    a = _dot(hb, swg_ref[...])
    u = _dot(hb, swu_ref[...])
    shared = _dot((a * _sigmoid(a) * u).astype(BF16), swd_ref[...])
    gate = _pick_mod(gt_ref, _ctx_rows(i, tm, n_ctx))
    wk = [jnp.broadcast_to(wt_ref[:, k:k + 1], (tm, SW)) for k in range(TOP_K)]
    for k in range(TOP_K):
        _wait_slabs(tm, y_hbm, ybuf.at[slot, k], sem.at[slot])
    half = SLAB * SW
    burst = tm // SLAB
    for s in range(SLAB):
        lo = slice(s * SW, (s + 1) * SW)
        hi = slice(half + s * SW, half + (s + 1) * SW)
        acc_lo = shared[:, lo]
        acc_hi = shared[:, hi]
        for k in range(TOP_K):
            w = ybuf[slot, k, pl.ds(s, tm, stride=PITCH), :]
            acc_lo = acc_lo + wk[k] * _unpack_lo(w)
            acc_hi = acc_hi + wk[k] * _unpack_hi(w)
            _issue_slabs(sln_ref.at[0, k], s * burst, (s + 1) * burst, y_hbm, ybuf.at[1 - slot, k], sem.at[1 - slot])
        o_ref[:, lo] = x_ref[:, lo] + gate[:, lo] * acc_lo
        o_ref[:, hi] = x_ref[:, hi] + gate[:, hi] * acc_hi

    @pl.when(i + 1 >= nt)
    def _():
        for k in range(TOP_K):
            _wait_slabs(tm, y_hbm, ybuf.at[1 - slot, k], sem.at[1 - slot])


def _combine(y2, slot_rows, wt_t, x, h2, mods, l, swg, swu, swd, n_ctx, tm, drop_ctx):
    T, D = x.shape
    SW = y2.shape[1]
    nt = T // tm
    FF = swg.shape[1]
    sl3 = slot_rows.reshape(TOP_K, nt, tm).transpose(1, 0, 2)
    skip = n_ctx // tm if drop_ctx else 0
    return pl.pallas_call(
        functools.partial(_combine_kernel, tm=tm, n_ctx=n_ctx, nt=nt),
        grid=(nt,),
        in_specs=[pl.BlockSpec((1, TOP_K, tm), lambda i: (i, 0, 0), memory_space=pltpu.SMEM),
                  pl.BlockSpec((1, TOP_K, tm), lambda i: (jnp.minimum(i + 1, nt - 1), 0, 0),
                               memory_space=pltpu.SMEM),
                  pl.BlockSpec(memory_space=pl.ANY),
                  pl.BlockSpec((tm, D), lambda i: (i, 0)),
                  pl.BlockSpec((tm, D), lambda i: (i, 0)),
                  pl.BlockSpec((tm, TOP_K), lambda i: (i, 0)),
                  pl.BlockSpec((1, 8, D), lambda i: (l, 0, 5)),
                  pl.BlockSpec((D, FF), lambda i: (0, 0)),
                  pl.BlockSpec((D, FF), lambda i: (0, 0)),
                  pl.BlockSpec((FF, D), lambda i: (0, 0))],
        out_specs=pl.BlockSpec((tm, D), lambda i: (jnp.maximum(i - skip, 0), 0)),
        out_shape=jax.ShapeDtypeStruct((T - skip * tm, D), F32),
        scratch_shapes=[pltpu.VMEM((2, TOP_K, tm * PITCH, SW), jnp.uint32), pltpu.SemaphoreType.DMA((2,))],
        compiler_params=_cp(("arbitrary",)),
        name="combine",
    )(sl3, sl3, y2, x, h2, wt_t, mods, swg, swu, swd)


def _seg_indicator(width, seg):
    idx = np.arange(width) // seg
    return jnp.asarray(idx[:, None] == idx[None, :], dtype=BF16)


def _seg_columns(width, seg):
    assert width // seg <= 128
    return jnp.asarray((np.arange(width) // seg)[:, None] == np.arange(128)[None, :], dtype=BF16)


def _rope_tables(n_ctx, n_lat):
    rows = n_lat // GRID_W
    row = jnp.repeat(jnp.arange(rows, dtype=F32), GRID_W)
    col = jnp.tile(jnp.arange(GRID_W, dtype=F32), rows)
    n_freq = ATTN_HEAD // 4
    inv_freq = ROPE_THETA ** (-jnp.arange(n_freq, dtype=F32) / n_freq)
    ar, ac = row[:, None] * inv_freq, col[:, None] * inv_freq
    cos = jnp.concatenate([jnp.cos(ar), jnp.cos(ar), jnp.cos(ac), jnp.cos(ac)], axis=1)
    sin = jnp.concatenate([-jnp.sin(ar), jnp.sin(ar), -jnp.sin(ac), jnp.sin(ac)], axis=1)
    cos = jnp.concatenate([jnp.ones((n_ctx, ATTN_HEAD), F32), cos], axis=0)
    sin = jnp.concatenate([jnp.zeros((n_ctx, ATTN_HEAD), F32), sin], axis=0)
    return jnp.tile(cos, (1, 2)), jnp.tile(sin, (1, 2))


def kernel(x, c, ctx, c_ctx, w_ada, b_ada, norm1_g, norm2_g, w_in, w_o, rw_mu, rw_w0, rw_w2, rw_a0, rw_a2, rw_g2, rw_kk, rw_ka, rw_rk, rw_ln_g, rw_ln_b, gm_norm_g, gm_ws, gm_b, at_qn, at_kn, at_sink, moe_router, moe_bias, moe_wg, moe_wu, moe_wd, sh_wg, sh_wu, sh_wd):
    B, S, D = x.shape
    C = ctx.shape[1]
    assert B == 1
    L = w_ada.shape[0]
    T = C + S
    RW = rw_kk.shape[1]
    GW = gm_norm_g.shape[1]
    QW = at_sink.shape[1] * ATTN_HEAD
    KW = QW // ATTN_GROUP
    rw_proj = rw_mu.shape[2]
    RWP = 2048
    assert rw_proj <= RWP and T % 768 == 0 and C % 256 == 0 and S % 256 == 0 and D == 2 * SLAB * 128

    xs = jnp.concatenate([ctx[0], x[0]], axis=0)
    cond8 = jnp.zeros((8, D), F32).at[0].set(c_ctx).at[1].set(c[0])
    mods = _ada(cond8, w_ada, b_ada)

    gh_rw = _seg_indicator(RW, RWKV_HEAD)
    gh_q = _seg_columns(QW, ATTN_HEAD)
    gh_k = _seg_columns(KW, ATTN_HEAD)
    cos, sin = _rope_tables(C, S)
    tri = jnp.asarray(np.arange(256)[:, None] < np.arange(256)[None, :], dtype=BF16)
    TB = EXPERT_TILE
    nblk = -(-(T * TOP_K) // TB) + N_EXPERTS

    for l in range(L):
        w_in_l = w_in[l]
        w_in_p = jnp.concatenate([w_in_l[:, :rw_proj], jnp.zeros((D, RWP - rw_proj), F32), w_in_l[:, rw_proj:]],
                                 axis=1).astype(BF16)
        P = _inproj(xs, mods, l, norm1_g[l][None], w_in_p, C, 768)
        gm_block = RWP // (2 * GW)
        q_block = (RWP + 2 * GW) // QW
        k_block = (RWP + 2 * GW + QW) // KW
        v_block = k_block + 1

        pad = lambda a: jnp.pad(a, ((0, 0), (0, RWP - rw_proj)))
        prm = {"mu": pad(rw_mu[l]), "w0": rw_w0[l], "w2": rw_w2[l], "a0": rw_a0[l], "a2": rw_a2[l],
               "g2": rw_g2[l], "kk": rw_kk[l][None], "ka": rw_ka[l][None], "rk": rw_rk[l].reshape(1, RW)}
        r, v, kk, lwf, kf, bf, lwb, kb, bb, gate, bonus = _rwkv_prep(P, prm, gh_rw, C, 256)
        yf, yb = _rwkv_scan(r, v, kk, lwf, kf, bf, lwb, kb, bb, C)
        y_rw = _rwkv_post(yf, yb, bonus, gate, rw_ln_g[l][None], rw_ln_b[l][None], gh_rw, 256)

        gm_bias = jnp.repeat(gm_b[l].T, GW // GM_GROUPS, axis=1)
        y_gm = _gmlp(P, gm_block, gm_norm_g[l][None], gm_ws[l].astype(BF16), gm_bias, 256)

        qg = jnp.tile(at_qn[l], QW // ATTN_HEAD)[None]
        kg = jnp.tile(at_kn[l], KW // ATTN_HEAD)[None]
        qr, kr = _qk_prep(P, q_block, k_block, cos, sin, qg, kg, gh_q, gh_k, 256)
        y_at = _attention(qr, kr, P, v_block, at_sink[l], C)

        xs = _oproj(xs, mods, l, y_rw, y_gm, y_at, w_o[l].astype(BF16), C, 768)

        bias_col = jnp.broadcast_to(moe_bias[l][:, None], (N_EXPERTS, 128))
        h2, h2s, idx8, wt8, rank8, cnt = _router(xs, mods, l, norm2_g[l][None], moe_router[l].T, bias_col, tri, C, 256)
        counts = cnt[:, 0].astype(jnp.int32)
        padded = (counts + TB - 1) // TB * TB
        pad_end = jnp.cumsum(padded)
        pad_start = pad_end - padded
        e_ids = jnp.arange(N_EXPERTS, dtype=jnp.int32)
        slot8 = jnp.sum(jnp.where(idx8[:, :, None] == e_ids, pad_start, 0), axis=-1) + rank8
        tok = jnp.zeros((nblk * TB,), jnp.int32).at[slot8.reshape(-1)].set(
            jnp.tile(jnp.arange(T, dtype=jnp.int32), TOP_K), unique_indices=True)
        blk_pos = jnp.arange(nblk, dtype=jnp.int32) * TB
        block_e = jnp.minimum(jnp.sum((pad_end[None, :] <= blk_pos[:, None]).astype(jnp.int32), axis=1), N_EXPERTS - 1)
        n_used = (pad_end[-1] // TB).astype(jnp.int32)
        first = (blk_pos < pad_end[-1]) & jnp.concatenate([jnp.ones((1,), jnp.bool_), block_e[1:] != block_e[:-1]])
        wslot = (jnp.cumsum(first.astype(jnp.int32)) - 1) % 2
        later = jnp.where((counts[None, :] > 0) & (e_ids[None, :] > e_ids[:, None]), e_ids[None, :], N_EXPERTS)
        nxt_of = jnp.min(later, axis=1)
        nxt_blk = jnp.sum(jnp.where(block_e[:, None] == e_ids, nxt_of, 0), axis=1)
        tables = (block_e, n_used.reshape(1), first.astype(jnp.int32), wslot.astype(jnp.int32),
                  jnp.minimum(nxt_blk, N_EXPERTS - 1).astype(jnp.int32), (nxt_blk < N_EXPERTS).astype(jnp.int32))
        y2 = _experts(h2s, tok * SLAB, tables, moe_wg, moe_wu, moe_wd, l)
        xs = _combine(y2, slot8 * SLAB, wt8.T, xs, h2, mods, l, sh_wg[l].astype(BF16), sh_wu[l].astype(BF16),
                      sh_wd[l].astype(BF16), C, 128, drop_ctx=(l == L - 1))
    return xs.reshape(B, S, D)
```
